```python
import functools
import jax
import jax.numpy as jnp
from jax import lax
import numpy as np

D_MODEL = 1024
BATCH = 32
SEQ = 2048
DEPTH = 2

GRID_W = 64
CTX_LEN = 256
EPS = 1e-6

N_BRANCH = 4
BR_W = 512
CONV_K = 31
RET_HEADS = 4
RET_DIM = BR_W // RET_HEADS
MLSTM_HEADS = 4
MLSTM_DIM = BR_W // MLSTM_HEADS
MLSTM_CONV_K = 3
MLSTM_GATES = 4 * MLSTM_HEADS
ATTN_HEADS = 8
ATTN_KV_HEADS = 2
ATTN_DIM = BR_W // ATTN_HEADS
WINDOW = 128
BLOCK = 128
CHUNK = 128
ROPE_BASE = 10000.0
NEG_INIT = -1e30
PEER_HEADS = 8
PEER_KEYS = 128
PEER_EXPERTS = PEER_KEYS * PEER_KEYS
PEER_QDIM = 256
PEER_TOPK = 16
PEER_TOKENS = 128

CONV_IN = 2 * BR_W
RET_IN = 4 * BR_W
MLSTM_IN = 4 * BR_W + MLSTM_GATES
ATTN_IN = ATTN_HEADS * ATTN_DIM + 2 * ATTN_KV_HEADS * ATTN_DIM
GATE_IN = N_BRANCH * D_MODEL
IN_SPLITS = (CONV_IN, RET_IN, MLSTM_IN, ATTN_IN, GATE_IN)
P_IN = CONV_IN + RET_IN + MLSTM_IN + ATTN_IN + GATE_IN

kernel_name = "hybrid_gated_dit_peer_block"


def split_cols(a, sizes):
    cuts = [int(s) for s in np.cumsum(sizes)[:-1]]
    return jnp.split(a, cuts, axis=-1)


def heads(a, n):
    return a.reshape(a.shape[:-1] + (n, a.shape[-1] // n))


def rmsnorm(x, g):
    xf = x.astype(jnp.float32)
    y = xf * lax.rsqrt(jnp.mean(xf * xf, axis=-1, keepdims=True) + EPS)
    return (y * g.astype(jnp.float32)).astype(x.dtype)


def _standardize(x):
    xf = x.astype(jnp.float32)
    xc = xf - jnp.mean(xf, axis=-1, keepdims=True)
    return xc * lax.rsqrt(jnp.mean(xc * xc, axis=-1, keepdims=True) + EPS)


def layernorm(x, g, b):
    return (_standardize(x) * g.astype(jnp.float32) + b.astype(jnp.float32)).astype(x.dtype)


def head_norm(y):
    z = _standardize(y)
    return z.reshape(z.shape[:-2] + (-1,))


def modulate(x, g, shift, scale):
    return rmsnorm(x, g) * (1.0 + scale) + shift


def axial_rope(x, rows, cols):
    d = x.shape[-1]
    nf = d // 4
    freqs = ROPE_BASE ** (-jnp.arange(nf, dtype=jnp.float32) / nf)

    def rot(xh, pos):
        ang = pos.astype(jnp.float32)[:, None] * freqs[None, :]
        cos = jnp.cos(ang)[None, :, None, :]
        sin = jnp.sin(ang)[None, :, None, :]
        x1, x2 = xh[..., :nf], xh[..., nf:]
        return jnp.concatenate([x1 * cos - x2 * sin, x2 * cos + x1 * sin], axis=-1)

    xf = x.astype(jnp.float32)
    out = jnp.concatenate([rot(xf[..., : d // 2], rows), rot(xf[..., d // 2:], cols)], axis=-1)
    return out.astype(x.dtype)


def depthwise_conv(x, w, b):
    k = w.shape[0]
    y = lax.conv_general_dilated(
        x, w[:, None, :].astype(x.dtype), window_strides=(1,),
        padding=[((k - 1) // 2, k // 2)],
        dimension_numbers=('NWC', 'WIO', 'NWC'),
        feature_group_count=x.shape[-1])
    return y + b.astype(x.dtype)


def retention_scan(q, k, v, s0, log_gamma):
    bsz, t, h, d = q.shape
    nc, L = t // CHUNK, CHUNK
    idx = jnp.arange(L, dtype=jnp.float32)
    diff = idx[:, None] - idx[None, :]
    lower = diff >= 0
    dmat = jnp.where(lower, jnp.exp(jnp.where(lower, diff, 0.0)[None] * log_gamma[:, None, None]), 0.0)
    q_dec = jnp.exp((idx + 1.0)[None, :] * log_gamma[:, None])
    k_dec = jnp.exp((L - 1.0 - idx)[None, :] * log_gamma[:, None])
    c_dec = jnp.exp(L * log_gamma)
    chunks = lambda a: a.reshape(bsz, nc, L, h, d).transpose(1, 0, 3, 2, 4)

    def step(s, inp):
        qc, kc, vc = inp
        att = jnp.einsum('bhid,bhjd->bhij', qc, kc) * dmat
        o = jnp.einsum('bhij,bhjd->bhid', att, vc) + q_dec[..., None] * jnp.einsum('bhid,bhde->bhie', qc, s)
        s = c_dec[:, None, None] * s + jnp.einsum('bhjd,bhje->bhde', kc * k_dec[..., None], vc)
        return s, o

    s, o = lax.scan(step, s0, (chunks(q), chunks(k), chunks(v)))
    return o.transpose(1, 0, 3, 2, 4).reshape(bsz, t, h, d), s


def mlstm_scan(q, k, v, log_i, log_f, state0):
    bsz, t, h, d = q.shape
    nc, L = t // CHUNK, CHUNK
    seq_chunks = lambda a: a.reshape(bsz, nc, L, h, d).transpose(1, 0, 3, 2, 4)
    gate_chunks = lambda a: a.reshape(bsz, nc, L, h).transpose(1, 0, 3, 2)
    tril = jnp.tril(jnp.ones((L, L), dtype=bool))

    def step(carry, inp):
        cmat, nvec, m = carry
        qc, kc, vc, ic, fc = inp
        b = jnp.cumsum(fc, axis=-1)
        logw = jnp.where(tril, b[..., :, None] - b[..., None, :] + ic[..., None, :], -jnp.inf)
        inter = b + m[..., None]
        m_t = jnp.maximum(inter, jnp.max(logw, axis=-1))
        s = jnp.einsum('bhid,bhjd->bhij', qc, kc) * jnp.exp(logw - m_t[..., None])
        w_prev = jnp.exp(inter - m_t)
        num = jnp.einsum('bhij,bhjd->bhid', s, vc) + w_prev[..., None] * jnp.einsum('bhid,bhde->bhie', qc, cmat)
        den = jnp.sum(s, axis=-1) + w_prev * jnp.einsum('bhid,bhd->bhi', qc, nvec)
        hc = num / jnp.maximum(jnp.abs(den), jnp.exp(-m_t))[..., None]
        b_end = b[..., -1]
        log_wk = b_end[..., None] - b + ic
        m_new = jnp.maximum(b_end + m, jnp.max(log_wk, axis=-1))
        decay = jnp.exp(b_end + m - m_new)
        kw = kc * jnp.exp(log_wk - m_new[..., None])[..., None]
        cmat = decay[..., None, None] * cmat + jnp.einsum('bhjd,bhje->bhde', kw, vc)
        nvec = decay[..., None] * nvec + jnp.sum(kw, axis=2)
        return (cmat, nvec, m_new), hc

    state, hs = lax.scan(step, state0, (seq_chunks(q), seq_chunks(k), seq_chunks(v),
                                        gate_chunks(log_i), gate_chunks(log_f)))
    return hs.transpose(1, 0, 3, 2, 4).reshape(bsz, t, h, d), state


def run_bidir(scan_f, scan_b, ctx_f, lat_f, ctx_b, lat_b, state0):
    flip = lambda arrs: tuple(jnp.flip(a, axis=1) for a in arrs)
    oc_f, st_f = scan_f(*ctx_f, state0)
    ol_f, _ = scan_f(*lat_f, st_f)
    oc_b, st_b = scan_b(*flip(ctx_b), state0)
    ol_b, _ = scan_b(*flip(lat_b), st_b)
    return oc_f + jnp.flip(oc_b, axis=1), ol_f + jnp.flip(ol_b, axis=1)


def softmax_with_sink(s, sink):
    sk = jnp.broadcast_to(sink[None, :, :, None, None], s.shape[:-1] + (1,))
    return jax.nn.softmax(jnp.concatenate([s, sk], axis=-1), axis=-1)[..., :-1]


def window_attention(q, k, v, k_ctx, v_ctx, sink):
    dt = q.dtype
    bsz, s_len, hq, d = q.shape
    hkv = k.shape[2]
    g = hq // hkv
    nb = s_len // BLOCK
    scale = d ** -0.5
    sink_l = sink.astype(jnp.float32).reshape(hkv, g)
    qb = q.reshape(bsz, nb, BLOCK, hkv, g, d).transpose(1, 0, 2, 3, 4, 5)

    def band(a):
        ap = jnp.pad(a, ((0, 0), (BLOCK, BLOCK), (0, 0), (0, 0))).reshape(bsz, nb + 2, BLOCK, hkv, d)
        return jnp.concatenate([ap[:, :-2], ap[:, 1:-1], ap[:, 2:]], axis=2).transpose(1, 0, 2, 3, 4)

    kb, vb = band(k), band(v)
    blk = jnp.arange(nb)[:, None, None]
    qpos = blk * BLOCK + jnp.arange(BLOCK)[None, :, None]
    kpos = (blk - 1) * BLOCK + jnp.arange(3 * BLOCK)[None, None, :]
    mask = (jnp.abs(qpos - kpos) <= WINDOW) & (kpos >= 0) & (kpos < s_len)
    n_loc = 3 * BLOCK

    def block(args):
        qi, ki, vi, mi = args
        s_loc = jnp.einsum('bqhgd,bkhd->bhgqk', qi, ki).astype(jnp.float32) * scale
        s_loc = jnp.where(mi[None, None, None], s_loc, -jnp.inf)
        s_ctx = jnp.einsum('bqhgd,bkhd->bhgqk', qi, k_ctx).astype(jnp.float32) * scale
        p = softmax_with_sink(jnp.concatenate([s_loc, s_ctx], axis=-1), sink_l)
        return (jnp.einsum('bhgqk,bkhd->bqhgd', p[..., :n_loc].astype(dt), vi)
                + jnp.einsum('bhgqk,bkhd->bqhgd', p[..., n_loc:].astype(dt), v_ctx))

    o = lax.map(block, (qb, kb, vb, mask))
    return o.transpose(1, 0, 2, 3, 4, 5).reshape(bsz, s_len, hq * d)


def context_attention(q, k, v, sink):
    dt = q.dtype
    bsz, n, hq, d = q.shape
    hkv = k.shape[2]
    g = hq // hkv
    s = jnp.einsum('bqhgd,bkhd->bhgqk', q.reshape(bsz, n, hkv, g, d), k).astype(jnp.float32) * d ** -0.5
    p = softmax_with_sink(s, sink.astype(jnp.float32).reshape(hkv, g)).astype(dt)
    return jnp.einsum('bhgqk,bkhd->bqhgd', p, v).reshape(bsz, n, hq * d)


def merge(ys, gates_pre, w_branch, w_out):
    gates = jnp.split(gates_pre, N_BRANCH, axis=-1)
    acc = jax.nn.sigmoid(gates[0]) * (ys[0] @ w_branch[0])
    for i in range(1, N_BRANCH):
        acc = acc + jax.nn.sigmoid(gates[i]) * (ys[i] @ w_branch[i])
    return acc @ w_out


def token_mixer(h, hc, rows, cols, w_in, conv_w, conv_b, conv_ln_g, conv_ln_b, ret_decay,
                mlstm_conv_w, mlstm_conv_b, mlstm_gate_b, attn_sink, w_branch, w_out, with_ctx_out):
    dt = h.dtype
    bsz = h.shape[0]
    conv_l, ret_l, mls_l, att_l, gate_l = split_cols(h @ w_in, IN_SPLITS)
    conv_c, ret_c, mls_c, att_c, gate_c = split_cols(hc @ w_in, IN_SPLITS)

    def conformer_conv(u):
        a, g = jnp.split(u, 2, axis=-1)
        y = depthwise_conv(a * jax.nn.sigmoid(g), conv_w, conv_b)
        return jax.nn.silu(layernorm(y, conv_ln_g, conv_ln_b))

    def ret_inputs(u, rotate):
        q, k, v, g = jnp.split(u, 4, axis=-1)
        q, k, v = heads(q, RET_HEADS), heads(k, RET_HEADS), heads(v, RET_HEADS)
        if rotate:
            q, k = axial_rope(q, rows, cols), axial_rope(k, rows, cols)
        return (q.astype(jnp.float32), k.astype(jnp.float32) * RET_DIM ** -0.5, v.astype(jnp.float32)), g

    ret_in_l, ret_g_l = ret_inputs(ret_l, True)
    ret_in_c, ret_g_c = ret_inputs(ret_c, False)
    log_gamma = jax.nn.log_sigmoid(ret_decay.astype(jnp.float32))
    s0 = jnp.zeros((bsz, RET_HEADS, RET_DIM, RET_DIM), jnp.float32)
    ret_y_c, ret_y_l = run_bidir(functools.partial(retention_scan, log_gamma=log_gamma[0]),
                                 functools.partial(retention_scan, log_gamma=log_gamma[1]),
                                 ret_in_c, ret_in_l, ret_in_c, ret_in_l, s0)

    def ret_out(y, g):
        return head_norm(y).astype(dt) * jax.nn.silu(g)

    def mlstm_inputs(u):
        qk, v, o, gates = split_cols(u, (2 * BR_W, BR_W, BR_W, MLSTM_GATES))
        qk = jax.nn.silu(depthwise_conv(qk, mlstm_conv_w, mlstm_conv_b))
        q, k = jnp.split(qk, 2, axis=-1)
        q, k, v = (heads(a, MLSTM_HEADS).astype(jnp.float32) for a in (q, k, v))
        k = k * MLSTM_DIM ** -0.5
        gi_f, gf_f, gi_b, gf_b = jnp.split(
            gates.astype(jnp.float32) + mlstm_gate_b.reshape(-1).astype(jnp.float32), 4, axis=-1)
        fwd = (q, k, v, gi_f, jax.nn.log_sigmoid(gf_f))
        bwd = (q, k, v, gi_b, jax.nn.log_sigmoid(gf_b))
        return fwd, bwd, o

    mls_f_l, mls_b_l, mls_o_l = mlstm_inputs(mls_l)
    mls_f_c, mls_b_c, mls_o_c = mlstm_inputs(mls_c)
    m_state0 = (jnp.zeros((bsz, MLSTM_HEADS, MLSTM_DIM, MLSTM_DIM), jnp.float32),
                jnp.zeros((bsz, MLSTM_HEADS, MLSTM_DIM), jnp.float32),
                jnp.full((bsz, MLSTM_HEADS), NEG_INIT, jnp.float32))
    mls_y_c, mls_y_l = run_bidir(mlstm_scan, mlstm_scan, mls_f_c, mls_f_l, mls_b_c, mls_b_l, m_state0)

    def mlstm_out(y, o):
        return head_norm(y).astype(dt) * jax.nn.sigmoid(o)

    def attn_inputs(u):
        q, k, v = split_cols(u, (ATTN_HEADS * ATTN_DIM, ATTN_KV_HEADS * ATTN_DIM, ATTN_KV_HEADS * ATTN_DIM))
        return heads(q, ATTN_HEADS), heads(k, ATTN_KV_HEADS), heads(v, ATTN_KV_HEADS)

    q_l, k_l, v_l = attn_inputs(att_l)
    q_c, k_c, v_c = attn_inputs(att_c)
    q_l, k_l = axial_rope(q_l, rows, cols), axial_rope(k_l, rows, cols)
    att_y_l = window_attention(q_l, k_l, v_l, k_c, v_c, attn_sink)

    out_l = merge((conformer_conv(conv_l), ret_out(ret_y_l, ret_g_l), mlstm_out(mls_y_l, mls_o_l), att_y_l),
                  gate_l, w_branch, w_out)
    if not with_ctx_out:
        return out_l, None
    att_y_c = context_attention(q_c, k_c, v_c, attn_sink)
    out_c = merge((conformer_conv(conv_c), ret_out(ret_y_c, ret_g_c), mlstm_out(mls_y_c, mls_o_c), att_y_c),
                  gate_c, w_branch, w_out)
    return out_l, out_c


def peer_ffn(h, wq, keys, u, v):
    dt = h.dtype
    bsz, t, d = h.shape
    half = PEER_QDIM // 2
    tokens = h.reshape(-1, PEER_TOKENS, d)
    keys32 = keys.astype(jnp.float32)

    def block(xb):
        q = (xb @ wq).astype(jnp.float32).reshape(-1, PEER_HEADS, 2, half)
        s = jnp.einsum('nhcd,ckd->nhck', q, keys32)
        sv, si = lax.top_k(s, PEER_TOPK)
        cand = sv[:, :, 0, :, None] + sv[:, :, 1, None, :]
        cv, ci = lax.top_k(cand.reshape(cand.shape[:2] + (-1,)), PEER_TOPK)
        i1 = jnp.take_along_axis(si[:, :, 0], ci // PEER_TOPK, axis=-1)
        i2 = jnp.take_along_axis(si[:, :, 1], ci % PEER_TOPK, axis=-1)
        expert = i1 * PEER_KEYS + i2
        gate = jax.nn.softmax(cv, axis=-1)
        u_sel = jnp.take(u, expert, axis=0)
        v_sel = jnp.take(v, expert, axis=0)
        act = jax.nn.gelu(jnp.einsum('nd,nhkd->nhk', xb, u_sel).astype(jnp.float32), approximate=False)
        return jnp.einsum('nhk,nhkd->nd', (gate * act).astype(dt), v_sel)

    return lax.map(block, tokens).reshape(bsz, t, d)


def setup_inputs(seed: int = 0) -> dict:
    key = jax.random.key(seed)
    ks = jax.random.split(key, 26)
    nrm = lambda k, shape, s: jax.random.normal(k, shape, jnp.float32) * s
    D = D_MODEL
    ret_init = jnp.log(2.0 ** (5.0 + jnp.arange(RET_HEADS, dtype=jnp.float32)) - 1.0)
    f_init = jnp.linspace(3.0, 6.0, MLSTM_HEADS, dtype=jnp.float32)
    z_init = jnp.zeros((MLSTM_HEADS,), jnp.float32)
    gate_init = jnp.stack([z_init, f_init, z_init, f_init])
    return {
        "x": nrm(ks[0], (BATCH, SEQ, D), 1.0),
        "c": nrm(ks[1], (BATCH, D), 1.0),
        "ctx": nrm(ks[2], (BATCH, CTX_LEN, D), 1.0),
        "c_ctx": nrm(ks[3], (D,), 1.0),
        "mod_w": nrm(ks[4], (DEPTH, D, 6 * D), D ** -0.5),
        "mod_b": nrm(ks[5], (DEPTH, 6 * D), 0.02),
        "norm1_g": 1.0 + nrm(ks[6], (DEPTH, D), 0.02),
        "norm2_g": 1.0 + nrm(ks[7], (DEPTH, D), 0.02),
        "w_in": nrm(ks[8], (DEPTH, D, P_IN), D ** -0.5),
        "conv_w": nrm(ks[9], (DEPTH, CONV_K, BR_W), CONV_K ** -0.5),
        "conv_b": nrm(ks[10], (DEPTH, BR_W), 0.02),
        "conv_ln_g": 1.0 + nrm(ks[11], (DEPTH, BR_W), 0.02),
        "conv_ln_b": nrm(ks[12], (DEPTH, BR_W), 0.02),
        "ret_decay": ret_init + nrm(ks[13], (DEPTH, 2, RET_HEADS), 0.1),
        "mlstm_conv_w": nrm(ks[14], (DEPTH, MLSTM_CONV_K, 2 * BR_W), MLSTM_CONV_K ** -0.5),
        "mlstm_conv_b": nrm(ks[15], (DEPTH, 2 * BR_W), 0.02),
        "mlstm_gate_b": gate_init + nrm(ks[16], (DEPTH, 4, MLSTM_HEADS), 0.1),
        "attn_sink": nrm(ks[17], (DEPTH, ATTN_HEADS), 0.5),
        "w_branch": nrm(ks[18], (DEPTH, N_BRANCH, BR_W, D), BR_W ** -0.5),
        "w_out": nrm(ks[19], (DEPTH, D, D), D ** -0.5),
        "peer_wq": nrm(ks[20], (DEPTH, D, PEER_HEADS * PEER_QDIM), D ** -0.5),
        "peer_keys": nrm(ks[21], (DEPTH, 2, PEER_KEYS, PEER_QDIM // 2), (PEER_QDIM // 2) ** -0.5),
        "peer_u": nrm(ks[22], (DEPTH, PEER_EXPERTS, D), D ** -0.5),
        "peer_v": nrm(ks[23], (DEPTH, PEER_EXPERTS, D), PEER_HEADS ** -0.5),
        "final_g": 1.0 + nrm(ks[24], (D,), 0.02),
    }


def reference(x, c, ctx, c_ctx, mod_w, mod_b, norm1_g, norm2_g, w_in, conv_w, conv_b, conv_ln_g, conv_ln_b,
              ret_decay, mlstm_conv_w, mlstm_conv_b, mlstm_gate_b, attn_sink, w_branch, w_out,
              peer_wq, peer_keys, peer_u, peer_v, final_g):
    n_rows = x.shape[1] // GRID_W
    rows = jnp.repeat(jnp.arange(n_rows, dtype=jnp.int32), GRID_W)
    cols = jnp.tile(jnp.arange(GRID_W, dtype=jnp.int32), n_rows)
    s_lat = jax.nn.silu(c)
    s_ctx = jax.nn.silu(c_ctx)
    for l in range(DEPTH):
        last = l == DEPTH - 1
        mod_l = (s_lat @ mod_w[l] + mod_b[l])[:, None, :]
        mod_c = (s_ctx @ mod_w[l] + mod_b[l])[None, None, :]
        sh1, sc1, ga1, sh2, sc2, ga2 = jnp.split(mod_l, 6, axis=-1)
        csh1, csc1, cga1, csh2, csc2, cga2 = jnp.split(mod_c, 6, axis=-1)
        o_lat, o_ctx = token_mixer(
            modulate(x, norm1_g[l], sh1, sc1), modulate(ctx, norm1_g[l], csh1, csc1), rows, cols,
            w_in[l], conv_w[l], conv_b[l], conv_ln_g[l], conv_ln_b[l], ret_decay[l],
            mlstm_conv_w[l], mlstm_conv_b[l], mlstm_gate_b[l], attn_sink[l], w_branch[l], w_out[l],
            not last)
        x = x + ga1 * o_lat
        x = x + ga2 * peer_ffn(modulate(x, norm2_g[l], sh2, sc2), peer_wq[l], peer_keys[l], peer_u[l], peer_v[l])
        if not last:
            ctx = ctx + cga1 * o_ctx
            ctx = ctx + cga2 * peer_ffn(modulate(ctx, norm2_g[l], csh2, csc2),
                                        peer_wq[l], peer_keys[l], peer_u[l], peer_v[l])
    return rmsnorm(x, final_g)
```

```python
import functools

import jax
import jax.numpy as jnp
import numpy as np
from jax import lax
from jax.experimental import pallas as pl
from jax.experimental.pallas import tpu as pltpu

F32 = jnp.float32
BF16 = jnp.bfloat16

GRID_W = 64
EPS = 1e-6
N_BRANCH = 4
BR_W = 512
RET_HEADS = 4
MLSTM_HEADS = 4
MLSTM_GATES = 4 * MLSTM_HEADS
ATTN_HEADS = 8
ATTN_KV_HEADS = 2
ATTN_DIM = BR_W // ATTN_HEADS
WINDOW = 128
BLOCK = 128
CHUNK = 128
ROPE_BASE = 10000.0
NEG_INIT = -1e30
PEER_HEADS = 8
PEER_KEYS = 128
PEER_TOPK = 16

CONV_IN = 2 * BR_W
RET_IN = 4 * BR_W
MLSTM_QKVO = 4 * BR_W
ATTN_IN = ATTN_HEADS * ATTN_DIM + 2 * ATTN_KV_HEADS * ATTN_DIM
REF_MGATE_OFF = CONV_IN + RET_IN + MLSTM_QKVO
REF_ATTN_OFF = REF_MGATE_OFF + MLSTM_GATES
REF_GATE_OFF = REF_ATTN_OFF + ATTN_IN
OFF_CONV = 0
OFF_RET = OFF_CONV + CONV_IN
OFF_MLSTM = OFF_RET + RET_IN
OFF_GATE = OFF_MLSTM + MLSTM_QKVO
LANE = 128
PROJ_TILE_N = 1024


def _round_up(a, m):
    return (a + m - 1) // m * m


VMEM_LIMIT = 56 * 1024 * 1024


def _cparams(sem):
    return pltpu.CompilerParams(dimension_semantics=sem, vmem_limit_bytes=VMEM_LIMIT)


def _mod_kernel(s_ref, w_ref, b_ref, o_ref):
    s = s_ref[...]
    s = s * jax.nn.sigmoid(s)
    o_ref[...] = jnp.dot(s, w_ref[...], preferred_element_type=F32,
                         precision=lax.Precision.HIGHEST) + b_ref[...]


def mod_vectors(cond, w, b):
    r, d = cond.shape
    n = w.shape[1]
    tn = 512
    return pl.pallas_call(
        _mod_kernel,
        grid=(n // tn,),
        in_specs=[pl.BlockSpec((r, d), lambda j: (0, 0)),
                  pl.BlockSpec((d, tn), lambda j: (0, j)),
                  pl.BlockSpec((1, tn), lambda j: (0, j))],
        out_specs=pl.BlockSpec((r, tn), lambda j: (0, j)),
        out_shape=jax.ShapeDtypeStruct((r, n), F32),
        compiler_params=_cparams(("parallel",)),
        name="mod_vectors",
    )(cond, w, b.reshape(1, n))


def _norm_matmul_kernel(x_ref, g_ref, sh_ref, sc_ref, w_ref, o_ref, *rest, emit_h):
    h_ref = rest[-1]

    @pl.when(pl.program_id(2) == 0)
    def _():
        x = x_ref[0]
        y = x * lax.rsqrt(jnp.mean(x * x, axis=-1, keepdims=True) + EPS)
        h = (y * g_ref[...]) * (1.0 + sc_ref[0]) + sh_ref[0]
        h_ref[...] = h.astype(BF16)
        if emit_h:
            rest[0][0] = h.astype(BF16)

    o_ref[0] = jnp.dot(h_ref[...], w_ref[...], preferred_element_type=F32)


def norm_matmul(x, g, shift, scale, w, emit_h=False):
    bsz, t, d = x.shape
    n = w.shape[1]
    tm = min(t, 1024)
    tn = min(n, PROJ_TILE_N)
    out_shape = [jax.ShapeDtypeStruct((bsz, t, n), F32)]
    out_specs = [pl.BlockSpec((1, tm, tn), lambda b, i, j: (b, i, j))]
    if emit_h:
        out_shape.append(jax.ShapeDtypeStruct((bsz, t, d), BF16))
        out_specs.append(pl.BlockSpec((1, tm, d), lambda b, i, j: (b, i, 0)))
    res = pl.pallas_call(
        functools.partial(_norm_matmul_kernel, emit_h=emit_h),
        grid=(bsz, t // tm, n // tn),
        in_specs=[pl.BlockSpec((1, tm, d), lambda b, i, j: (b, i, 0)),
                  pl.BlockSpec((1, d), lambda b, i, j: (0, 0)),
                  pl.BlockSpec((1, 1, d), lambda b, i, j: (b, 0, 0)),
                  pl.BlockSpec((1, 1, d), lambda b, i, j: (b, 0, 0)),
                  pl.BlockSpec((d, tn), lambda b, i, j: (0, j))],
        out_specs=out_specs,
        out_shape=out_shape,
        scratch_shapes=[pltpu.VMEM((tm, d), BF16)],
        compiler_params=_cparams(("parallel", "parallel", "arbitrary")),
        name="norm_matmul",
    )(x, g.reshape(1, d), shift, scale, w)
    return res if emit_h else res[0]


def _merge_kernel(y0, y1, y2, y3, g0, g1, g2, g3, wb_ref, wo_ref, x_ref, ga_ref, o_ref):
    acc = None
    for i, (y, g) in enumerate(((y0, g0), (y1, g1), (y2, g2), (y3, g3))):
        p = jnp.dot(y[0].astype(BF16), wb_ref[i], preferred_element_type=F32)
        t = jax.nn.sigmoid(g[0]) * p
        acc = t if acc is None else acc + t
    o = jnp.dot(acc.astype(BF16), wo_ref[...], preferred_element_type=F32)
    o_ref[0] = x_ref[0] + ga_ref[0] * o


def merge(ys, proj, w_branch, w_out, x, ga):
    bsz, t, d = x.shape
    tm = min(t, 512)
    gate_blk = OFF_GATE // d
    y_spec = pl.BlockSpec((1, tm, BR_W), lambda b, i: (b, i, 0))
    g_specs = [pl.BlockSpec((1, tm, d), functools.partial(lambda b, i, k: (b, i, gate_blk + k), k=k))
               for k in range(N_BRANCH)]
    return pl.pallas_call(
        _merge_kernel,
        grid=(bsz, t // tm),
        in_specs=[y_spec] * N_BRANCH + g_specs + [
            pl.BlockSpec((N_BRANCH, BR_W, d), lambda b, i: (0, 0, 0)),
            pl.BlockSpec((d, d), lambda b, i: (0, 0)),
            pl.BlockSpec((1, tm, d), lambda b, i: (b, i, 0)),
            pl.BlockSpec((1, 1, d), lambda b, i: (b, 0, 0))],
        out_specs=pl.BlockSpec((1, tm, d), lambda b, i: (b, i, 0)),
        out_shape=jax.ShapeDtypeStruct((bsz, t, d), F32),
        compiler_params=_cparams(("parallel", "parallel")),
        name="merge",
    )(*ys, proj, proj, proj, proj, w_branch, w_out, x, ga)


PEER_TE = 512
PEER_ROWS = PEER_TE // PEER_KEYS


def _gelu_exact(x):
    return 0.5 * x * (1.0 + lax.erf(x * np.float32(1.0 / np.sqrt(2.0))))


def _peer_kernel(xm_ref, a_ref, lb_ref, b_ref, r2_ref, u_ref, vt_ref, x_ref, ga_ref, o_ref, acc_ref, w_ref):
    j = pl.program_id(2)

    @pl.when(j == 0)
    def _():
        acc_ref[...] = jnp.zeros_like(acc_ref)

    act = _gelu_exact(lax.dot_general(u_ref[...], xm_ref[0], (((1,), (1,)), ((), ())),
                                      preferred_element_type=F32))
    for a in range(PEER_ROWS):
        i1 = j * PEER_ROWS + a
        g = None
        for h in range(PEER_HEADS):
            sel = r2_ref[0, h] < lb_ref[0, h, pl.ds(i1, 1), :]
            t = jnp.where(sel, b_ref[0, h], 0.0) * a_ref[0, h, pl.ds(i1, 1), :]
            g = t if g is None else g + t
        w_ref[a * PEER_KEYS:(a + 1) * PEER_KEYS, :] = (g * act[a * PEER_KEYS:(a + 1) * PEER_KEYS, :]).astype(BF16)
    acc_ref[...] += jnp.dot(vt_ref[...], w_ref[...], preferred_element_type=F32)

    @pl.when(j == pl.num_programs(2) - 1)
    def _():
        o_ref[0] = x_ref[0] + ga_ref[0] * acc_ref[...].T


def peer_dense(xm, a_t, lb_t, b_t, r2_t, u, vt, x, ga):
    bsz, t, d = x.shape
    n_exp = u.shape[0]
    tn = min(t, 512)
    tab = pl.BlockSpec((1, PEER_HEADS, PEER_KEYS, tn), lambda b, i, j: (b, 0, 0, i))
    return pl.pallas_call(
        _peer_kernel,
        grid=(bsz, t // tn, n_exp // PEER_TE),
        in_specs=[pl.BlockSpec((1, tn, d), lambda b, i, j: (b, i, 0)),
                  tab, tab, tab, tab,
                  pl.BlockSpec((PEER_TE, d), lambda b, i, j: (j, 0)),
                  pl.BlockSpec((d, PEER_TE), lambda b, i, j: (0, j)),
                  pl.BlockSpec((1, tn, d), lambda b, i, j: (b, i, 0)),
                  pl.BlockSpec((1, 1, d), lambda b, i, j: (b, 0, 0))],
        out_specs=pl.BlockSpec((1, tn, d), lambda b, i, j: (b, i, 0)),
        out_shape=jax.ShapeDtypeStruct((bsz, t, d), F32),
        scratch_shapes=[pltpu.VMEM((d, tn), F32), pltpu.VMEM((PEER_TE, tn), BF16)],
        compiler_params=_cparams(("parallel", "parallel", "arbitrary")),
        name="peer_dense",
    )(xm, a_t, lb_t, b_t, r2_t, u, vt, x, ga)


def _rank_desc(s):
    gt = s[..., None, :] > s[..., :, None]
    idx = jnp.arange(s.shape[-1])
    tie = (s[..., None, :] == s[..., :, None]) & (idx[None, :] < idx[:, None])
    return jnp.sum((gt | tie).astype(jnp.int32), axis=-1)


def peer_tables(q, keys):
    bsz, t, _ = q.shape
    half = keys.shape[-1]

    def one(qb):
        qh = qb.reshape(t, PEER_HEADS, 2, half)
        s = jnp.einsum('nhcd,ckd->nhck', qh, keys, precision=lax.Precision.HIGHEST)
        rank = _rank_desc(s)
        pos = jnp.arange(PEER_TOPK)
        onehot = (rank[..., None, :] == pos[:, None]).astype(F32)
        sv = jnp.sum(onehot * s[..., None, :], axis=-1)
        cand = sv[:, :, 0, :, None] + sv[:, :, 1, None, :]
        cv, ci = lax.top_k(cand.reshape(t, PEER_HEADS, -1), PEER_TOPK)
        z = jnp.sum(jnp.exp(cv - cv[..., :1]), axis=-1, keepdims=True)
        cnt = jnp.sum((ci[..., None] // PEER_TOPK == pos).astype(F32), axis=-2)
        lb = jnp.sum(onehot[:, :, 0] * cnt[..., None], axis=-2)
        a = jnp.exp(s[:, :, 0] - sv[:, :, 0, :1])
        b = jnp.exp(s[:, :, 1] - sv[:, :, 1, :1]) / z
        r2 = jnp.minimum(rank[:, :, 1], PEER_TOPK).astype(F32)
        return tuple(jnp.transpose(v, (1, 2, 0)) for v in (a, lb, b, r2))

    return lax.map(one, q)


def _heads(a, n):
    return a.reshape(a.shape[:-1] + (n, a.shape[-1] // n))


def _standardize(x):
    xc = x - jnp.mean(x, axis=-1, keepdims=True)
    return xc * lax.rsqrt(jnp.mean(xc * xc, axis=-1, keepdims=True) + EPS)


def _head_norm(y):
    z = _standardize(y)
    return z.reshape(z.shape[:-2] + (-1,))


def _axial_rope(x, rows, cols):
    d = x.shape[-1]
    nf = d // 4
    freqs = ROPE_BASE ** (-jnp.arange(nf, dtype=F32) / nf)

    def rot(xh, pos):
        ang = pos.astype(F32)[:, None] * freqs[None, :]
        cos = jnp.cos(ang)[None, :, None, :]
        sin = jnp.sin(ang)[None, :, None, :]
        x1, x2 = xh[..., :nf], xh[..., nf:]
        return jnp.concatenate([x1 * cos - x2 * sin, x2 * cos + x1 * sin], axis=-1)

    return jnp.concatenate([rot(x[..., : d // 2], rows), rot(x[..., d // 2:], cols)], axis=-1)


def _depthwise_conv(x, w, b):
    k = w.shape[0]
    y = lax.conv_general_dilated(
        x, w[:, None, :], window_strides=(1,), padding=[((k - 1) // 2, k // 2)],
        dimension_numbers=('NWC', 'WIO', 'NWC'), feature_group_count=x.shape[-1])
    return y + b


def _retention_scan(q, k, v, s0, log_gamma):
    bsz, t, h, d = q.shape
    nc, L = t // CHUNK, CHUNK
    idx = jnp.arange(L, dtype=F32)
    diff = idx[:, None] - idx[None, :]
    lower = diff >= 0
    dmat = jnp.where(lower, jnp.exp(jnp.where(lower, diff, 0.0)[None] * log_gamma[:, None, None]), 0.0)
    q_dec = jnp.exp((idx + 1.0)[None, :] * log_gamma[:, None])
    k_dec = jnp.exp((L - 1.0 - idx)[None, :] * log_gamma[:, None])
    c_dec = jnp.exp(L * log_gamma)
    chunks = lambda a: a.reshape(bsz, nc, L, h, d).transpose(1, 0, 3, 2, 4)

    def step(s, inp):
        qc, kc, vc = inp
        att = jnp.einsum('bhid,bhjd->bhij', qc, kc) * dmat
        o = jnp.einsum('bhij,bhjd->bhid', att, vc) + q_dec[..., None] * jnp.einsum('bhid,bhde->bhie', qc, s)
        s = c_dec[:, None, None] * s + jnp.einsum('bhjd,bhje->bhde', kc * k_dec[..., None], vc)
        return s, o

    s, o = lax.scan(step, s0, (chunks(q), chunks(k), chunks(v)))
    return o.transpose(1, 0, 3, 2, 4).reshape(bsz, t, h, d), s


def _mlstm_scan(q, k, v, log_i, log_f, state0):
    bsz, t, h, d = q.shape
    nc, L = t // CHUNK, CHUNK
    seq_chunks = lambda a: a.reshape(bsz, nc, L, h, d).transpose(1, 0, 3, 2, 4)
    gate_chunks = lambda a: a.reshape(bsz, nc, L, h).transpose(1, 0, 3, 2)
    tril = jnp.tril(jnp.ones((L, L), dtype=bool))

    def step(carry, inp):
        cmat, nvec, m = carry
        qc, kc, vc, ic, fc = inp
        b = jnp.cumsum(fc, axis=-1)
        logw = jnp.where(tril, b[..., :, None] - b[..., None, :] + ic[..., None, :], -jnp.inf)
        inter = b + m[..., None]
        m_t = jnp.maximum(inter, jnp.max(logw, axis=-1))
        s = jnp.einsum('bhid,bhjd->bhij', qc, kc) * jnp.exp(logw - m_t[..., None])
        w_prev = jnp.exp(inter - m_t)
        num = jnp.einsum('bhij,bhjd->bhid', s, vc) + w_prev[..., None] * jnp.einsum('bhid,bhde->bhie', qc, cmat)
        den = jnp.sum(s, axis=-1) + w_prev * jnp.einsum('bhid,bhd->bhi', qc, nvec)
        hc = num / jnp.maximum(jnp.abs(den), jnp.exp(-m_t))[..., None]
        b_end = b[..., -1]
        log_wk = b_end[..., None] - b + ic
        m_new = jnp.maximum(b_end + m, jnp.max(log_wk, axis=-1))
        decay = jnp.exp(b_end + m - m_new)
        kw = kc * jnp.exp(log_wk - m_new[..., None])[..., None]
        cmat = decay[..., None, None] * cmat + jnp.einsum('bhjd,bhje->bhde', kw, vc)
        nvec = decay[..., None] * nvec + jnp.sum(kw, axis=2)
        return (cmat, nvec, m_new), hc

    state, hs = lax.scan(step, state0, (seq_chunks(q), seq_chunks(k), seq_chunks(v),
                                        gate_chunks(log_i), gate_chunks(log_f)))
    return hs.transpose(1, 0, 3, 2, 4).reshape(bsz, t, h, d), state


def _run_bidir(scan_f, scan_b, ctx_f, lat_f, ctx_b, lat_b, state0):
    flip = lambda arrs: tuple(jnp.flip(a, axis=1) for a in arrs)
    oc_f, st_f = scan_f(*ctx_f, state0)
    ol_f, _ = scan_f(*lat_f, st_f)
    oc_b, st_b = scan_b(*flip(ctx_b), state0)
    ol_b, _ = scan_b(*flip(lat_b), st_b)
    return oc_f + jnp.flip(oc_b, axis=1), ol_f + jnp.flip(ol_b, axis=1)


def _softmax_with_sink(s, sink):
    sk = jnp.broadcast_to(sink[None, :, :, None, None], s.shape[:-1] + (1,))
    return jax.nn.softmax(jnp.concatenate([s, sk], axis=-1), axis=-1)[..., :-1]


def _window_attention(q, k, v, k_ctx, v_ctx, sink):
    bsz, s_len, hq, d = q.shape
    hkv = k.shape[2]
    g = hq // hkv
    nb = s_len // BLOCK
    scale = d ** -0.5
    sink_l = sink.reshape(hkv, g)
    qb = q.reshape(bsz, nb, BLOCK, hkv, g, d).transpose(1, 0, 2, 3, 4, 5)

    def band(a):
        ap = jnp.pad(a, ((0, 0), (BLOCK, BLOCK), (0, 0), (0, 0))).reshape(bsz, nb + 2, BLOCK, hkv, d)
        return jnp.concatenate([ap[:, :-2], ap[:, 1:-1], ap[:, 2:]], axis=2).transpose(1, 0, 2, 3, 4)

    kb, vb = band(k), band(v)
    blk = jnp.arange(nb)[:, None, None]
    qpos = blk * BLOCK + jnp.arange(BLOCK)[None, :, None]
    kpos = (blk - 1) * BLOCK + jnp.arange(3 * BLOCK)[None, None, :]
    mask = (jnp.abs(qpos - kpos) <= WINDOW) & (kpos >= 0) & (kpos < s_len)
    n_loc = 3 * BLOCK

    def block(args):
        qi, ki, vi, mi = args
        s_loc = jnp.einsum('bqhgd,bkhd->bhgqk', qi, ki) * scale
        s_loc = jnp.where(mi[None, None, None], s_loc, -jnp.inf)
        s_ctx = jnp.einsum('bqhgd,bkhd->bhgqk', qi, k_ctx) * scale
        p = _softmax_with_sink(jnp.concatenate([s_loc, s_ctx], axis=-1), sink_l)
        return (jnp.einsum('bhgqk,bkhd->bqhgd', p[..., :n_loc], vi)
                + jnp.einsum('bhgqk,bkhd->bqhgd', p[..., n_loc:], v_ctx))

    o = lax.map(block, (qb, kb, vb, mask))
    return o.transpose(1, 0, 2, 3, 4, 5).reshape(bsz, s_len, hq * d)


def _context_attention(q, k, v, sink):
    bsz, n, hq, d = q.shape
    hkv = k.shape[2]
    g = hq // hkv
    s = jnp.einsum('bqhgd,bkhd->bhgqk', q.reshape(bsz, n, hkv, g, d), k) * d ** -0.5
    p = _softmax_with_sink(s, sink.reshape(hkv, g))
    return jnp.einsum('bhgqk,bkhd->bqhgd', p, v).reshape(bsz, n, hq * d)


def _mixer_branches(proj_l, proj_c, rows, cols, conv_w, conv_b, conv_ln_g, conv_ln_b, ret_decay,
                    mlstm_conv_w, mlstm_conv_b, mlstm_gate_b, attn_sink, with_ctx_out):
    bsz = proj_l.shape[0]
    off_attn = OFF_GATE + N_BRANCH * conv_ln_g.shape[0] * 2
    off_mgate = off_attn + ATTN_IN

    def conformer_conv(p):
        a, g = jnp.split(p[..., OFF_CONV:OFF_CONV + CONV_IN], 2, axis=-1)
        y = _depthwise_conv(a * jax.nn.sigmoid(g), conv_w, conv_b)
        return jax.nn.silu(_standardize(y) * conv_ln_g + conv_ln_b)

    def ret_inputs(p, rotate):
        q, k, v, g = jnp.split(p[..., OFF_RET:OFF_RET + RET_IN], 4, axis=-1)
        q, k, v = _heads(q, RET_HEADS), _heads(k, RET_HEADS), _heads(v, RET_HEADS)
        if rotate:
            q, k = _axial_rope(q, rows, cols), _axial_rope(k, rows, cols)
        return (q, k * (BR_W // RET_HEADS) ** -0.5, v), g

    ret_in_l, ret_g_l = ret_inputs(proj_l, True)
    ret_in_c, ret_g_c = ret_inputs(proj_c, False)
    log_gamma = jax.nn.log_sigmoid(ret_decay)
    rd = BR_W // RET_HEADS
    s0 = jnp.zeros((bsz, RET_HEADS, rd, rd), F32)
    ret_y_c, ret_y_l = _run_bidir(functools.partial(_retention_scan, log_gamma=log_gamma[0]),
                                  functools.partial(_retention_scan, log_gamma=log_gamma[1]),
                                  ret_in_c, ret_in_l, ret_in_c, ret_in_l, s0)
    ret_out = lambda y, g: _head_norm(y) * jax.nn.silu(g)

    def mlstm_inputs(p):
        u = p[..., OFF_MLSTM:OFF_MLSTM + MLSTM_QKVO]
        qk, v, o = u[..., :2 * BR_W], u[..., 2 * BR_W:3 * BR_W], u[..., 3 * BR_W:]
        gates = p[..., off_mgate:off_mgate + MLSTM_GATES]
        qk = jax.nn.silu(_depthwise_conv(qk, mlstm_conv_w, mlstm_conv_b))
        q, k = jnp.split(qk, 2, axis=-1)
        q, k, v = (_heads(a, MLSTM_HEADS) for a in (q, k, v))
        k = k * (BR_W // MLSTM_HEADS) ** -0.5
        gi_f, gf_f, gi_b, gf_b = jnp.split(gates + mlstm_gate_b.reshape(-1), 4, axis=-1)
        return (q, k, v, gi_f, jax.nn.log_sigmoid(gf_f)), (q, k, v, gi_b, jax.nn.log_sigmoid(gf_b)), o

    mls_f_l, mls_b_l, mls_o_l = mlstm_inputs(proj_l)
    mls_f_c, mls_b_c, mls_o_c = mlstm_inputs(proj_c)
    md = BR_W // MLSTM_HEADS
    m_state0 = (jnp.zeros((bsz, MLSTM_HEADS, md, md), F32), jnp.zeros((bsz, MLSTM_HEADS, md), F32),
                jnp.full((bsz, MLSTM_HEADS), NEG_INIT, F32))
    mls_y_c, mls_y_l = _run_bidir(_mlstm_scan, _mlstm_scan, mls_f_c, mls_f_l, mls_b_c, mls_b_l, m_state0)
    mlstm_out = lambda y, o: _head_norm(y) * jax.nn.sigmoid(o)

    def attn_inputs(p):
        u = p[..., off_attn:off_attn + ATTN_IN]
        nq, nkv = ATTN_HEADS * ATTN_DIM, ATTN_KV_HEADS * ATTN_DIM
        return (_heads(u[..., :nq], ATTN_HEADS), _heads(u[..., nq:nq + nkv], ATTN_KV_HEADS),
                _heads(u[..., nq + nkv:], ATTN_KV_HEADS))

    q_l, k_l, v_l = attn_inputs(proj_l)
    q_c, k_c, v_c = attn_inputs(proj_c)
    q_l, k_l = _axial_rope(q_l, rows, cols), _axial_rope(k_l, rows, cols)
    att_y_l = _window_attention(q_l, k_l, v_l, k_c, v_c, attn_sink)
    ys_l = (conformer_conv(proj_l), ret_out(ret_y_l, ret_g_l), mlstm_out(mls_y_l, mls_o_l), att_y_l)
    if not with_ctx_out:
        return ys_l, None
    att_y_c = _context_attention(q_c, k_c, v_c, attn_sink)
    ys_c = (conformer_conv(proj_c), ret_out(ret_y_c, ret_g_c), mlstm_out(mls_y_c, mls_o_c), att_y_c)
    return ys_l, ys_c


def _pack_w_in(w):
    d = w.shape[0]
    used = w.shape[1] - MLSTM_GATES + LANE
    cols = [w[:, :REF_MGATE_OFF], w[:, REF_GATE_OFF:], w[:, REF_ATTN_OFF:REF_GATE_OFF],
            w[:, REF_MGATE_OFF:REF_ATTN_OFF],
            jnp.zeros((d, _round_up(used, PROJ_TILE_N) - w.shape[1]), w.dtype)]
    return jnp.concatenate(cols, axis=1).astype(BF16)


def _final_norm(x, g):
    y = x * lax.rsqrt(jnp.mean(x * x, axis=-1, keepdims=True) + EPS)
    return y * g


def kernel(x, c, ctx, c_ctx, mod_w, mod_b, norm1_g, norm2_g, w_in, conv_w, conv_b, conv_ln_g, conv_ln_b,
           ret_decay, mlstm_conv_w, mlstm_conv_b, mlstm_gate_b, attn_sink, w_branch, w_out,
           peer_wq, peer_keys, peer_u, peer_v, final_g):
    bsz, seq, d = x.shape
    depth = mod_w.shape[0]
    n_rows = seq // GRID_W
    rows = jnp.repeat(jnp.arange(n_rows, dtype=jnp.int32), GRID_W)
    cols = jnp.tile(jnp.arange(GRID_W, dtype=jnp.int32), n_rows)
    n_cond = _round_up(bsz + 1, 8)
    cond = jnp.concatenate([c, c_ctx[None], jnp.zeros((n_cond - bsz - 1, d), F32)], axis=0)
    for l in range(depth):
        last = l == depth - 1
        mod = mod_vectors(cond, mod_w[l], mod_b[l])
        sh1, sc1, ga1, sh2, sc2, ga2 = (m[:, None, :] for m in jnp.split(mod[:bsz], 6, axis=-1))
        csh1, csc1, cga1, csh2, csc2, cga2 = (jnp.broadcast_to(m[None], (bsz, 1, d))
                                              for m in jnp.split(mod[bsz:bsz + 1], 6, axis=-1))
        w_in_p = _pack_w_in(w_in[l])
        proj_l = norm_matmul(x, norm1_g[l], sh1, sc1, w_in_p)
        proj_c = norm_matmul(ctx, norm1_g[l], csh1, csc1, w_in_p)
        ys_l, ys_c = _mixer_branches(proj_l, proj_c, rows, cols, conv_w[l], conv_b[l], conv_ln_g[l], conv_ln_b[l],
                                     ret_decay[l], mlstm_conv_w[l], mlstm_conv_b[l], mlstm_gate_b[l],
                                     attn_sink[l], not last)
        wb, wo = w_branch[l].astype(BF16), w_out[l].astype(BF16)
        wq = peer_wq[l].astype(BF16)
        u = peer_u[l].astype(BF16)
        vt = peer_v[l].T.astype(BF16)

        def peer(xx, sh, sc, ga):
            q, xm = norm_matmul(xx, norm2_g[l], sh, sc, wq, emit_h=True)
            tabs = peer_tables(q, peer_keys[l])
            return peer_dense(xm, *tabs, u, vt, xx, ga)

        x = merge(ys_l, proj_l, wb, wo, x, ga1)
        x = peer(x, sh2, sc2, ga2)
        if not last:
            ctx = merge(ys_c, proj_c, wb, wo, ctx, cga1)
            ctx = peer(ctx, csh2, csc2, cga2)
    return _final_norm(x, final_g)
```

```python
import functools

import jax
import jax.numpy as jnp
import numpy as np
from jax import lax
from jax.experimental import pallas as pl
from jax.experimental.pallas import tpu as pltpu

F32 = jnp.float32
BF16 = jnp.bfloat16

GRID_W = 64
EPS = 1e-6
N_BRANCH = 4
BR_W = 512
RET_HEADS = 4
MLSTM_HEADS = 4
MLSTM_GATES = 4 * MLSTM_HEADS
ATTN_HEADS = 8
ATTN_KV_HEADS = 2
ATTN_DIM = BR_W // ATTN_HEADS
WINDOW = 128
BLOCK = 128
CHUNK = 128
ROPE_BASE = 10000.0
NEG_INIT = -1e30
PEER_HEADS = 8
PEER_KEYS = 128
PEER_TOPK = 16

CONV_IN = 2 * BR_W
RET_IN = 4 * BR_W
MLSTM_QKVO = 4 * BR_W
ATTN_IN = ATTN_HEADS * ATTN_DIM + 2 * ATTN_KV_HEADS * ATTN_DIM
REF_MGATE_OFF = CONV_IN + RET_IN + MLSTM_QKVO
REF_ATTN_OFF = REF_MGATE_OFF + MLSTM_GATES
REF_GATE_OFF = REF_ATTN_OFF + ATTN_IN
OFF_CONV = 0
OFF_RET = OFF_CONV + CONV_IN
OFF_MLSTM = OFF_RET + RET_IN
OFF_GATE = OFF_MLSTM + MLSTM_QKVO
LANE = 128
PROJ_TILE_N = 1024


def _round_up(a, m):
    return (a + m - 1) // m * m


VMEM_LIMIT = 56 * 1024 * 1024


def _cparams(sem):
    return pltpu.CompilerParams(dimension_semantics=sem, vmem_limit_bytes=VMEM_LIMIT)


def _mod_kernel(s_ref, w_ref, b_ref, o_ref):
    s = s_ref[...]
    s = s * jax.nn.sigmoid(s)
    o_ref[...] = jnp.dot(s, w_ref[...], preferred_element_type=F32,
                         precision=lax.Precision.HIGHEST) + b_ref[...]


def mod_vectors(cond, w, b):
    r, d = cond.shape
    n = w.shape[1]
    tn = 512
    return pl.pallas_call(
        _mod_kernel,
        grid=(n // tn,),
        in_specs=[pl.BlockSpec((r, d), lambda j: (0, 0)),
                  pl.BlockSpec((d, tn), lambda j: (0, j)),
                  pl.BlockSpec((1, tn), lambda j: (0, j))],
        out_specs=pl.BlockSpec((r, tn), lambda j: (0, j)),
        out_shape=jax.ShapeDtypeStruct((r, n), F32),
        compiler_params=_cparams(("parallel",)),
        name="mod_vectors",
    )(cond, w, b.reshape(1, n))


def _norm_matmul_kernel(x_ref, g_ref, sh_ref, sc_ref, w_ref, o_ref, *rest, emit_h):
    h_ref = rest[-1]

    @pl.when(pl.program_id(2) == 0)
    def _():
        x = x_ref[0]
        y = x * lax.rsqrt(jnp.mean(x * x, axis=-1, keepdims=True) + EPS)
        h = (y * g_ref[...]) * (1.0 + sc_ref[0]) + sh_ref[0]
        h_ref[...] = h.astype(BF16)
        if emit_h:
            rest[0][0] = h.astype(BF16)

    o_ref[0] = jnp.dot(h_ref[...], w_ref[...], preferred_element_type=F32)


def norm_matmul(x, g, shift, scale, w, emit_h=False):
    bsz, t, d = x.shape
    n = w.shape[1]
    tm = min(t, 1024)
    tn = min(n, PROJ_TILE_N)
    out_shape = [jax.ShapeDtypeStruct((bsz, t, n), F32)]
    out_specs = [pl.BlockSpec((1, tm, tn), lambda b, i, j: (b, i, j))]
    if emit_h:
        out_shape.append(jax.ShapeDtypeStruct((bsz, t, d), BF16))
        out_specs.append(pl.BlockSpec((1, tm, d), lambda b, i, j: (b, i, 0)))
    res = pl.pallas_call(
        functools.partial(_norm_matmul_kernel, emit_h=emit_h),
        grid=(bsz, t // tm, n // tn),
        in_specs=[pl.BlockSpec((1, tm, d), lambda b, i, j: (b, i, 0)),
                  pl.BlockSpec((1, d), lambda b, i, j: (0, 0)),
                  pl.BlockSpec((1, 1, d), lambda b, i, j: (b, 0, 0)),
                  pl.BlockSpec((1, 1, d), lambda b, i, j: (b, 0, 0)),
                  pl.BlockSpec((d, tn), lambda b, i, j: (0, j))],
        out_specs=out_specs,
        out_shape=out_shape,
        scratch_shapes=[pltpu.VMEM((tm, d), BF16)],
        compiler_params=_cparams(("parallel", "parallel", "arbitrary")),
        name="norm_matmul",
    )(x, g.reshape(1, d), shift, scale, w)
    return res if emit_h else res[0]


def _merge_kernel(y0, y1, y2, y3, g0, g1, g2, g3, wb_ref, wo_ref, x_ref, ga_ref, o_ref):
    acc = None
    for i, (y, g) in enumerate(((y0, g0), (y1, g1), (y2, g2), (y3, g3))):
        p = jnp.dot(y[0].astype(BF16), wb_ref[i], preferred_element_type=F32)
        t = jax.nn.sigmoid(g[0]) * p
        acc = t if acc is None else acc + t
    o = jnp.dot(acc.astype(BF16), wo_ref[...], preferred_element_type=F32)
    o_ref[0] = x_ref[0] + ga_ref[0] * o


def merge(ys, proj, w_branch, w_out, x, ga):
    bsz, t, d = x.shape
    tm = min(t, 512)
    gate_blk = OFF_GATE // d
    y_spec = pl.BlockSpec((1, tm, BR_W), lambda b, i: (b, i, 0))
    g_specs = [pl.BlockSpec((1, tm, d), functools.partial(lambda b, i, k: (b, i, gate_blk + k), k=k))
               for k in range(N_BRANCH)]
    return pl.pallas_call(
        _merge_kernel,
        grid=(bsz, t // tm),
        in_specs=[y_spec] * N_BRANCH + g_specs + [
            pl.BlockSpec((N_BRANCH, BR_W, d), lambda b, i: (0, 0, 0)),
            pl.BlockSpec((d, d), lambda b, i: (0, 0)),
            pl.BlockSpec((1, tm, d), lambda b, i: (b, i, 0)),
            pl.BlockSpec((1, 1, d), lambda b, i: (b, 0, 0))],
        out_specs=pl.BlockSpec((1, tm, d), lambda b, i: (b, i, 0)),
        out_shape=jax.ShapeDtypeStruct((bsz, t, d), F32),
        compiler_params=_cparams(("parallel", "parallel")),
        name="merge",
    )(*ys, proj, proj, proj, proj, w_branch, w_out, x, ga)


PEER_TE = 512
PEER_ROWS = PEER_TE // PEER_KEYS


def _gelu_exact(x):
    return 0.5 * x * (1.0 + lax.erf(x * np.float32(1.0 / np.sqrt(2.0))))


def _peer_kernel(xm_ref, a_ref, lb_ref, b_ref, r2_ref, u_ref, vt_ref, x_ref, ga_ref, o_ref, acc_ref, w_ref):
    j = pl.program_id(2)

    @pl.when(j == 0)
    def _():
        acc_ref[...] = jnp.zeros_like(acc_ref)

    act = _gelu_exact(lax.dot_general(u_ref[...], xm_ref[0], (((1,), (1,)), ((), ())),
                                      preferred_element_type=F32))
    for a in range(PEER_ROWS):
        i1 = j * PEER_ROWS + a
        g = None
        for h in range(PEER_HEADS):
            sel = r2_ref[0, h] < lb_ref[0, h, pl.ds(i1, 1), :]
            t = jnp.where(sel, b_ref[0, h], 0.0) * a_ref[0, h, pl.ds(i1, 1), :]
            g = t if g is None else g + t
        w_ref[a * PEER_KEYS:(a + 1) * PEER_KEYS, :] = (g * act[a * PEER_KEYS:(a + 1) * PEER_KEYS, :]).astype(BF16)
    acc_ref[...] += jnp.dot(vt_ref[...], w_ref[...], preferred_element_type=F32)

    @pl.when(j == pl.num_programs(2) - 1)
    def _():
        o_ref[0] = x_ref[0] + ga_ref[0] * acc_ref[...].T


def peer_dense(xm, a_t, lb_t, b_t, r2_t, u, vt, x, ga):
    bsz, t, d = x.shape
    n_exp = u.shape[0]
    tn = min(t, 512)
    tab = pl.BlockSpec((1, PEER_HEADS, PEER_KEYS, tn), lambda b, i, j: (b, 0, 0, i))
    return pl.pallas_call(
        _peer_kernel,
        grid=(bsz, t // tn, n_exp // PEER_TE),
        in_specs=[pl.BlockSpec((1, tn, d), lambda b, i, j: (b, i, 0)),
                  tab, tab, tab, tab,
                  pl.BlockSpec((PEER_TE, d), lambda b, i, j: (j, 0)),
                  pl.BlockSpec((d, PEER_TE), lambda b, i, j: (0, j)),
                  pl.BlockSpec((1, tn, d), lambda b, i, j: (b, i, 0)),
                  pl.BlockSpec((1, 1, d), lambda b, i, j: (b, 0, 0))],
        out_specs=pl.BlockSpec((1, tn, d), lambda b, i, j: (b, i, 0)),
        out_shape=jax.ShapeDtypeStruct((bsz, t, d), F32),
        scratch_shapes=[pltpu.VMEM((d, tn), F32), pltpu.VMEM((PEER_TE, tn), BF16)],
        compiler_params=_cparams(("parallel", "parallel", "arbitrary")),
        name="peer_dense",
    )(xm, a_t, lb_t, b_t, r2_t, u, vt, x, ga)


def _extract_top(s, row, n_pick):
    rank = jnp.full(s.shape, float(n_pick), F32)
    tops = []
    for k in range(n_pick):
        m = jnp.max(s, axis=0, keepdims=True)
        idx = jnp.min(jnp.where(s == m, row, s.shape[0]), axis=0, keepdims=True)
        hit = row == idx
        rank = jnp.where(hit, float(k), rank)
        s = jnp.where(hit, -jnp.inf, s)
        tops.append(m)
    return rank, tops


def _peer_prep_kernel(x_ref, g_ref, sh_ref, sc_ref, wqt_ref, keys_ref,
                      xm_ref, a_ref, lb_ref, b_ref, r2_ref, qt_ref):
    tn = x_ref.shape[1]
    x = x_ref[0]
    y = x * lax.rsqrt(jnp.mean(x * x, axis=-1, keepdims=True) + EPS)
    xm = ((y * g_ref[...]) * (1.0 + sc_ref[0]) + sh_ref[0]).astype(BF16)
    xm_ref[0] = xm
    qt_ref[...] = lax.dot_general(wqt_ref[...], xm, (((1,), (1,)), ((), ())), preferred_element_type=F32)
    half = keys_ref.shape[2]
    row = lax.broadcasted_iota(jnp.int32, (PEER_KEYS, LANE), 0)
    sub = lax.broadcasted_iota(jnp.int32, (8, LANE), 0)
    n_cand_rows = PEER_TOPK + 8 * (PEER_TOPK // 2 - 1) + PEER_TOPK // 2
    crow = lax.broadcasted_iota(jnp.int32, (n_cand_rows, LANE), 0)

    def head_body(h, carry):
        for blk in range(tn // LANE):
            lanes = pl.ds(blk * LANE, LANE)
            s, rank, tops = [], [], []
            for c in range(2):
                qs = qt_ref[pl.ds(pl.multiple_of((2 * h + c) * half, half), half), lanes]
                sc_ = jnp.dot(keys_ref[c], qs, preferred_element_type=F32, precision=lax.Precision.HIGHEST)
                r, t = _extract_top(sc_, row, PEER_TOPK)
                s.append(sc_)
                rank.append(r)
                tops.append(t)
            t1 = jnp.concatenate(tops[0], axis=0)
            t2 = jnp.concatenate(tops[1], axis=0)
            groups = [t1[0:1] + t2]
            for a in range(1, PEER_TOPK // 2):
                n_valid = PEER_TOPK // (a + 1)
                grp = t1[a:a + 1] + t2[0:8]
                groups.append(grp if n_valid >= 8 else jnp.where(sub < n_valid, grp, -jnp.inf))
            groups.append(t1[PEER_TOPK // 2:] + t2[0:1])
            cand = jnp.concatenate(groups, axis=0)
            crank, _ = _extract_top(cand, crow, PEER_TOPK)
            picked = crank < float(PEER_TOPK)
            z = jnp.sum(jnp.where(picked, jnp.exp(cand - cand[0:1]), 0.0), axis=0, keepdims=True)
            pk = picked.astype(F32)
            counts = [jnp.sum(pk[0:PEER_TOPK], axis=0, keepdims=True)]
            for a in range(1, PEER_TOPK // 2):
                lo = PEER_TOPK + 8 * (a - 1)
                counts.append(jnp.sum(pk[lo:lo + 8], axis=0, keepdims=True))
            lo = PEER_TOPK + 8 * (PEER_TOPK // 2 - 1)
            for a in range(PEER_TOPK // 2):
                counts.append(pk[lo + a:lo + a + 1])
            lb = jnp.zeros((PEER_KEYS, LANE), F32)
            for a in range(PEER_TOPK):
                lb = lb + jnp.where(rank[0] == float(a), counts[a], 0.0)
            a_ref[0, h, :, lanes] = jnp.exp(s[0] - tops[0][0])
            lb_ref[0, h, :, lanes] = lb
            b_ref[0, h, :, lanes] = jnp.exp(s[1] - tops[1][0]) / z
            r2_ref[0, h, :, lanes] = rank[1]
        return carry

    lax.fori_loop(0, PEER_HEADS, head_body, 0)


def peer_prep(x, g, shift, scale, wqt, keys):
    bsz, t, d = x.shape
    tn = min(t, 256)
    nq = wqt.shape[0]
    tab_shape = jax.ShapeDtypeStruct((bsz, PEER_HEADS, PEER_KEYS, t), F32)
    tab_spec = pl.BlockSpec((1, PEER_HEADS, PEER_KEYS, tn), lambda b, i: (b, 0, 0, i))
    return pl.pallas_call(
        _peer_prep_kernel,
        grid=(bsz, t // tn),
        in_specs=[pl.BlockSpec((1, tn, d), lambda b, i: (b, i, 0)),
                  pl.BlockSpec((1, d), lambda b, i: (0, 0)),
                  pl.BlockSpec((1, 1, d), lambda b, i: (b, 0, 0)),
                  pl.BlockSpec((1, 1, d), lambda b, i: (b, 0, 0)),
                  pl.BlockSpec((nq, d), lambda b, i: (0, 0)),
                  pl.BlockSpec(keys.shape, lambda b, i: (0, 0, 0))],
        out_specs=[pl.BlockSpec((1, tn, d), lambda b, i: (b, i, 0)), tab_spec, tab_spec, tab_spec, tab_spec],
        out_shape=[jax.ShapeDtypeStruct((bsz, t, d), BF16), tab_shape, tab_shape, tab_shape, tab_shape],
        scratch_shapes=[pltpu.VMEM((nq, tn), F32)],
        compiler_params=_cparams(("parallel", "parallel")),
        name="peer_prep",
    )(x, g.reshape(1, d), shift, scale, wqt, keys)


def _heads(a, n):
    return a.reshape(a.shape[:-1] + (n, a.shape[-1] // n))


def _standardize(x):
    xc = x - jnp.mean(x, axis=-1, keepdims=True)
    return xc * lax.rsqrt(jnp.mean(xc * xc, axis=-1, keepdims=True) + EPS)


def _head_norm(y):
    z = _standardize(y)
    return z.reshape(z.shape[:-2] + (-1,))


def _axial_rope(x, rows, cols):
    d = x.shape[-1]
    nf = d // 4
    freqs = ROPE_BASE ** (-jnp.arange(nf, dtype=F32) / nf)

    def rot(xh, pos):
        ang = pos.astype(F32)[:, None] * freqs[None, :]
        cos = jnp.cos(ang)[None, :, None, :]
        sin = jnp.sin(ang)[None, :, None, :]
        x1, x2 = xh[..., :nf], xh[..., nf:]
        return jnp.concatenate([x1 * cos - x2 * sin, x2 * cos + x1 * sin], axis=-1)

    return jnp.concatenate([rot(x[..., : d // 2], rows), rot(x[..., d // 2:], cols)], axis=-1)


def _depthwise_conv(x, w, b):
    k = w.shape[0]
    y = lax.conv_general_dilated(
        x, w[:, None, :], window_strides=(1,), padding=[((k - 1) // 2, k // 2)],
        dimension_numbers=('NWC', 'WIO', 'NWC'), feature_group_count=x.shape[-1])
    return y + b


def _retention_scan(q, k, v, s0, log_gamma):
    bsz, t, h, d = q.shape
    nc, L = t // CHUNK, CHUNK
    idx = jnp.arange(L, dtype=F32)
    diff = idx[:, None] - idx[None, :]
    lower = diff >= 0
    dmat = jnp.where(lower, jnp.exp(jnp.where(lower, diff, 0.0)[None] * log_gamma[:, None, None]), 0.0)
    q_dec = jnp.exp((idx + 1.0)[None, :] * log_gamma[:, None])
    k_dec = jnp.exp((L - 1.0 - idx)[None, :] * log_gamma[:, None])
    c_dec = jnp.exp(L * log_gamma)
    chunks = lambda a: a.reshape(bsz, nc, L, h, d).transpose(1, 0, 3, 2, 4)

    def step(s, inp):
        qc, kc, vc = inp
        att = jnp.einsum('bhid,bhjd->bhij', qc, kc) * dmat
        o = jnp.einsum('bhij,bhjd->bhid', att, vc) + q_dec[..., None] * jnp.einsum('bhid,bhde->bhie', qc, s)
        s = c_dec[:, None, None] * s + jnp.einsum('bhjd,bhje->bhde', kc * k_dec[..., None], vc)
        return s, o

    s, o = lax.scan(step, s0, (chunks(q), chunks(k), chunks(v)))
    return o.transpose(1, 0, 3, 2, 4).reshape(bsz, t, h, d), s


def _mlstm_scan(q, k, v, log_i, log_f, state0):
    bsz, t, h, d = q.shape
    nc, L = t // CHUNK, CHUNK
    seq_chunks = lambda a: a.reshape(bsz, nc, L, h, d).transpose(1, 0, 3, 2, 4)
    gate_chunks = lambda a: a.reshape(bsz, nc, L, h).transpose(1, 0, 3, 2)
    tril = jnp.tril(jnp.ones((L, L), dtype=bool))

    def step(carry, inp):
        cmat, nvec, m = carry
        qc, kc, vc, ic, fc = inp
        b = jnp.cumsum(fc, axis=-1)
        logw = jnp.where(tril, b[..., :, None] - b[..., None, :] + ic[..., None, :], -jnp.inf)
        inter = b + m[..., None]
        m_t = jnp.maximum(inter, jnp.max(logw, axis=-1))
        s = jnp.einsum('bhid,bhjd->bhij', qc, kc) * jnp.exp(logw - m_t[..., None])
        w_prev = jnp.exp(inter - m_t)
        num = jnp.einsum('bhij,bhjd->bhid', s, vc) + w_prev[..., None] * jnp.einsum('bhid,bhde->bhie', qc, cmat)
        den = jnp.sum(s, axis=-1) + w_prev * jnp.einsum('bhid,bhd->bhi', qc, nvec)
        hc = num / jnp.maximum(jnp.abs(den), jnp.exp(-m_t))[..., None]
        b_end = b[..., -1]
        log_wk = b_end[..., None] - b + ic
        m_new = jnp.maximum(b_end + m, jnp.max(log_wk, axis=-1))
        decay = jnp.exp(b_end + m - m_new)
        kw = kc * jnp.exp(log_wk - m_new[..., None])[..., None]
        cmat = decay[..., None, None] * cmat + jnp.einsum('bhjd,bhje->bhde', kw, vc)
        nvec = decay[..., None] * nvec + jnp.sum(kw, axis=2)
        return (cmat, nvec, m_new), hc

    state, hs = lax.scan(step, state0, (seq_chunks(q), seq_chunks(k), seq_chunks(v),
                                        gate_chunks(log_i), gate_chunks(log_f)))
    return hs.transpose(1, 0, 3, 2, 4).reshape(bsz, t, h, d), state


def _run_bidir(scan_f, scan_b, ctx_f, lat_f, ctx_b, lat_b, state0):
    flip = lambda arrs: tuple(jnp.flip(a, axis=1) for a in arrs)
    oc_f, st_f = scan_f(*ctx_f, state0)
    ol_f, _ = scan_f(*lat_f, st_f)
    oc_b, st_b = scan_b(*flip(ctx_b), state0)
    ol_b, _ = scan_b(*flip(lat_b), st_b)
    return oc_f + jnp.flip(oc_b, axis=1), ol_f + jnp.flip(ol_b, axis=1)


def _softmax_with_sink(s, sink):
    sk = jnp.broadcast_to(sink[None, :, :, None, None], s.shape[:-1] + (1,))
    return jax.nn.softmax(jnp.concatenate([s, sk], axis=-1), axis=-1)[..., :-1]


def _window_attention(q, k, v, k_ctx, v_ctx, sink):
    bsz, s_len, hq, d = q.shape
    hkv = k.shape[2]
    g = hq // hkv
    nb = s_len // BLOCK
    scale = d ** -0.5
    sink_l = sink.reshape(hkv, g)
    qb = q.reshape(bsz, nb, BLOCK, hkv, g, d).transpose(1, 0, 2, 3, 4, 5)

    def band(a):
        ap = jnp.pad(a, ((0, 0), (BLOCK, BLOCK), (0, 0), (0, 0))).reshape(bsz, nb + 2, BLOCK, hkv, d)
        return jnp.concatenate([ap[:, :-2], ap[:, 1:-1], ap[:, 2:]], axis=2).transpose(1, 0, 2, 3, 4)

    kb, vb = band(k), band(v)
    blk = jnp.arange(nb)[:, None, None]
    qpos = blk * BLOCK + jnp.arange(BLOCK)[None, :, None]
    kpos = (blk - 1) * BLOCK + jnp.arange(3 * BLOCK)[None, None, :]
    mask = (jnp.abs(qpos - kpos) <= WINDOW) & (kpos >= 0) & (kpos < s_len)
    n_loc = 3 * BLOCK

    def block(args):
        qi, ki, vi, mi = args
        s_loc = jnp.einsum('bqhgd,bkhd->bhgqk', qi, ki) * scale
        s_loc = jnp.where(mi[None, None, None], s_loc, -jnp.inf)
        s_ctx = jnp.einsum('bqhgd,bkhd->bhgqk', qi, k_ctx) * scale
        p = _softmax_with_sink(jnp.concatenate([s_loc, s_ctx], axis=-1), sink_l)
        return (jnp.einsum('bhgqk,bkhd->bqhgd', p[..., :n_loc], vi)
                + jnp.einsum('bhgqk,bkhd->bqhgd', p[..., n_loc:], v_ctx))

    o = lax.map(block, (qb, kb, vb, mask))
    return o.transpose(1, 0, 2, 3, 4, 5).reshape(bsz, s_len, hq * d)


def _context_attention(q, k, v, sink):
    bsz, n, hq, d = q.shape
    hkv = k.shape[2]
    g = hq // hkv
    s = jnp.einsum('bqhgd,bkhd->bhgqk', q.reshape(bsz, n, hkv, g, d), k) * d ** -0.5
    p = _softmax_with_sink(s, sink.reshape(hkv, g))
    return jnp.einsum('bhgqk,bkhd->bqhgd', p, v).reshape(bsz, n, hq * d)


def _mixer_branches(proj_l, proj_c, rows, cols, conv_w, conv_b, conv_ln_g, conv_ln_b, ret_decay,
                    mlstm_conv_w, mlstm_conv_b, mlstm_gate_b, attn_sink, with_ctx_out):
    bsz = proj_l.shape[0]
    off_attn = OFF_GATE + N_BRANCH * conv_ln_g.shape[0] * 2
    off_mgate = off_attn + ATTN_IN

    def conformer_conv(p):
        a, g = jnp.split(p[..., OFF_CONV:OFF_CONV + CONV_IN], 2, axis=-1)
        y = _depthwise_conv(a * jax.nn.sigmoid(g), conv_w, conv_b)
        return jax.nn.silu(_standardize(y) * conv_ln_g + conv_ln_b)

    def ret_inputs(p, rotate):
        q, k, v, g = jnp.split(p[..., OFF_RET:OFF_RET + RET_IN], 4, axis=-1)
        q, k, v = _heads(q, RET_HEADS), _heads(k, RET_HEADS), _heads(v, RET_HEADS)
        if rotate:
            q, k = _axial_rope(q, rows, cols), _axial_rope(k, rows, cols)
        return (q, k * (BR_W // RET_HEADS) ** -0.5, v), g

    ret_in_l, ret_g_l = ret_inputs(proj_l, True)
    ret_in_c, ret_g_c = ret_inputs(proj_c, False)
    log_gamma = jax.nn.log_sigmoid(ret_decay)
    rd = BR_W // RET_HEADS
    s0 = jnp.zeros((bsz, RET_HEADS, rd, rd), F32)
    ret_y_c, ret_y_l = _run_bidir(functools.partial(_retention_scan, log_gamma=log_gamma[0]),
                                  functools.partial(_retention_scan, log_gamma=log_gamma[1]),
                                  ret_in_c, ret_in_l, ret_in_c, ret_in_l, s0)
    ret_out = lambda y, g: _head_norm(y) * jax.nn.silu(g)

    def mlstm_inputs(p):
        u = p[..., OFF_MLSTM:OFF_MLSTM + MLSTM_QKVO]
        qk, v, o = u[..., :2 * BR_W], u[..., 2 * BR_W:3 * BR_W], u[..., 3 * BR_W:]
        gates = p[..., off_mgate:off_mgate + MLSTM_GATES]
        qk = jax.nn.silu(_depthwise_conv(qk, mlstm_conv_w, mlstm_conv_b))
        q, k = jnp.split(qk, 2, axis=-1)
        q, k, v = (_heads(a, MLSTM_HEADS) for a in (q, k, v))
        k = k * (BR_W // MLSTM_HEADS) ** -0.5
        gi_f, gf_f, gi_b, gf_b = jnp.split(gates + mlstm_gate_b.reshape(-1), 4, axis=-1)
        return (q, k, v, gi_f, jax.nn.log_sigmoid(gf_f)), (q, k, v, gi_b, jax.nn.log_sigmoid(gf_b)), o

    mls_f_l, mls_b_l, mls_o_l = mlstm_inputs(proj_l)
    mls_f_c, mls_b_c, mls_o_c = mlstm_inputs(proj_c)
    md = BR_W // MLSTM_HEADS
    m_state0 = (jnp.zeros((bsz, MLSTM_HEADS, md, md), F32), jnp.zeros((bsz, MLSTM_HEADS, md), F32),
                jnp.full((bsz, MLSTM_HEADS), NEG_INIT, F32))
    mls_y_c, mls_y_l = _run_bidir(_mlstm_scan, _mlstm_scan, mls_f_c, mls_f_l, mls_b_c, mls_b_l, m_state0)
    mlstm_out = lambda y, o: _head_norm(y) * jax.nn.sigmoid(o)

    def attn_inputs(p):
        u = p[..., off_attn:off_attn + ATTN_IN]
        nq, nkv = ATTN_HEADS * ATTN_DIM, ATTN_KV_HEADS * ATTN_DIM
        return (_heads(u[..., :nq], ATTN_HEADS), _heads(u[..., nq:nq + nkv], ATTN_KV_HEADS),
                _heads(u[..., nq + nkv:], ATTN_KV_HEADS))

    q_l, k_l, v_l = attn_inputs(proj_l)
    q_c, k_c, v_c = attn_inputs(proj_c)
    q_l, k_l = _axial_rope(q_l, rows, cols), _axial_rope(k_l, rows, cols)
    att_y_l = _window_attention(q_l, k_l, v_l, k_c, v_c, attn_sink)
    ys_l = (conformer_conv(proj_l), ret_out(ret_y_l, ret_g_l), mlstm_out(mls_y_l, mls_o_l), att_y_l)
    if not with_ctx_out:
        return ys_l, None
    att_y_c = _context_attention(q_c, k_c, v_c, attn_sink)
    ys_c = (conformer_conv(proj_c), ret_out(ret_y_c, ret_g_c), mlstm_out(mls_y_c, mls_o_c), att_y_c)
    return ys_l, ys_c


def _pack_w_in(w):
    d = w.shape[0]
    used = w.shape[1] - MLSTM_GATES + LANE
    cols = [w[:, :REF_MGATE_OFF], w[:, REF_GATE_OFF:], w[:, REF_ATTN_OFF:REF_GATE_OFF],
            w[:, REF_MGATE_OFF:REF_ATTN_OFF],
            jnp.zeros((d, _round_up(used, PROJ_TILE_N) - w.shape[1]), w.dtype)]
    return jnp.concatenate(cols, axis=1).astype(BF16)


def _final_norm(x, g):
    y = x * lax.rsqrt(jnp.mean(x * x, axis=-1, keepdims=True) + EPS)
    return y * g


def kernel(x, c, ctx, c_ctx, mod_w, mod_b, norm1_g, norm2_g, w_in, conv_w, conv_b, conv_ln_g, conv_ln_b,
           ret_decay, mlstm_conv_w, mlstm_conv_b, mlstm_gate_b, attn_sink, w_branch, w_out,
           peer_wq, peer_keys, peer_u, peer_v, final_g):
    bsz, seq, d = x.shape
    depth = mod_w.shape[0]
    n_rows = seq // GRID_W
    rows = jnp.repeat(jnp.arange(n_rows, dtype=jnp.int32), GRID_W)
    cols = jnp.tile(jnp.arange(GRID_W, dtype=jnp.int32), n_rows)
    n_cond = _round_up(bsz + 1, 8)
    cond = jnp.concatenate([c, c_ctx[None], jnp.zeros((n_cond - bsz - 1, d), F32)], axis=0)
    for l in range(depth):
        last = l == depth - 1
        mod = mod_vectors(cond, mod_w[l], mod_b[l])
        sh1, sc1, ga1, sh2, sc2, ga2 = (m[:, None, :] for m in jnp.split(mod[:bsz], 6, axis=-1))
        csh1, csc1, cga1, csh2, csc2, cga2 = (jnp.broadcast_to(m[None], (bsz, 1, d))
                                              for m in jnp.split(mod[bsz:bsz + 1], 6, axis=-1))
        w_in_p = _pack_w_in(w_in[l])
        proj_l = norm_matmul(x, norm1_g[l], sh1, sc1, w_in_p)
        proj_c = norm_matmul(ctx, norm1_g[l], csh1, csc1, w_in_p)
        ys_l, ys_c = _mixer_branches(proj_l, proj_c, rows, cols, conv_w[l], conv_b[l], conv_ln_g[l], conv_ln_b[l],
                                     ret_decay[l], mlstm_conv_w[l], mlstm_conv_b[l], mlstm_gate_b[l],
                                     attn_sink[l], not last)
        wb, wo = w_branch[l].astype(BF16), w_out[l].astype(BF16)
        wqt = peer_wq[l].T.astype(BF16)
        u = peer_u[l].astype(BF16)
        vt = peer_v[l].T.astype(BF16)

        def peer(xx, sh, sc, ga):
            xm, *tabs = peer_prep(xx, norm2_g[l], sh, sc, wqt, peer_keys[l])
            return peer_dense(xm, *tabs, u, vt, xx, ga)

        x = merge(ys_l, proj_l, wb, wo, x, ga1)
        x = peer(x, sh2, sc2, ga2)
        if not last:
            ctx = merge(ys_c, proj_c, wb, wo, ctx, cga1)
            ctx = peer(ctx, csh2, csc2, cga2)
    return _final_norm(x, final_g)
```

```python
import functools

import jax
import jax.numpy as jnp
import numpy as np
from jax import lax
from jax.experimental import pallas as pl
from jax.experimental.pallas import tpu as pltpu

F32 = jnp.float32
BF16 = jnp.bfloat16

GRID_W = 64
EPS = 1e-6
N_BRANCH = 4
BR_W = 512
RET_HEADS = 4
MLSTM_HEADS = 4
MLSTM_GATES = 4 * MLSTM_HEADS
ATTN_HEADS = 8
ATTN_KV_HEADS = 2
ATTN_DIM = BR_W // ATTN_HEADS
WINDOW = 128
BLOCK = 128
CHUNK = 128
ROPE_BASE = 10000.0
NEG_INIT = -1e30
PEER_HEADS = 8
PEER_KEYS = 128
PEER_TOPK = 16

CONV_IN = 2 * BR_W
RET_IN = 4 * BR_W
MLSTM_QKVO = 4 * BR_W
ATTN_IN = ATTN_HEADS * ATTN_DIM + 2 * ATTN_KV_HEADS * ATTN_DIM
REF_MGATE_OFF = CONV_IN + RET_IN + MLSTM_QKVO
REF_ATTN_OFF = REF_MGATE_OFF + MLSTM_GATES
REF_GATE_OFF = REF_ATTN_OFF + ATTN_IN
OFF_CONV = 0
OFF_RET = OFF_CONV + CONV_IN
OFF_MLSTM = OFF_RET + RET_IN
OFF_GATE = OFF_MLSTM + MLSTM_QKVO
D_MODEL = 1024
OFF_ATTN = OFF_GATE + N_BRANCH * D_MODEL
OFF_MGATE = OFF_ATTN + ATTN_IN
LANE = 128
PROJ_TILE_N = 1024


def _round_up(a, m):
    return (a + m - 1) // m * m


VMEM_LIMIT = 56 * 1024 * 1024


def _cparams(sem):
    return pltpu.CompilerParams(dimension_semantics=sem, vmem_limit_bytes=VMEM_LIMIT)


def _mod_kernel(s_ref, w_ref, b_ref, o_ref):
    s = s_ref[...]
    s = s * jax.nn.sigmoid(s)
    o_ref[...] = jnp.dot(s, w_ref[...], preferred_element_type=F32,
                         precision=lax.Precision.HIGHEST) + b_ref[...]


def mod_vectors(cond, w, b):
    r, d = cond.shape
    n = w.shape[1]
    tn = 512
    return pl.pallas_call(
        _mod_kernel,
        grid=(n // tn,),
        in_specs=[pl.BlockSpec((r, d), lambda j: (0, 0)),
                  pl.BlockSpec((d, tn), lambda j: (0, j)),
                  pl.BlockSpec((1, tn), lambda j: (0, j))],
        out_specs=pl.BlockSpec((r, tn), lambda j: (0, j)),
        out_shape=jax.ShapeDtypeStruct((r, n), F32),
        compiler_params=_cparams(("parallel",)),
        name="mod_vectors",
    )(cond, w, b.reshape(1, n))


def _norm_matmul_kernel(x_ref, g_ref, sh_ref, sc_ref, w_ref, o_ref, *rest, emit_h):
    h_ref = rest[-1]

    @pl.when(pl.program_id(2) == 0)
    def _():
        x = x_ref[0]
        y = x * lax.rsqrt(jnp.mean(x * x, axis=-1, keepdims=True) + EPS)
        h = (y * g_ref[...]) * (1.0 + sc_ref[0]) + sh_ref[0]
        h_ref[...] = h.astype(BF16)
        if emit_h:
            rest[0][0] = h.astype(BF16)

    o_ref[0] = jnp.dot(h_ref[...], w_ref[...], preferred_element_type=F32)


def norm_matmul(x, g, shift, scale, w, emit_h=False):
    bsz, t, d = x.shape
    n = w.shape[1]
    tm = min(t, 1024)
    tn = min(n, PROJ_TILE_N)
    out_shape = [jax.ShapeDtypeStruct((bsz, t, n), F32)]
    out_specs = [pl.BlockSpec((1, tm, tn), lambda b, i, j: (b, i, j))]
    if emit_h:
        out_shape.append(jax.ShapeDtypeStruct((bsz, t, d), BF16))
        out_specs.append(pl.BlockSpec((1, tm, d), lambda b, i, j: (b, i, 0)))
    res = pl.pallas_call(
        functools.partial(_norm_matmul_kernel, emit_h=emit_h),
        grid=(bsz, t // tm, n // tn),
        in_specs=[pl.BlockSpec((1, tm, d), lambda b, i, j: (b, i, 0)),
                  pl.BlockSpec((1, d), lambda b, i, j: (0, 0)),
                  pl.BlockSpec((1, 1, d), lambda b, i, j: (b, 0, 0)),
                  pl.BlockSpec((1, 1, d), lambda b, i, j: (b, 0, 0)),
                  pl.BlockSpec((d, tn), lambda b, i, j: (0, j))],
        out_specs=out_specs,
        out_shape=out_shape,
        scratch_shapes=[pltpu.VMEM((tm, d), BF16)],
        compiler_params=_cparams(("parallel", "parallel", "arbitrary")),
        name="norm_matmul",
    )(x, g.reshape(1, d), shift, scale, w)
    return res if emit_h else res[0]


def _merge_kernel(y0, y1, y2, y3, g0, g1, g2, g3, wb_ref, wo_ref, x_ref, ga_ref, o_ref):
    acc = None
    for i, (y, g) in enumerate(((y0, g0), (y1, g1), (y2, g2), (y3, g3))):
        p = jnp.dot(y[0].astype(BF16), wb_ref[i], preferred_element_type=F32)
        t = jax.nn.sigmoid(g[0]) * p
        acc = t if acc is None else acc + t
    o = jnp.dot(acc.astype(BF16), wo_ref[...], preferred_element_type=F32)
    o_ref[0] = x_ref[0] + ga_ref[0] * o


def merge(ys, proj, w_branch, w_out, x, ga):
    bsz, t, d = x.shape
    tm = min(t, 512)
    gate_blk = OFF_GATE // d
    y_spec = pl.BlockSpec((1, tm, BR_W), lambda b, i: (b, i, 0))
    g_specs = [pl.BlockSpec((1, tm, d), functools.partial(lambda b, i, k: (b, i, gate_blk + k), k=k))
               for k in range(N_BRANCH)]
    return pl.pallas_call(
        _merge_kernel,
        grid=(bsz, t // tm),
        in_specs=[y_spec] * N_BRANCH + g_specs + [
            pl.BlockSpec((N_BRANCH, BR_W, d), lambda b, i: (0, 0, 0)),
            pl.BlockSpec((d, d), lambda b, i: (0, 0)),
            pl.BlockSpec((1, tm, d), lambda b, i: (b, i, 0)),
            pl.BlockSpec((1, 1, d), lambda b, i: (b, 0, 0))],
        out_specs=pl.BlockSpec((1, tm, d), lambda b, i: (b, i, 0)),
        out_shape=jax.ShapeDtypeStruct((bsz, t, d), F32),
        compiler_params=_cparams(("parallel", "parallel")),
        name="merge",
    )(*ys, proj, proj, proj, proj, w_branch, w_out, x, ga)


PEER_TE = 512
PEER_ROWS = PEER_TE // PEER_KEYS


def _gelu_exact(x):
    return 0.5 * x * (1.0 + lax.erf(x * np.float32(1.0 / np.sqrt(2.0))))


def _peer_kernel(xm_ref, a_ref, lb_ref, b_ref, r2_ref, u_ref, vt_ref, x_ref, ga_ref, o_ref, acc_ref, w_ref):
    j = pl.program_id(2)

    @pl.when(j == 0)
    def _():
        acc_ref[...] = jnp.zeros_like(acc_ref)

    act = _gelu_exact(lax.dot_general(u_ref[...], xm_ref[0], (((1,), (1,)), ((), ())),
                                      preferred_element_type=F32))
    for a in range(PEER_ROWS):
        i1 = j * PEER_ROWS + a
        g = None
        for h in range(PEER_HEADS):
            sel = r2_ref[0, h] < lb_ref[0, h, pl.ds(i1, 1), :]
            t = jnp.where(sel, b_ref[0, h], 0.0) * a_ref[0, h, pl.ds(i1, 1), :]
            g = t if g is None else g + t
        w_ref[a * PEER_KEYS:(a + 1) * PEER_KEYS, :] = (g * act[a * PEER_KEYS:(a + 1) * PEER_KEYS, :]).astype(BF16)
    acc_ref[...] += jnp.dot(vt_ref[...], w_ref[...], preferred_element_type=F32)

    @pl.when(j == pl.num_programs(2) - 1)
    def _():
        o_ref[0] = x_ref[0] + ga_ref[0] * acc_ref[...].T


def peer_dense(xm, a_t, lb_t, b_t, r2_t, u, vt, x, ga):
    bsz, t, d = x.shape
    n_exp = u.shape[0]
    tn = min(t, 512)
    tab = pl.BlockSpec((1, PEER_HEADS, PEER_KEYS, tn), lambda b, i, j: (b, 0, 0, i))
    return pl.pallas_call(
        _peer_kernel,
        grid=(bsz, t // tn, n_exp // PEER_TE),
        in_specs=[pl.BlockSpec((1, tn, d), lambda b, i, j: (b, i, 0)),
                  tab, tab, tab, tab,
                  pl.BlockSpec((PEER_TE, d), lambda b, i, j: (j, 0)),
                  pl.BlockSpec((d, PEER_TE), lambda b, i, j: (0, j)),
                  pl.BlockSpec((1, tn, d), lambda b, i, j: (b, i, 0)),
                  pl.BlockSpec((1, 1, d), lambda b, i, j: (b, 0, 0))],
        out_specs=pl.BlockSpec((1, tn, d), lambda b, i, j: (b, i, 0)),
        out_shape=jax.ShapeDtypeStruct((bsz, t, d), F32),
        scratch_shapes=[pltpu.VMEM((d, tn), F32), pltpu.VMEM((PEER_TE, tn), BF16)],
        compiler_params=_cparams(("parallel", "parallel", "arbitrary")),
        name="peer_dense",
    )(xm, a_t, lb_t, b_t, r2_t, u, vt, x, ga)


def _extract_top(s, row, n_pick):
    rank = jnp.full(s.shape, float(n_pick), F32)
    tops = []
    for k in range(n_pick):
        m = jnp.max(s, axis=0, keepdims=True)
        idx = jnp.min(jnp.where(s == m, row, s.shape[0]), axis=0, keepdims=True)
        hit = row == idx
        rank = jnp.where(hit, float(k), rank)
        s = jnp.where(hit, -jnp.inf, s)
        tops.append(m)
    return rank, tops


def _peer_prep_kernel(x_ref, g_ref, sh_ref, sc_ref, wqt_ref, keys_ref,
                      xm_ref, a_ref, lb_ref, b_ref, r2_ref, qt_ref):
    tn = x_ref.shape[1]
    x = x_ref[0]
    y = x * lax.rsqrt(jnp.mean(x * x, axis=-1, keepdims=True) + EPS)
    xm = ((y * g_ref[...]) * (1.0 + sc_ref[0]) + sh_ref[0]).astype(BF16)
    xm_ref[0] = xm
    qt_ref[...] = lax.dot_general(wqt_ref[...], xm, (((1,), (1,)), ((), ())), preferred_element_type=F32)
    half = keys_ref.shape[2]
    row = lax.broadcasted_iota(jnp.int32, (PEER_KEYS, LANE), 0)
    sub = lax.broadcasted_iota(jnp.int32, (8, LANE), 0)
    n_cand_rows = PEER_TOPK + 8 * (PEER_TOPK // 2 - 1) + PEER_TOPK // 2
    crow = lax.broadcasted_iota(jnp.int32, (n_cand_rows, LANE), 0)

    def head_body(h, carry):
        for blk in range(tn // LANE):
            lanes = pl.ds(blk * LANE, LANE)
            s, rank, tops = [], [], []
            for c in range(2):
                qs = qt_ref[pl.ds(pl.multiple_of((2 * h + c) * half, half), half), lanes]
                sc_ = jnp.dot(keys_ref[c], qs, preferred_element_type=F32, precision=lax.Precision.HIGHEST)
                r, t = _extract_top(sc_, row, PEER_TOPK)
                s.append(sc_)
                rank.append(r)
                tops.append(t)
            t1 = jnp.concatenate(tops[0], axis=0)
            t2 = jnp.concatenate(tops[1], axis=0)
            groups = [t1[0:1] + t2]
            for a in range(1, PEER_TOPK // 2):
                n_valid = PEER_TOPK // (a + 1)
                grp = t1[a:a + 1] + t2[0:8]
                groups.append(grp if n_valid >= 8 else jnp.where(sub < n_valid, grp, -jnp.inf))
            groups.append(t1[PEER_TOPK // 2:] + t2[0:1])
            cand = jnp.concatenate(groups, axis=0)
            crank, _ = _extract_top(cand, crow, PEER_TOPK)
            picked = crank < float(PEER_TOPK)
            z = jnp.sum(jnp.where(picked, jnp.exp(cand - cand[0:1]), 0.0), axis=0, keepdims=True)
            pk = picked.astype(F32)
            counts = [jnp.sum(pk[0:PEER_TOPK], axis=0, keepdims=True)]
            for a in range(1, PEER_TOPK // 2):
                lo = PEER_TOPK + 8 * (a - 1)
                counts.append(jnp.sum(pk[lo:lo + 8], axis=0, keepdims=True))
            lo = PEER_TOPK + 8 * (PEER_TOPK // 2 - 1)
            for a in range(PEER_TOPK // 2):
                counts.append(pk[lo + a:lo + a + 1])
            lb = jnp.zeros((PEER_KEYS, LANE), F32)
            for a in range(PEER_TOPK):
                lb = lb + jnp.where(rank[0] == float(a), counts[a], 0.0)
            a_ref[0, h, :, lanes] = jnp.exp(s[0] - tops[0][0])
            lb_ref[0, h, :, lanes] = lb
            b_ref[0, h, :, lanes] = jnp.exp(s[1] - tops[1][0]) / z
            r2_ref[0, h, :, lanes] = rank[1]
        return carry

    lax.fori_loop(0, PEER_HEADS, head_body, 0)


def peer_prep(x, g, shift, scale, wqt, keys):
    bsz, t, d = x.shape
    tn = min(t, 256)
    nq = wqt.shape[0]
    tab_shape = jax.ShapeDtypeStruct((bsz, PEER_HEADS, PEER_KEYS, t), F32)
    tab_spec = pl.BlockSpec((1, PEER_HEADS, PEER_KEYS, tn), lambda b, i: (b, 0, 0, i))
    return pl.pallas_call(
        _peer_prep_kernel,
        grid=(bsz, t // tn),
        in_specs=[pl.BlockSpec((1, tn, d), lambda b, i: (b, i, 0)),
                  pl.BlockSpec((1, d), lambda b, i: (0, 0)),
                  pl.BlockSpec((1, 1, d), lambda b, i: (b, 0, 0)),
                  pl.BlockSpec((1, 1, d), lambda b, i: (b, 0, 0)),
                  pl.BlockSpec((nq, d), lambda b, i: (0, 0)),
                  pl.BlockSpec(keys.shape, lambda b, i: (0, 0, 0))],
        out_specs=[pl.BlockSpec((1, tn, d), lambda b, i: (b, i, 0)), tab_spec, tab_spec, tab_spec, tab_spec],
        out_shape=[jax.ShapeDtypeStruct((bsz, t, d), BF16), tab_shape, tab_shape, tab_shape, tab_shape],
        scratch_shapes=[pltpu.VMEM((nq, tn), F32)],
        compiler_params=_cparams(("parallel", "parallel")),
        name="peer_prep",
    )(x, g.reshape(1, d), shift, scale, wqt, keys)


ROW_TILE = 128
CONV_PAD = 16


def _conv_kernel(p_ref, w_ref, b_ref, lg_ref, lb_ref, o_ref, pad_ref, y_ref):
    t, c = o_ref.shape[1], o_ref.shape[2]
    k_taps = w_ref.shape[0]
    first = CONV_PAD - (k_taps - 1) // 2
    lane_w = 256
    pad_ref[0:CONV_PAD, :] = jnp.zeros((CONV_PAD, c), F32)
    pad_ref[CONV_PAD + t:, :] = jnp.zeros((CONV_PAD, c), F32)

    def glu_body(i, carry):
        rows = pl.ds(pl.multiple_of(i * ROW_TILE, ROW_TILE), ROW_TILE)
        a = p_ref[0, rows, 0:c]
        g = p_ref[0, rows, c:2 * c]
        pad_ref[pl.ds(pl.multiple_of(CONV_PAD + i * ROW_TILE, 8), ROW_TILE), :] = a * jax.nn.sigmoid(g)
        return carry

    lax.fori_loop(0, t // ROW_TILE, glu_body, 0)

    def conv_body(i, carry):
        base = pl.multiple_of(i * ROW_TILE, ROW_TILE)
        for lb in range(c // lane_w):
            lanes = pl.ds(lb * lane_w, lane_w)
            win = pad_ref[pl.ds(base, ROW_TILE + 2 * CONV_PAD), lanes]
            acc = jnp.broadcast_to(b_ref[:, lanes], (ROW_TILE, lane_w))
            for k in range(k_taps):
                acc = acc + w_ref[k:k + 1, lanes] * win[first + k:first + k + ROW_TILE]
            y_ref[:, lanes] = acc
        y = y_ref[...]
        yc = y - jnp.mean(y, axis=-1, keepdims=True)
        z = yc * lax.rsqrt(jnp.mean(yc * yc, axis=-1, keepdims=True) + EPS) * lg_ref[...] + lb_ref[...]
        o_ref[0, pl.ds(base, ROW_TILE), :] = z * jax.nn.sigmoid(z)
        return carry

    lax.fori_loop(0, t // ROW_TILE, conv_body, 0)


def conv_branch(proj, w, b, ln_g, ln_b):
    bsz, t, _ = proj.shape
    k_taps, c = w.shape
    vec = pl.BlockSpec((1, c), lambda i: (0, 0))
    return pl.pallas_call(
        _conv_kernel,
        grid=(bsz,),
        in_specs=[pl.BlockSpec((1, t, 2 * c), lambda i: (i, 0, OFF_CONV // (2 * c))),
                  pl.BlockSpec((k_taps, c), lambda i: (0, 0)), vec, vec, vec],
        out_specs=pl.BlockSpec((1, t, c), lambda i: (i, 0, 0)),
        out_shape=jax.ShapeDtypeStruct((bsz, t, c), F32),
        scratch_shapes=[pltpu.VMEM((t + 2 * CONV_PAD, c), F32), pltpu.VMEM((ROW_TILE, c), F32)],
        compiler_params=_cparams(("parallel",)),
        name="conv_branch",
    )(proj, w, b.reshape(1, c), ln_g.reshape(1, c), ln_b.reshape(1, c))


def rope_tables(n_tok, head_dim):
    nf = head_dim // 4
    lane = np.arange(LANE)
    d_idx = lane % head_dim
    by_col = d_idx >= head_dim // 2
    freqs = ROPE_BASE ** (-(d_idx % nf).astype(np.float32) / nf)
    tok = np.arange(n_tok)
    pos = np.where(by_col[None, :], (tok % GRID_W)[:, None], (tok // GRID_W)[:, None]).astype(np.float32)
    ang = jnp.asarray(pos) * jnp.asarray(freqs, F32)[None, :]
    sign = np.where(d_idx % (2 * nf) < nf, -1.0, 1.0).astype(np.float32)
    return jnp.cos(ang), jnp.sin(ang) * sign[None, :]


def _rope(x, cos, sin_signed, nf):
    lane = lax.broadcasted_iota(jnp.int32, x.shape, 1)
    partner = jnp.where(lane % (2 * nf) < nf, pltpu.roll(x, LANE - nf, 1), pltpu.roll(x, nf, 1))
    return x * cos + partner * sin_signed


def _head_norm_rows(o):
    oc = o - jnp.mean(o, axis=-1, keepdims=True)
    return oc * lax.rsqrt(jnp.mean(oc * oc, axis=-1, keepdims=True) + EPS)


def _ret_kernel(dec_ref, ql_ref, kl_ref, vl_ref, gl_ref, qc_ref, kc_ref, vc_ref, gc_ref, cos_ref, sin_ref,
                yl_ref, yc_ref, ol_ref, oc_ref):
    h = pl.program_id(1)
    n_l, n_c = ql_ref.shape[1] // CHUNK, qc_ref.shape[1] // CHUNK
    d = ql_ref.shape[2]
    row = lax.broadcasted_iota(jnp.int32, (CHUNK, CHUNK), 0).astype(F32)
    col = lax.broadcasted_iota(jnp.int32, (CHUNK, CHUNK), 1).astype(F32)
    lg_f = jax.nn.log_sigmoid(jnp.full((CHUNK, CHUNK), dec_ref[0, h], F32))
    lg_b = jax.nn.log_sigmoid(jnp.full((CHUNK, CHUNK), dec_ref[1, h], F32))
    dmat = (jnp.where(row >= col, jnp.exp((row - col) * lg_f), 0.0)
            + jnp.where(col >= row, jnp.exp((col - row) * lg_b), 0.0))
    qdec_f, kdec_f, cdec_f = jnp.exp((row + 1.0) * lg_f), jnp.exp((CHUNK - 1.0 - row) * lg_f), jnp.exp(CHUNK * lg_f)
    qdec_b, kdec_b, cdec_b = jnp.exp((CHUNK - row) * lg_b), jnp.exp(row * lg_b), jnp.exp(CHUNK * lg_b)
    scale = d ** -0.5
    nt = (((1,), (1,)), ((), ()))
    tn = (((0,), (0,)), ((), ()))

    def load(refs, c, rotate):
        q_ref, k_ref, v_ref = refs
        rows = pl.ds(pl.multiple_of(c * CHUNK, CHUNK), CHUNK)
        q, k, v = q_ref[0, rows, :], k_ref[0, rows, :], v_ref[0, rows, :]
        if rotate:
            cos, sin = cos_ref[rows, :], sin_ref[rows, :]
            q, k = _rope(q, cos, sin, d // 4), _rope(k, cos, sin, d // 4)
        return q.astype(BF16), k * scale, v.astype(BF16), rows

    def fwd(refs, o_ref, rotate):
        def body(c, s):
            q, k, v, rows = load(refs, c, rotate)
            att = lax.dot_general(q, k.astype(BF16), nt, preferred_element_type=F32) * dmat
            o = jnp.dot(att.astype(BF16), v, preferred_element_type=F32)
            o_ref[rows, :] = o + qdec_f * jnp.dot(q, s.astype(BF16), preferred_element_type=F32)
            return cdec_f * s + lax.dot_general((k * kdec_f).astype(BF16), v, tn, preferred_element_type=F32)
        return body

    def bwd(refs, o_ref, rotate, n):
        def body(i, s):
            q, k, v, rows = load(refs, n - 1 - i, rotate)
            o_ref[rows, :] += qdec_b * jnp.dot(q, s.astype(BF16), preferred_element_type=F32)
            return cdec_b * s + lax.dot_general((k * kdec_b).astype(BF16), v, tn, preferred_element_type=F32)
        return body

    lat, ctx = (ql_ref, kl_ref, vl_ref), (qc_ref, kc_ref, vc_ref)
    s0 = jnp.zeros((d, d), F32)
    s = lax.fori_loop(0, n_c, fwd(ctx, oc_ref, False), s0)
    lax.fori_loop(0, n_l, fwd(lat, ol_ref, True), s)
    s = lax.fori_loop(0, n_c, bwd(ctx, oc_ref, False, n_c), s0)
    lax.fori_loop(0, n_l, bwd(lat, ol_ref, True, n_l), s)

    def finish(o_ref, g_ref, y_ref, n):
        def body(c, carry):
            rows = pl.ds(pl.multiple_of(c * CHUNK, CHUNK), CHUNK)
            g = g_ref[0, rows, :]
            y_ref[0, rows, :] = _head_norm_rows(o_ref[rows, :]) * (g * jax.nn.sigmoid(g))
            return carry
        lax.fori_loop(0, n, body, 0)

    finish(ol_ref, gl_ref, yl_ref, n_l)
    finish(oc_ref, gc_ref, yc_ref, n_c)


def retention_branch(proj_l, proj_c, ret_decay, cos, sin):
    bsz, t_l, _ = proj_l.shape
    t_c = proj_c.shape[1]
    d = BR_W // RET_HEADS
    blk0 = OFF_RET // d

    def col(t, part):
        return pl.BlockSpec((1, t, d), functools.partial(lambda b, h, dec, p: (b, 0, blk0 + p * RET_HEADS + h), p=part))

    table = pl.BlockSpec((t_l, LANE), lambda b, h, dec: (0, 0))
    grid_spec = pltpu.PrefetchScalarGridSpec(
        num_scalar_prefetch=1,
        grid=(bsz, RET_HEADS),
        in_specs=[col(t_l, p) for p in range(4)] + [col(t_c, p) for p in range(4)] + [table, table],
        out_specs=[pl.BlockSpec((1, t_l, d), lambda b, h, dec: (b, 0, h)),
                   pl.BlockSpec((1, t_c, d), lambda b, h, dec: (b, 0, h))],
        scratch_shapes=[pltpu.VMEM((t_l, d), F32), pltpu.VMEM((t_c, d), F32)])
    return pl.pallas_call(
        _ret_kernel,
        grid_spec=grid_spec,
        out_shape=[jax.ShapeDtypeStruct((bsz, t_l, BR_W), F32), jax.ShapeDtypeStruct((bsz, t_c, BR_W), F32)],
        compiler_params=_cparams(("parallel", "parallel")),
        name="retention_branch",
    )(ret_decay, proj_l, proj_l, proj_l, proj_l, proj_c, proj_c, proj_c, proj_c, cos, sin)


SHORT_PAD = 8


def _mlstm_kernel(ql_ref, kl_ref, vl_ref, ol_ref, gl_ref, qc_ref, kc_ref, vc_ref, oc_ref, gc_ref,
                  wq_ref, wk_ref, bq_ref, bk_ref, gb_ref, yl_ref, yc_ref,
                  pad_ref, qsl_ref, ksl_ref, qsc_ref, ksc_ref, hl_ref, hc_ref):
    h = pl.program_id(1)
    t_l, t_c = ql_ref.shape[1], qc_ref.shape[1]
    d = ql_ref.shape[2]
    row = lax.broadcasted_iota(jnp.int32, (CHUNK, CHUNK), 0)
    col = lax.broadcasted_iota(jnp.int32, (CHUNK, CHUNK), 1)
    eye = row == col
    scale = d ** -0.5
    nt = (((1,), (1,)), ((), ()))
    tn = (((0,), (0,)), ((), ()))

    def short_conv(x_ref, w_ref, b_ref, out_ref, t):
        k_taps = w_ref.shape[0]
        first = SHORT_PAD - (k_taps - 1) // 2
        pad_ref[0:SHORT_PAD, :] = jnp.zeros((SHORT_PAD, d), F32)
        pad_ref[SHORT_PAD + t:SHORT_PAD + t + SHORT_PAD, :] = jnp.zeros((SHORT_PAD, d), F32)

        def copy(i, carry):
            pad_ref[pl.ds(pl.multiple_of(SHORT_PAD + i * CHUNK, 8), CHUNK), :] = \
                x_ref[0, pl.ds(pl.multiple_of(i * CHUNK, CHUNK), CHUNK), :]
            return carry

        lax.fori_loop(0, t // CHUNK, copy, 0)

        def conv(i, carry):
            base = pl.multiple_of(i * CHUNK, CHUNK)
            win = pad_ref[pl.ds(base, CHUNK + 2 * SHORT_PAD), :]
            acc = jnp.broadcast_to(b_ref[...], (CHUNK, d))
            for k in range(k_taps):
                acc = acc + w_ref[k:k + 1, :] * win[first + k:first + k + CHUNK]
            out_ref[pl.ds(base, CHUNK), :] = acc * jax.nn.sigmoid(acc)
            return carry

        lax.fori_loop(0, t // CHUNK, conv, 0)

    short_conv(ql_ref, wq_ref, bq_ref, qsl_ref, t_l)
    short_conv(kl_ref, wk_ref, bk_ref, ksl_ref, t_l)
    short_conv(qc_ref, wq_ref, bq_ref, qsc_ref, t_c)
    short_conv(kc_ref, wk_ref, bk_ref, ksc_ref, t_c)

    def gate_cols(g_ref, rows, i_lane, f_lane):
        x = g_ref[0, rows, :] + gb_ref[...]
        i_col = jnp.sum(jnp.where(col == i_lane, x, 0.0), axis=1, keepdims=True)
        f_col = jax.nn.log_sigmoid(jnp.sum(jnp.where(col == f_lane, x, 0.0), axis=1, keepdims=True))
        i_row = jnp.sum(jnp.where(eye, i_col, 0.0), axis=0, keepdims=True)
        f_row = jnp.sum(jnp.where(eye, f_col, 0.0), axis=0, keepdims=True)
        return i_col, f_col, i_row, f_row

    def chunk_step(seq, c, state, forward, accumulate):
        qs_ref, ks_ref, v_ref, g_ref, h_ref = seq
        cmat, nvec, m = state
        rows = pl.ds(pl.multiple_of(c * CHUNK, CHUNK), CHUNK)
        q = qs_ref[rows, :]
        k = ks_ref[rows, :] * scale
        qb, vb = q.astype(BF16), v_ref[0, rows, :].astype(BF16)
        i_col, f_col, i_row, f_row = gate_cols(g_ref, rows, (0 if forward else 2 * MLSTM_HEADS) + h,
                                               (MLSTM_HEADS if forward else 3 * MLSTM_HEADS) + h)
        seen = (col <= row) if forward else (col >= row)
        seen_t = (row <= col) if forward else (row >= col)
        b_col = jnp.sum(jnp.where(seen, f_row, 0.0), axis=1, keepdims=True)
        b_row = jnp.sum(jnp.where(seen_t, f_col, 0.0), axis=0, keepdims=True)
        b_end = jnp.sum(f_row, axis=1, keepdims=True)
        logw = jnp.where(seen, b_col - b_row + i_row, -jnp.inf)
        inter = b_col + m
        m_t = jnp.maximum(inter, jnp.max(logw, axis=1, keepdims=True))
        s = lax.dot_general(qb, k.astype(BF16), nt, preferred_element_type=F32) * jnp.exp(logw - m_t)
        w_prev = jnp.exp(inter - m_t)
        num = (jnp.dot(s.astype(BF16), vb, preferred_element_type=F32)
               + w_prev * jnp.dot(qb, cmat.astype(BF16), preferred_element_type=F32))
        den = jnp.sum(s, axis=1, keepdims=True) + w_prev * jnp.sum(q * nvec, axis=1, keepdims=True)
        hc = num * (1.0 / jnp.maximum(jnp.abs(den), jnp.exp(-m_t)))
        if accumulate:
            h_ref[rows, :] += hc
        else:
            h_ref[rows, :] = hc
        log_wk = b_end - b_col + i_col
        m_new = jnp.maximum(b_end + m, jnp.max(log_wk, axis=0, keepdims=True))
        decay = jnp.exp(b_end + m - m_new)
        kw = k * jnp.exp(log_wk - m_new)
        cmat = decay * cmat + lax.dot_general(kw.astype(BF16), vb, tn, preferred_element_type=F32)
        nvec = decay * nvec + jnp.sum(kw, axis=0, keepdims=True)
        return cmat, nvec, m_new

    lat = (qsl_ref, ksl_ref, vl_ref, gl_ref, hl_ref)
    ctx = (qsc_ref, ksc_ref, vc_ref, gc_ref, hc_ref)
    n_l, n_c = t_l // CHUNK, t_c // CHUNK
    state0 = (jnp.zeros((d, d), F32), jnp.zeros((1, d), F32), jnp.full((1, 1), NEG_INIT, F32))
    st = lax.fori_loop(0, n_c, lambda c, s: chunk_step(ctx, c, s, True, False), state0)
    lax.fori_loop(0, n_l, lambda c, s: chunk_step(lat, c, s, True, False), st)
    st = lax.fori_loop(0, n_c, lambda i, s: chunk_step(ctx, n_c - 1 - i, s, False, True), state0)
    lax.fori_loop(0, n_l, lambda i, s: chunk_step(lat, n_l - 1 - i, s, False, True), st)

    def finish(h_ref, o_ref, y_ref, n):
        def body(c, carry):
            rows = pl.ds(pl.multiple_of(c * CHUNK, CHUNK), CHUNK)
            y_ref[0, rows, :] = _head_norm_rows(h_ref[rows, :]) * jax.nn.sigmoid(o_ref[0, rows, :])
            return carry
        lax.fori_loop(0, n, body, 0)

    finish(hl_ref, ol_ref, yl_ref, n_l)
    finish(hc_ref, oc_ref, yc_ref, n_c)


def mlstm_branch(proj_l, proj_c, conv_w, conv_b, gate_b):
    bsz, t_l, _ = proj_l.shape
    t_c = proj_c.shape[1]
    d = BR_W // MLSTM_HEADS
    blk0 = OFF_MLSTM // d
    k_taps = conv_w.shape[0]

    def col(t, part):
        return pl.BlockSpec((1, t, d), functools.partial(lambda b, h, p: (b, 0, blk0 + p * MLSTM_HEADS + h), p=part))

    def gates(t):
        return pl.BlockSpec((1, t, LANE), lambda b, h: (b, 0, OFF_MGATE // LANE))

    gb = jnp.zeros((1, LANE), F32).at[0, :MLSTM_GATES].set(gate_b.reshape(-1))
    return pl.pallas_call(
        _mlstm_kernel,
        grid=(bsz, MLSTM_HEADS),
        in_specs=[col(t_l, p) for p in range(4)] + [gates(t_l)] + [col(t_c, p) for p in range(4)] + [gates(t_c)] + [
            pl.BlockSpec((k_taps, d), lambda b, h: (0, h)),
            pl.BlockSpec((k_taps, d), lambda b, h: (0, MLSTM_HEADS + h)),
            pl.BlockSpec((1, d), lambda b, h: (0, h)),
            pl.BlockSpec((1, d), lambda b, h: (0, MLSTM_HEADS + h)),
            pl.BlockSpec((1, LANE), lambda b, h: (0, 0))],
        out_specs=[pl.BlockSpec((1, t_l, d), lambda b, h: (b, 0, h)),
                   pl.BlockSpec((1, t_c, d), lambda b, h: (b, 0, h))],
        out_shape=[jax.ShapeDtypeStruct((bsz, t_l, BR_W), F32), jax.ShapeDtypeStruct((bsz, t_c, BR_W), F32)],
        scratch_shapes=[pltpu.VMEM((t_l + 2 * SHORT_PAD, d), F32),
                        pltpu.VMEM((t_l, d), F32), pltpu.VMEM((t_l, d), F32),
                        pltpu.VMEM((t_c, d), F32), pltpu.VMEM((t_c, d), F32),
                        pltpu.VMEM((t_l, d), F32), pltpu.VMEM((t_c, d), F32)],
        compiler_params=_cparams(("parallel", "parallel")),
        name="mlstm_branch",
    )(proj_l, proj_l, proj_l, proj_l, proj_l, proj_c, proj_c, proj_c, proj_c, proj_c,
      conv_w, conv_w, conv_b.reshape(1, -1), conv_b.reshape(1, -1), gb)


def _attn_kernel(sink_ref, q_ref, *refs, local):
    if local:
        kp_ref, kc_ref, kn_ref, vp_ref, vc_ref, vn_ref, kx_ref, vx_ref, cos_ref, sin_ref, o_ref, k_scr, v_scr = refs
    else:
        kx_ref, vx_ref, o_ref, k_scr, v_scr = refs
    i = pl.program_id(1)
    nq = pl.num_programs(1)
    t_x = kx_ref.shape[1]
    n_loc = 3 * BLOCK if local else 0
    n_keys = n_loc + t_x
    nf = ATTN_DIM // 4
    half = LANE // 2

    def put(dst, lo, x):
        dst[0, lo:lo + x.shape[0], :] = x.astype(BF16)
        dst[1, lo:lo + x.shape[0], :] = pltpu.roll(x, half, 1).astype(BF16)

    if local:
        for j, (k_ref, v_ref) in enumerate(((kp_ref, vp_ref), (kc_ref, vc_ref), (kn_ref, vn_ref))):
            blk = jnp.clip(i + (j - 1), 0, nq - 1)
            rows = pl.ds(pl.multiple_of(blk * BLOCK, BLOCK), BLOCK)
            put(k_scr, j * BLOCK, _rope(k_ref[0], cos_ref[rows, :], sin_ref[rows, :], nf))
            put(v_scr, j * BLOCK, v_ref[0])
    put(k_scr, n_loc, kx_ref[0])
    put(v_scr, n_loc, vx_ref[0])

    lane = lax.broadcasted_iota(jnp.int32, (BLOCK, LANE), 1)
    if local:
        rel = (lax.broadcasted_iota(jnp.int32, (BLOCK, n_keys), 1)
               - lax.broadcasted_iota(jnp.int32, (BLOCK, n_keys), 0))
        kpos = (i - 1) * BLOCK + lax.broadcasted_iota(jnp.int32, (BLOCK, n_keys), 1)
        ok = (rel >= 0) & (rel <= 2 * WINDOW) & (kpos >= 0) & (kpos < nq * BLOCK)
        ok = ok | (lax.broadcasted_iota(jnp.int32, (BLOCK, n_keys), 1) >= n_loc)
        bias = jnp.where(ok, 0.0, -jnp.inf)
        qrows = pl.ds(pl.multiple_of(i * BLOCK, BLOCK), BLOCK)
        cos_q, sin_q = cos_ref[qrows, :], sin_ref[qrows, :]
    nt = (((1,), (1,)), ((), ()))
    for p in range(ATTN_HEADS // 2):
        qp = q_ref[0, :, p * LANE:(p + 1) * LANE]
        if local:
            qp = _rope(qp, cos_q, sin_q, nf)
        qp = qp * (ATTN_DIM ** -0.5)
        outs = []
        for sub in range(2):
            hd = 2 * p + sub
            kv_half = hd // (ATTN_HEADS // ATTN_KV_HEADS)
            sel = (lane >= half) if sub else (lane < half)
            qm = jnp.where(sel, qp, 0.0).astype(BF16)
            swap = 0 if kv_half == sub else 1
            s = lax.dot_general(qm, k_scr[swap], nt, preferred_element_type=F32)
            if local:
                s = s + bias
            sink = jnp.full((BLOCK, 1), sink_ref[hd], F32)
            m = jnp.maximum(jnp.max(s, axis=1, keepdims=True), sink)
            e = jnp.exp(s - m)
            inv = 1.0 / (jnp.sum(e, axis=1, keepdims=True) + jnp.exp(sink - m))
            outs.append(jnp.dot((e * inv).astype(BF16), v_scr[swap], preferred_element_type=F32))
        o_ref[0, :, p * LANE:(p + 1) * LANE] = jnp.where(lane < half, outs[0], outs[1])


def attention_branch(proj_q, proj_c, sink, cos, sin, local):
    bsz, t, _ = proj_q.shape
    t_c = proj_c.shape[1]
    nq = t // BLOCK
    n_qcol = ATTN_HEADS * ATTN_DIM
    q_blk = OFF_ATTN // n_qcol
    k_blk = (OFF_ATTN + n_qcol) // LANE
    v_blk = k_blk + 1
    n_keys = (3 * BLOCK if local else 0) + t_c

    def kv(blk, shift):
        return pl.BlockSpec((1, BLOCK, LANE), lambda b, i, s: (b, jnp.clip(i + shift, 0, nq - 1), blk))

    in_specs = [pl.BlockSpec((1, BLOCK, n_qcol), lambda b, i, s: (b, i, q_blk))]
    args = [proj_q]
    if local:
        in_specs += [kv(k_blk, -1), kv(k_blk, 0), kv(k_blk, 1), kv(v_blk, -1), kv(v_blk, 0), kv(v_blk, 1)]
        args += [proj_q] * 6
    in_specs += [pl.BlockSpec((1, t_c, LANE), lambda b, i, s: (b, 0, k_blk)),
                 pl.BlockSpec((1, t_c, LANE), lambda b, i, s: (b, 0, v_blk))]
    args += [proj_c, proj_c]
    if local:
        in_specs += [pl.BlockSpec((t, LANE), lambda b, i, s: (0, 0))] * 2
        args += [cos, sin]
    grid_spec = pltpu.PrefetchScalarGridSpec(
        num_scalar_prefetch=1,
        grid=(bsz, nq),
        in_specs=in_specs,
        out_specs=pl.BlockSpec((1, BLOCK, n_qcol), lambda b, i, s: (b, i, 0)),
        scratch_shapes=[pltpu.VMEM((2, n_keys, LANE), BF16), pltpu.VMEM((2, n_keys, LANE), BF16)])
    return pl.pallas_call(
        functools.partial(_attn_kernel, local=local),
        grid_spec=grid_spec,
        out_shape=jax.ShapeDtypeStruct((bsz, t, n_qcol), F32),
        compiler_params=_cparams(("parallel", "parallel")),
        name="attention_branch",
    )(sink, *args)


def _pack_w_in(w):
    d = w.shape[0]
    used = w.shape[1] - MLSTM_GATES + LANE
    cols = [w[:, :REF_MGATE_OFF], w[:, REF_GATE_OFF:], w[:, REF_ATTN_OFF:REF_GATE_OFF],
            w[:, REF_MGATE_OFF:REF_ATTN_OFF],
            jnp.zeros((d, _round_up(used, PROJ_TILE_N) - w.shape[1]), w.dtype)]
    return jnp.concatenate(cols, axis=1).astype(BF16)


def _final_norm(x, g):
    y = x * lax.rsqrt(jnp.mean(x * x, axis=-1, keepdims=True) + EPS)
    return y * g


def kernel(x, c, ctx, c_ctx, mod_w, mod_b, norm1_g, norm2_g, w_in, conv_w, conv_b, conv_ln_g, conv_ln_b,
           ret_decay, mlstm_conv_w, mlstm_conv_b, mlstm_gate_b, attn_sink, w_branch, w_out,
           peer_wq, peer_keys, peer_u, peer_v, final_g):
    bsz, seq, d = x.shape
    assert d == D_MODEL
    depth = mod_w.shape[0]
    cos_r, sin_r = rope_tables(seq, BR_W // RET_HEADS)
    cos_a, sin_a = rope_tables(seq, ATTN_DIM)
    n_cond = _round_up(bsz + 1, 8)
    cond = jnp.concatenate([c, c_ctx[None], jnp.zeros((n_cond - bsz - 1, d), F32)], axis=0)
    for l in range(depth):
        last = l == depth - 1
        mod = mod_vectors(cond, mod_w[l], mod_b[l])
        sh1, sc1, ga1, sh2, sc2, ga2 = (m[:, None, :] for m in jnp.split(mod[:bsz], 6, axis=-1))
        csh1, csc1, cga1, csh2, csc2, cga2 = (jnp.broadcast_to(m[None], (bsz, 1, d))
                                              for m in jnp.split(mod[bsz:bsz + 1], 6, axis=-1))
        w_in_p = _pack_w_in(w_in[l])
        proj_l = norm_matmul(x, norm1_g[l], sh1, sc1, w_in_p)
        proj_c = norm_matmul(ctx, norm1_g[l], csh1, csc1, w_in_p)
        ret_l, ret_c = retention_branch(proj_l, proj_c, ret_decay[l], cos_r, sin_r)
        mls_l, mls_c = mlstm_branch(proj_l, proj_c, mlstm_conv_w[l], mlstm_conv_b[l], mlstm_gate_b[l])
        conv_args = (conv_w[l], conv_b[l], conv_ln_g[l], conv_ln_b[l])
        ys_l = (conv_branch(proj_l, *conv_args), ret_l, mls_l,
                attention_branch(proj_l, proj_c, attn_sink[l], cos_a, sin_a, True))
        if not last:
            ys_c = (conv_branch(proj_c, *conv_args), ret_c, mls_c,
                    attention_branch(proj_c, proj_c, attn_sink[l], cos_a, sin_a, False))
        wb, wo = w_branch[l].astype(BF16), w_out[l].astype(BF16)
        wqt = peer_wq[l].T.astype(BF16)
        u = peer_u[l].astype(BF16)
        vt = peer_v[l].T.astype(BF16)

        def peer(xx, sh, sc, ga):
            xm, *tabs = peer_prep(xx, norm2_g[l], sh, sc, wqt, peer_keys[l])
            return peer_dense(xm, *tabs, u, vt, xx, ga)

        x = merge(ys_l, proj_l, wb, wo, x, ga1)
        x = peer(x, sh2, sc2, ga2)
        if not last:
            ctx = merge(ys_c, proj_c, wb, wo, ctx, cga1)
            ctx = peer(ctx, csh2, csc2, cga2)
    return _final_norm(x, final_g)
```

```python
import functools

import jax
import jax.numpy as jnp
import numpy as np
from jax import lax
from jax.experimental import pallas as pl
from jax.experimental.pallas import tpu as pltpu

F32 = jnp.float32
BF16 = jnp.bfloat16

GRID_W = 64
EPS = 1e-6
N_BRANCH = 4
BR_W = 512
RET_HEADS = 4
MLSTM_HEADS = 4
MLSTM_GATES = 4 * MLSTM_HEADS
ATTN_HEADS = 8
ATTN_KV_HEADS = 2
ATTN_DIM = BR_W // ATTN_HEADS
WINDOW = 128
BLOCK = 128
CHUNK = 128
ROPE_BASE = 10000.0
NEG_INIT = -1e30
PEER_HEADS = 8
PEER_KEYS = 128
PEER_TOPK = 16

CONV_IN = 2 * BR_W
RET_IN = 4 * BR_W
MLSTM_QKVO = 4 * BR_W
ATTN_IN = ATTN_HEADS * ATTN_DIM + 2 * ATTN_KV_HEADS * ATTN_DIM
REF_MGATE_OFF = CONV_IN + RET_IN + MLSTM_QKVO
REF_ATTN_OFF = REF_MGATE_OFF + MLSTM_GATES
REF_GATE_OFF = REF_ATTN_OFF + ATTN_IN
OFF_CONV = 0
OFF_RET = OFF_CONV + CONV_IN
OFF_MLSTM = OFF_RET + RET_IN
OFF_GATE = OFF_MLSTM + MLSTM_QKVO
D_MODEL = 1024
OFF_ATTN = OFF_GATE + N_BRANCH * D_MODEL
OFF_MGATE = OFF_ATTN + ATTN_IN
LANE = 128
BF16_ROWS = 16
PROJ_TILE_N = 1024


def _round_up(a, m):
    return (a + m - 1) // m * m


VMEM_LIMIT = 56 * 1024 * 1024


def _cparams(sem):
    return pltpu.CompilerParams(dimension_semantics=sem, vmem_limit_bytes=VMEM_LIMIT)


def _mod_kernel(s_ref, w_ref, b_ref, o_ref):
    s = s_ref[...]
    s = s * jax.nn.sigmoid(s)
    o_ref[...] = jnp.dot(s, w_ref[...], preferred_element_type=F32,
                         precision=lax.Precision.HIGHEST) + b_ref[...]


def mod_vectors(cond, w, b):
    r, d = cond.shape
    n = w.shape[1]
    tn = 512
    return pl.pallas_call(
        _mod_kernel,
        grid=(n // tn,),
        in_specs=[pl.BlockSpec((r, d), lambda j: (0, 0)),
                  pl.BlockSpec((d, tn), lambda j: (0, j)),
                  pl.BlockSpec((1, tn), lambda j: (0, j))],
        out_specs=pl.BlockSpec((r, tn), lambda j: (0, j)),
        out_shape=jax.ShapeDtypeStruct((r, n), F32),
        compiler_params=_cparams(("parallel",)),
        name="mod_vectors",
    )(cond, w, b.reshape(1, n))


def _norm_matmul_kernel(x_ref, g_ref, sh_ref, sc_ref, w_ref, o_ref, *rest, emit_h):
    h_ref = rest[-1]

    @pl.when(pl.program_id(2) == 0)
    def _():
        x = x_ref[0]
        y = x * lax.rsqrt(jnp.mean(x * x, axis=-1, keepdims=True) + EPS)
        h = (y * g_ref[...]) * (1.0 + sc_ref[0]) + sh_ref[0]
        h_ref[...] = h.astype(BF16)
        if emit_h:
            rest[0][0] = h.astype(BF16)

    o_ref[0] = jnp.dot(h_ref[...], w_ref[...], preferred_element_type=F32)


def norm_matmul(x, g, shift, scale, w, emit_h=False):
    bsz, t, d = x.shape
    n = w.shape[1]
    tm = min(t, 1024)
    tn = min(n, PROJ_TILE_N)
    out_shape = [jax.ShapeDtypeStruct((bsz, t, n), F32)]
    out_specs = [pl.BlockSpec((1, tm, tn), lambda b, i, j: (b, i, j))]
    if emit_h:
        out_shape.append(jax.ShapeDtypeStruct((bsz, t, d), BF16))
        out_specs.append(pl.BlockSpec((1, tm, d), lambda b, i, j: (b, i, 0)))
    res = pl.pallas_call(
        functools.partial(_norm_matmul_kernel, emit_h=emit_h),
        grid=(bsz, t // tm, n // tn),
        in_specs=[pl.BlockSpec((1, tm, d), lambda b, i, j: (b, i, 0)),
                  pl.BlockSpec((1, d), lambda b, i, j: (0, 0)),
                  pl.BlockSpec((1, 1, d), lambda b, i, j: (b, 0, 0)),
                  pl.BlockSpec((1, 1, d), lambda b, i, j: (b, 0, 0)),
                  pl.BlockSpec((d, tn), lambda b, i, j: (0, j))],
        out_specs=out_specs,
        out_shape=out_shape,
        scratch_shapes=[pltpu.VMEM((tm, d), BF16)],
        compiler_params=_cparams(("parallel", "parallel", "arbitrary")),
        name="norm_matmul",
    )(x, g.reshape(1, d), shift, scale, w)
    return res if emit_h else res[0]


def _merge_kernel(y0, y1, y2, y3, g0, g1, g2, g3, wb_ref, wo_ref, x_ref, ga_ref, o_ref):
    acc = None
    for i, (y, g) in enumerate(((y0, g0), (y1, g1), (y2, g2), (y3, g3))):
        p = jnp.dot(y[0].astype(BF16), wb_ref[i], preferred_element_type=F32)
        t = jax.nn.sigmoid(g[0]) * p
        acc = t if acc is None else acc + t
    o = jnp.dot(acc.astype(BF16), wo_ref[...], preferred_element_type=F32)
    o_ref[0] = x_ref[0] + ga_ref[0] * o


def merge(ys, proj, w_branch, w_out, x, ga):
    bsz, t, d = x.shape
    tm = min(t, 512)
    gate_blk = OFF_GATE // d
    y_spec = pl.BlockSpec((1, tm, BR_W), lambda b, i: (b, i, 0))
    g_specs = [pl.BlockSpec((1, tm, d), functools.partial(lambda b, i, k: (b, i, gate_blk + k), k=k))
               for k in range(N_BRANCH)]
    return pl.pallas_call(
        _merge_kernel,
        grid=(bsz, t // tm),
        in_specs=[y_spec] * N_BRANCH + g_specs + [
            pl.BlockSpec((N_BRANCH, BR_W, d), lambda b, i: (0, 0, 0)),
            pl.BlockSpec((d, d), lambda b, i: (0, 0)),
            pl.BlockSpec((1, tm, d), lambda b, i: (b, i, 0)),
            pl.BlockSpec((1, 1, d), lambda b, i: (b, 0, 0))],
        out_specs=pl.BlockSpec((1, tm, d), lambda b, i: (b, i, 0)),
        out_shape=jax.ShapeDtypeStruct((bsz, t, d), F32),
        compiler_params=_cparams(("parallel", "parallel")),
        name="merge",
    )(*ys, proj, proj, proj, proj, w_branch, w_out, x, ga)


PEER_TE = 1024
PEER_SUB = 512
PEER_SECOND = 1
PEER_GROUP = 4


def _gelu_exact(x):
    return 0.5 * x * (1.0 + lax.erf(x * np.float32(1.0 / np.sqrt(2.0))))


def _peer_kernel(xm_ref, a_ref, lb_ref, b_ref, r2_ref, u_ref, vt_ref, x_ref, ga_ref, o_ref,
                 acc_ref, w_ref, h_ref):
    j = pl.program_id(2)

    @pl.when(j == 0)
    def _():
        acc_ref[...] = jnp.zeros_like(acc_ref)

    tn = xm_ref.shape[1]

    def spread(row):
        return jnp.broadcast_to(row, (BF16_ROWS, LANE)).astype(BF16)[None]

    def slabs(words):
        return pltpu.bitcast(words, BF16).reshape(PEER_KEYS // BF16_ROWS, BF16_ROWS, LANE)

    n_sub = PEER_TE // PEER_SUB

    def first_matmul(sub):
        h_ref[sub % 2] = lax.dot_general(u_ref[pl.ds(sub * PEER_SUB, PEER_SUB), :], xm_ref[0],
                                         (((1,), (1,)), ((), ())), preferred_element_type=F32)

    first_matmul(0)
    for sub in range(n_sub):
        if sub + 1 < n_sub:
            first_matmul(sub + 1)
        for grp in range(PEER_SUB // (PEER_GROUP * PEER_KEYS)):
            row0 = sub * (PEER_SUB // PEER_KEYS) + grp * PEER_GROUP
            first_rows = [pl.ds(j * (PEER_TE // PEER_KEYS) + row0 + a, 1) for a in range(PEER_GROUP)]
            for blk in range(tn // LANE):
                lanes = pl.ds(blk * LANE, LANE)
                lo = blk * LANE
                g = [None] * PEER_GROUP
                for h in range(PEER_HEADS):
                    r2 = slabs(r2_ref[0, h, :, lanes])
                    b = slabs(b_ref[0, h, :, lanes])
                    for a in range(PEER_GROUP):
                        limit = spread(lb_ref[0, h, first_rows[a], :][:, lo:lo + LANE])
                        t = jnp.where(r2 < limit, b, 0.0) * spread(a_ref[0, h, first_rows[a], :][:, lo:lo + LANE])
                        g[a] = t if g[a] is None else g[a] + t
                for a in range(PEER_GROUP):
                    local = pl.ds((grp * PEER_GROUP + a) * PEER_KEYS, PEER_KEYS)
                    act = _gelu_exact(h_ref[sub % 2, local, lanes]).astype(BF16)
                    w_ref[pl.ds((row0 + a) * PEER_KEYS, PEER_KEYS), lanes] = g[a].reshape(PEER_KEYS, LANE) * act
        if sub % PEER_SECOND == PEER_SECOND - 1:
            experts = pl.ds((sub + 1 - PEER_SECOND) * PEER_SUB, PEER_SECOND * PEER_SUB)
            acc_ref[...] += jnp.dot(vt_ref[:, experts], w_ref[experts, :], preferred_element_type=F32)

    @pl.when(j == pl.num_programs(2) - 1)
    def _():
        o_ref[0] = x_ref[0] + ga_ref[0] * acc_ref[...].T


def peer_dense(xm, a_t, lb_t, b_t, r2_t, u, vt, x, ga):
    bsz, t, d = x.shape
    n_exp = u.shape[0]
    tn = min(t, 512)
    tab = pl.BlockSpec((1, PEER_HEADS, PEER_KEYS, tn), lambda b, i, j: (b, 0, 0, i))
    pair = pl.BlockSpec((1, PEER_HEADS, PEER_KEYS // 2, tn), lambda b, i, j: (b, 0, 0, i))
    return pl.pallas_call(
        _peer_kernel,
        grid=(bsz, t // tn, n_exp // PEER_TE),
        in_specs=[pl.BlockSpec((1, tn, d), lambda b, i, j: (b, i, 0)),
                  tab, tab, pair, pair,
                  pl.BlockSpec((PEER_TE, d), lambda b, i, j: (j, 0)),
                  pl.BlockSpec((d, PEER_TE), lambda b, i, j: (0, j)),
                  pl.BlockSpec((1, tn, d), lambda b, i, j: (b, i, 0)),
                  pl.BlockSpec((1, 1, d), lambda b, i, j: (b, 0, 0))],
        out_specs=pl.BlockSpec((1, tn, d), lambda b, i, j: (b, i, 0)),
        out_shape=jax.ShapeDtypeStruct((bsz, t, d), F32),
        scratch_shapes=[pltpu.VMEM((d, tn), F32), pltpu.VMEM((PEER_TE, tn), BF16),
                        pltpu.VMEM((2, PEER_SUB, tn), F32)],
        compiler_params=_cparams(("parallel", "parallel", "arbitrary")),
        name="peer_dense",
    )(xm, a_t, lb_t, b_t, r2_t, u, vt, x, ga)


def _extract_top(s, row, n_pick):
    rank = jnp.full(s.shape, float(n_pick), F32)
    tops = []
    for k in range(n_pick):
        m = jnp.max(s, axis=0, keepdims=True)
        idx = jnp.min(jnp.where(s == m, row, s.shape[0]), axis=0, keepdims=True)
        hit = row == idx
        rank = jnp.where(hit, float(k), rank)
        s = jnp.where(hit, -jnp.inf, s)
        tops.append(m)
    return rank, tops


def _peer_prep_kernel(x_ref, g_ref, sh_ref, sc_ref, wqt_ref, keys_ref,
                      xm_ref, a_ref, lb_ref, b_ref, r2_ref, qt_ref):
    tn = x_ref.shape[1]
    x = x_ref[0]
    y = x * lax.rsqrt(jnp.mean(x * x, axis=-1, keepdims=True) + EPS)
    xm = ((y * g_ref[...]) * (1.0 + sc_ref[0]) + sh_ref[0]).astype(BF16)
    xm_ref[0] = xm
    qt_ref[...] = lax.dot_general(wqt_ref[...], xm, (((1,), (1,)), ((), ())), preferred_element_type=F32)
    half = keys_ref.shape[2]
    row = lax.broadcasted_iota(jnp.int32, (PEER_KEYS, LANE), 0)
    sub = lax.broadcasted_iota(jnp.int32, (8, LANE), 0)
    n_cand_rows = PEER_TOPK + 8 * (PEER_TOPK // 2 - 1) + PEER_TOPK // 2
    crow = lax.broadcasted_iota(jnp.int32, (n_cand_rows, LANE), 0)

    def head_body(h, carry):
        for blk in range(tn // LANE):
            lanes = pl.ds(blk * LANE, LANE)
            s, rank, tops = [], [], []
            for c in range(2):
                qs = qt_ref[pl.ds(pl.multiple_of((2 * h + c) * half, half), half), lanes]
                sc_ = jnp.dot(keys_ref[c], qs, preferred_element_type=F32, precision=lax.Precision.HIGHEST)
                r, t = _extract_top(sc_, row, PEER_TOPK)
                s.append(sc_)
                rank.append(r)
                tops.append(t)
            t1 = jnp.concatenate(tops[0], axis=0)
            t2 = jnp.concatenate(tops[1], axis=0)
            groups = [t1[0:1] + t2]
            for a in range(1, PEER_TOPK // 2):
                n_valid = PEER_TOPK // (a + 1)
                grp = t1[a:a + 1] + t2[0:8]
                groups.append(grp if n_valid >= 8 else jnp.where(sub < n_valid, grp, -jnp.inf))
            groups.append(t1[PEER_TOPK // 2:] + t2[0:1])
            cand = jnp.concatenate(groups, axis=0)
            crank, _ = _extract_top(cand, crow, PEER_TOPK)
            picked = crank < float(PEER_TOPK)
            z = jnp.sum(jnp.where(picked, jnp.exp(cand - cand[0:1]), 0.0), axis=0, keepdims=True)
            pk = picked.astype(F32)
            counts = [jnp.sum(pk[0:PEER_TOPK], axis=0, keepdims=True)]
            for a in range(1, PEER_TOPK // 2):
                lo = PEER_TOPK + 8 * (a - 1)
                counts.append(jnp.sum(pk[lo:lo + 8], axis=0, keepdims=True))
            lo = PEER_TOPK + 8 * (PEER_TOPK // 2 - 1)
            for a in range(PEER_TOPK // 2):
                counts.append(pk[lo + a:lo + a + 1])
            lb = jnp.zeros((PEER_KEYS, LANE), F32)
            for a in range(PEER_TOPK):
                lb = lb + jnp.where(rank[0] == float(a), counts[a], 0.0)
            a_ref[0, h, :, lanes] = jnp.exp(s[0] - tops[0][0])
            lb_ref[0, h, :, lanes] = lb
            b_ref[0, h, :, lanes] = pltpu.bitcast((jnp.exp(s[1] - tops[1][0]) / z).astype(BF16), jnp.uint32)
            r2_ref[0, h, :, lanes] = pltpu.bitcast(rank[1].astype(BF16), jnp.uint32)
        return carry

    lax.fori_loop(0, PEER_HEADS, head_body, 0)


def peer_prep(x, g, shift, scale, wqt, keys):
    bsz, t, d = x.shape
    tn = min(t, 256)
    nq = wqt.shape[0]
    tab_shape = jax.ShapeDtypeStruct((bsz, PEER_HEADS, PEER_KEYS, t), F32)
    tab_spec = pl.BlockSpec((1, PEER_HEADS, PEER_KEYS, tn), lambda b, i: (b, 0, 0, i))
    pair_shape = jax.ShapeDtypeStruct((bsz, PEER_HEADS, PEER_KEYS // 2, t), jnp.uint32)
    pair_spec = pl.BlockSpec((1, PEER_HEADS, PEER_KEYS // 2, tn), lambda b, i: (b, 0, 0, i))
    return pl.pallas_call(
        _peer_prep_kernel,
        grid=(bsz, t // tn),
        in_specs=[pl.BlockSpec((1, tn, d), lambda b, i: (b, i, 0)),
                  pl.BlockSpec((1, d), lambda b, i: (0, 0)),
                  pl.BlockSpec((1, 1, d), lambda b, i: (b, 0, 0)),
                  pl.BlockSpec((1, 1, d), lambda b, i: (b, 0, 0)),
                  pl.BlockSpec((nq, d), lambda b, i: (0, 0)),
                  pl.BlockSpec(keys.shape, lambda b, i: (0, 0, 0))],
        out_specs=[pl.BlockSpec((1, tn, d), lambda b, i: (b, i, 0)), tab_spec, tab_spec, pair_spec, pair_spec],
        out_shape=[jax.ShapeDtypeStruct((bsz, t, d), BF16), tab_shape, tab_shape, pair_shape, pair_shape],
        scratch_shapes=[pltpu.VMEM((nq, tn), F32)],
        compiler_params=_cparams(("parallel", "parallel")),
        name="peer_prep",
    )(x, g.reshape(1, d), shift, scale, wqt, keys)


ROW_TILE = 128
CONV_PAD = 16


def _conv_kernel(p_ref, w_ref, b_ref, lg_ref, lb_ref, o_ref, pad_ref, y_ref):
    t, c = o_ref.shape[1], o_ref.shape[2]
    k_taps = w_ref.shape[0]
    first = CONV_PAD - (k_taps - 1) // 2
    lane_w = 256
    pad_ref[0:CONV_PAD, :] = jnp.zeros((CONV_PAD, c), F32)
    pad_ref[CONV_PAD + t:, :] = jnp.zeros((CONV_PAD, c), F32)

    def glu_body(i, carry):
        rows = pl.ds(pl.multiple_of(i * ROW_TILE, ROW_TILE), ROW_TILE)
        a = p_ref[0, rows, 0:c]
        g = p_ref[0, rows, c:2 * c]
        pad_ref[pl.ds(pl.multiple_of(CONV_PAD + i * ROW_TILE, 8), ROW_TILE), :] = a * jax.nn.sigmoid(g)
        return carry

    lax.fori_loop(0, t // ROW_TILE, glu_body, 0)

    def conv_body(i, carry):
        base = pl.multiple_of(i * ROW_TILE, ROW_TILE)
        for lb in range(c // lane_w):
            lanes = pl.ds(lb * lane_w, lane_w)
            win = pad_ref[pl.ds(base, ROW_TILE + 2 * CONV_PAD), lanes]
            acc = jnp.broadcast_to(b_ref[:, lanes], (ROW_TILE, lane_w))
            for k in range(k_taps):
                acc = acc + w_ref[k:k + 1, lanes] * win[first + k:first + k + ROW_TILE]
            y_ref[:, lanes] = acc
        y = y_ref[...]
        yc = y - jnp.mean(y, axis=-1, keepdims=True)
        z = yc * lax.rsqrt(jnp.mean(yc * yc, axis=-1, keepdims=True) + EPS) * lg_ref[...] + lb_ref[...]
        o_ref[0, pl.ds(base, ROW_TILE), :] = z * jax.nn.sigmoid(z)
        return carry

    lax.fori_loop(0, t // ROW_TILE, conv_body, 0)


def conv_branch(proj, w, b, ln_g, ln_b):
    bsz, t, _ = proj.shape
    k_taps, c = w.shape
    vec = pl.BlockSpec((1, c), lambda i: (0, 0))
    return pl.pallas_call(
        _conv_kernel,
        grid=(bsz,),
        in_specs=[pl.BlockSpec((1, t, 2 * c), lambda i: (i, 0, OFF_CONV // (2 * c))),
                  pl.BlockSpec((k_taps, c), lambda i: (0, 0)), vec, vec, vec],
        out_specs=pl.BlockSpec((1, t, c), lambda i: (i, 0, 0)),
        out_shape=jax.ShapeDtypeStruct((bsz, t, c), F32),
        scratch_shapes=[pltpu.VMEM((t + 2 * CONV_PAD, c), F32), pltpu.VMEM((ROW_TILE, c), F32)],
        compiler_params=_cparams(("parallel",)),
        name="conv_branch",
    )(proj, w, b.reshape(1, c), ln_g.reshape(1, c), ln_b.reshape(1, c))


def rope_tables(n_tok, head_dim):
    nf = head_dim // 4
    lane = np.arange(LANE)
    d_idx = lane % head_dim
    by_col = d_idx >= head_dim // 2
    freqs = ROPE_BASE ** (-(d_idx % nf).astype(np.float32) / nf)
    tok = np.arange(n_tok)
    pos = np.where(by_col[None, :], (tok % GRID_W)[:, None], (tok // GRID_W)[:, None]).astype(np.float32)
    ang = jnp.asarray(pos) * jnp.asarray(freqs, F32)[None, :]
    sign = np.where(d_idx % (2 * nf) < nf, -1.0, 1.0).astype(np.float32)
    return jnp.cos(ang), jnp.sin(ang) * sign[None, :]


def _rope(x, cos, sin_signed, nf):
    lane = lax.broadcasted_iota(jnp.int32, x.shape, 1)
    partner = jnp.where(lane % (2 * nf) < nf, pltpu.roll(x, LANE - nf, 1), pltpu.roll(x, nf, 1))
    return x * cos + partner * sin_signed


def _head_norm_rows(o):
    oc = o - jnp.mean(o, axis=-1, keepdims=True)
    return oc * lax.rsqrt(jnp.mean(oc * oc, axis=-1, keepdims=True) + EPS)


def _ret_kernel(dec_ref, ql_ref, kl_ref, vl_ref, gl_ref, qc_ref, kc_ref, vc_ref, gc_ref, cos_ref, sin_ref,
                yl_ref, yc_ref, ol_ref, oc_ref):
    h = pl.program_id(1)
    n_l, n_c = ql_ref.shape[1] // CHUNK, qc_ref.shape[1] // CHUNK
    d = ql_ref.shape[2]
    row = lax.broadcasted_iota(jnp.int32, (CHUNK, CHUNK), 0).astype(F32)
    col = lax.broadcasted_iota(jnp.int32, (CHUNK, CHUNK), 1).astype(F32)
    lg_f = jax.nn.log_sigmoid(jnp.full((CHUNK, CHUNK), dec_ref[0, h], F32))
    lg_b = jax.nn.log_sigmoid(jnp.full((CHUNK, CHUNK), dec_ref[1, h], F32))
    dmat = (jnp.where(row >= col, jnp.exp((row - col) * lg_f), 0.0)
            + jnp.where(col >= row, jnp.exp((col - row) * lg_b), 0.0))
    qdec_f, kdec_f, cdec_f = jnp.exp((row + 1.0) * lg_f), jnp.exp((CHUNK - 1.0 - row) * lg_f), jnp.exp(CHUNK * lg_f)
    qdec_b, kdec_b, cdec_b = jnp.exp((CHUNK - row) * lg_b), jnp.exp(row * lg_b), jnp.exp(CHUNK * lg_b)
    scale = d ** -0.5
    nt = (((1,), (1,)), ((), ()))
    tn = (((0,), (0,)), ((), ()))

    def load(refs, c, rotate):
        q_ref, k_ref, v_ref = refs
        rows = pl.ds(pl.multiple_of(c * CHUNK, CHUNK), CHUNK)
        q, k, v = q_ref[0, rows, :], k_ref[0, rows, :], v_ref[0, rows, :]
        if rotate:
            cos, sin = cos_ref[rows, :], sin_ref[rows, :]
            q, k = _rope(q, cos, sin, d // 4), _rope(k, cos, sin, d // 4)
        return q.astype(BF16), k * scale, v.astype(BF16), rows

    def fwd(refs, o_ref, rotate):
        def body(c, s):
            q, k, v, rows = load(refs, c, rotate)
            att = lax.dot_general(q, k.astype(BF16), nt, preferred_element_type=F32) * dmat
            o = jnp.dot(att.astype(BF16), v, preferred_element_type=F32)
            o_ref[rows, :] = o + qdec_f * jnp.dot(q, s.astype(BF16), preferred_element_type=F32)
            return cdec_f * s + lax.dot_general((k * kdec_f).astype(BF16), v, tn, preferred_element_type=F32)
        return body

    def bwd(refs, o_ref, rotate, n):
        def body(i, s):
            q, k, v, rows = load(refs, n - 1 - i, rotate)
            o_ref[rows, :] += qdec_b * jnp.dot(q, s.astype(BF16), preferred_element_type=F32)
            return cdec_b * s + lax.dot_general((k * kdec_b).astype(BF16), v, tn, preferred_element_type=F32)
        return body

    lat, ctx = (ql_ref, kl_ref, vl_ref), (qc_ref, kc_ref, vc_ref)
    s0 = jnp.zeros((d, d), F32)
    s = lax.fori_loop(0, n_c, fwd(ctx, oc_ref, False), s0)
    lax.fori_loop(0, n_l, fwd(lat, ol_ref, True), s)
    s = lax.fori_loop(0, n_c, bwd(ctx, oc_ref, False, n_c), s0)
    lax.fori_loop(0, n_l, bwd(lat, ol_ref, True, n_l), s)

    def finish(o_ref, g_ref, y_ref, n):
        def body(c, carry):
            rows = pl.ds(pl.multiple_of(c * CHUNK, CHUNK), CHUNK)
            g = g_ref[0, rows, :]
            y_ref[0, rows, :] = _head_norm_rows(o_ref[rows, :]) * (g * jax.nn.sigmoid(g))
            return carry
        lax.fori_loop(0, n, body, 0)

    finish(ol_ref, gl_ref, yl_ref, n_l)
    finish(oc_ref, gc_ref, yc_ref, n_c)


def retention_branch(proj_l, proj_c, ret_decay, cos, sin):
    bsz, t_l, _ = proj_l.shape
    t_c = proj_c.shape[1]
    d = BR_W // RET_HEADS
    blk0 = OFF_RET // d

    def col(t, part):
        return pl.BlockSpec((1, t, d), functools.partial(lambda b, h, dec, p: (b, 0, blk0 + p * RET_HEADS + h), p=part))

    table = pl.BlockSpec((t_l, LANE), lambda b, h, dec: (0, 0))
    grid_spec = pltpu.PrefetchScalarGridSpec(
        num_scalar_prefetch=1,
        grid=(bsz, RET_HEADS),
        in_specs=[col(t_l, p) for p in range(4)] + [col(t_c, p) for p in range(4)] + [table, table],
        out_specs=[pl.BlockSpec((1, t_l, d), lambda b, h, dec: (b, 0, h)),
                   pl.BlockSpec((1, t_c, d), lambda b, h, dec: (b, 0, h))],
        scratch_shapes=[pltpu.VMEM((t_l, d), F32), pltpu.VMEM((t_c, d), F32)])
    return pl.pallas_call(
        _ret_kernel,
        grid_spec=grid_spec,
        out_shape=[jax.ShapeDtypeStruct((bsz, t_l, BR_W), F32), jax.ShapeDtypeStruct((bsz, t_c, BR_W), F32)],
        compiler_params=_cparams(("parallel", "parallel")),
        name="retention_branch",
    )(ret_decay, proj_l, proj_l, proj_l, proj_l, proj_c, proj_c, proj_c, proj_c, cos, sin)


SHORT_PAD = 8


def _mlstm_kernel(ql_ref, kl_ref, vl_ref, ol_ref, gl_ref, qc_ref, kc_ref, vc_ref, oc_ref, gc_ref,
                  wq_ref, wk_ref, bq_ref, bk_ref, gb_ref, yl_ref, yc_ref,
                  pad_ref, qsl_ref, ksl_ref, qsc_ref, ksc_ref, hl_ref, hc_ref):
    h = pl.program_id(1)
    t_l, t_c = ql_ref.shape[1], qc_ref.shape[1]
    d = ql_ref.shape[2]
    row = lax.broadcasted_iota(jnp.int32, (CHUNK, CHUNK), 0)
    col = lax.broadcasted_iota(jnp.int32, (CHUNK, CHUNK), 1)
    eye = row == col
    scale = d ** -0.5
    nt = (((1,), (1,)), ((), ()))
    tn = (((0,), (0,)), ((), ()))

    def short_conv(x_ref, w_ref, b_ref, out_ref, t):
        k_taps = w_ref.shape[0]
        first = SHORT_PAD - (k_taps - 1) // 2
        pad_ref[0:SHORT_PAD, :] = jnp.zeros((SHORT_PAD, d), F32)
        pad_ref[SHORT_PAD + t:SHORT_PAD + t + SHORT_PAD, :] = jnp.zeros((SHORT_PAD, d), F32)

        def copy(i, carry):
            pad_ref[pl.ds(pl.multiple_of(SHORT_PAD + i * CHUNK, 8), CHUNK), :] = \
                x_ref[0, pl.ds(pl.multiple_of(i * CHUNK, CHUNK), CHUNK), :]
            return carry

        lax.fori_loop(0, t // CHUNK, copy, 0)

        def conv(i, carry):
            base = pl.multiple_of(i * CHUNK, CHUNK)
            win = pad_ref[pl.ds(base, CHUNK + 2 * SHORT_PAD), :]
            acc = jnp.broadcast_to(b_ref[...], (CHUNK, d))
            for k in range(k_taps):
                acc = acc + w_ref[k:k + 1, :] * win[first + k:first + k + CHUNK]
            out_ref[pl.ds(base, CHUNK), :] = acc * jax.nn.sigmoid(acc)
            return carry

        lax.fori_loop(0, t // CHUNK, conv, 0)

    short_conv(ql_ref, wq_ref, bq_ref, qsl_ref, t_l)
    short_conv(kl_ref, wk_ref, bk_ref, ksl_ref, t_l)
    short_conv(qc_ref, wq_ref, bq_ref, qsc_ref, t_c)
    short_conv(kc_ref, wk_ref, bk_ref, ksc_ref, t_c)

    def gate_cols(g_ref, rows, i_lane, f_lane):
        x = g_ref[0, rows, :] + gb_ref[...]
        i_col = jnp.sum(jnp.where(col == i_lane, x, 0.0), axis=1, keepdims=True)
        f_col = jax.nn.log_sigmoid(jnp.sum(jnp.where(col == f_lane, x, 0.0), axis=1, keepdims=True))
        i_row = jnp.sum(jnp.where(eye, i_col, 0.0), axis=0, keepdims=True)
        f_row = jnp.sum(jnp.where(eye, f_col, 0.0), axis=0, keepdims=True)
        return i_col, f_col, i_row, f_row

    def chunk_step(seq, c, state, forward, accumulate):
        qs_ref, ks_ref, v_ref, g_ref, h_ref = seq
        cmat, nvec, m = state
        rows = pl.ds(pl.multiple_of(c * CHUNK, CHUNK), CHUNK)
        q = qs_ref[rows, :]
        k = ks_ref[rows, :] * scale
        qb, vb = q.astype(BF16), v_ref[0, rows, :].astype(BF16)
        i_col, f_col, i_row, f_row = gate_cols(g_ref, rows, (0 if forward else 2 * MLSTM_HEADS) + h,
                                               (MLSTM_HEADS if forward else 3 * MLSTM_HEADS) + h)
        seen = (col <= row) if forward else (col >= row)
        seen_t = (row <= col) if forward else (row >= col)
        b_col = jnp.sum(jnp.where(seen, f_row, 0.0), axis=1, keepdims=True)
        b_row = jnp.sum(jnp.where(seen_t, f_col, 0.0), axis=0, keepdims=True)
        b_end = jnp.sum(f_row, axis=1, keepdims=True)
        logw = jnp.where(seen, b_col - b_row + i_row, -jnp.inf)
        inter = b_col + m
        m_t = jnp.maximum(inter, jnp.max(logw, axis=1, keepdims=True))
        s = lax.dot_general(qb, k.astype(BF16), nt, preferred_element_type=F32) * jnp.exp(logw - m_t)
        w_prev = jnp.exp(inter - m_t)
        num = (jnp.dot(s.astype(BF16), vb, preferred_element_type=F32)
               + w_prev * jnp.dot(qb, cmat.astype(BF16), preferred_element_type=F32))
        den = jnp.sum(s, axis=1, keepdims=True) + w_prev * jnp.sum(q * nvec, axis=1, keepdims=True)
        hc = num * (1.0 / jnp.maximum(jnp.abs(den), jnp.exp(-m_t)))
        if accumulate:
            h_ref[rows, :] += hc
        else:
            h_ref[rows, :] = hc
        log_wk = b_end - b_col + i_col
        m_new = jnp.maximum(b_end + m, jnp.max(log_wk, axis=0, keepdims=True))
        decay = jnp.exp(b_end + m - m_new)
        kw = k * jnp.exp(log_wk - m_new)
        cmat = decay * cmat + lax.dot_general(kw.astype(BF16), vb, tn, preferred_element_type=F32)
        nvec = decay * nvec + jnp.sum(kw, axis=0, keepdims=True)
        return cmat, nvec, m_new

    lat = (qsl_ref, ksl_ref, vl_ref, gl_ref, hl_ref)
    ctx = (qsc_ref, ksc_ref, vc_ref, gc_ref, hc_ref)
    n_l, n_c = t_l // CHUNK, t_c // CHUNK
    state0 = (jnp.zeros((d, d), F32), jnp.zeros((1, d), F32), jnp.full((1, 1), NEG_INIT, F32))
    st = lax.fori_loop(0, n_c, lambda c, s: chunk_step(ctx, c, s, True, False), state0)
    lax.fori_loop(0, n_l, lambda c, s: chunk_step(lat, c, s, True, False), st)
    st = lax.fori_loop(0, n_c, lambda i, s: chunk_step(ctx, n_c - 1 - i, s, False, True), state0)
    lax.fori_loop(0, n_l, lambda i, s: chunk_step(lat, n_l - 1 - i, s, False, True), st)

    def finish(h_ref, o_ref, y_ref, n):
        def body(c, carry):
            rows = pl.ds(pl.multiple_of(c * CHUNK, CHUNK), CHUNK)
            y_ref[0, rows, :] = _head_norm_rows(h_ref[rows, :]) * jax.nn.sigmoid(o_ref[0, rows, :])
            return carry
        lax.fori_loop(0, n, body, 0)

    finish(hl_ref, ol_ref, yl_ref, n_l)
    finish(hc_ref, oc_ref, yc_ref, n_c)


def mlstm_branch(proj_l, proj_c, conv_w, conv_b, gate_b):
    bsz, t_l, _ = proj_l.shape
    t_c = proj_c.shape[1]
    d = BR_W // MLSTM_HEADS
    blk0 = OFF_MLSTM // d
    k_taps = conv_w.shape[0]

    def col(t, part):
        return pl.BlockSpec((1, t, d), functools.partial(lambda b, h, p: (b, 0, blk0 + p * MLSTM_HEADS + h), p=part))

    def gates(t):
        return pl.BlockSpec((1, t, LANE), lambda b, h: (b, 0, OFF_MGATE // LANE))

    gb = jnp.zeros((1, LANE), F32).at[0, :MLSTM_GATES].set(gate_b.reshape(-1))
    return pl.pallas_call(
        _mlstm_kernel,
        grid=(bsz, MLSTM_HEADS),
        in_specs=[col(t_l, p) for p in range(4)] + [gates(t_l)] + [col(t_c, p) for p in range(4)] + [gates(t_c)] + [
            pl.BlockSpec((k_taps, d), lambda b, h: (0, h)),
            pl.BlockSpec((k_taps, d), lambda b, h: (0, MLSTM_HEADS + h)),
            pl.BlockSpec((1, d), lambda b, h: (0, h)),
            pl.BlockSpec((1, d), lambda b, h: (0, MLSTM_HEADS + h)),
            pl.BlockSpec((1, LANE), lambda b, h: (0, 0))],
        out_specs=[pl.BlockSpec((1, t_l, d), lambda b, h: (b, 0, h)),
                   pl.BlockSpec((1, t_c, d), lambda b, h: (b, 0, h))],
        out_shape=[jax.ShapeDtypeStruct((bsz, t_l, BR_W), F32), jax.ShapeDtypeStruct((bsz, t_c, BR_W), F32)],
        scratch_shapes=[pltpu.VMEM((t_l + 2 * SHORT_PAD, d), F32),
                        pltpu.VMEM((t_l, d), F32), pltpu.VMEM((t_l, d), F32),
                        pltpu.VMEM((t_c, d), F32), pltpu.VMEM((t_c, d), F32),
                        pltpu.VMEM((t_l, d), F32), pltpu.VMEM((t_c, d), F32)],
        compiler_params=_cparams(("parallel", "parallel")),
        name="mlstm_branch",
    )(proj_l, proj_l, proj_l, proj_l, proj_l, proj_c, proj_c, proj_c, proj_c, proj_c,
      conv_w, conv_w, conv_b.reshape(1, -1), conv_b.reshape(1, -1), gb)


def _attn_kernel(sink_ref, q_ref, *refs, local):
    if local:
        kp_ref, kc_ref, kn_ref, vp_ref, vc_ref, vn_ref, kx_ref, vx_ref, cos_ref, sin_ref, o_ref, k_scr, v_scr = refs
    else:
        kx_ref, vx_ref, o_ref, k_scr, v_scr = refs
    i = pl.program_id(1)
    nq = pl.num_programs(1)
    t_x = kx_ref.shape[1]
    n_loc = 3 * BLOCK if local else 0
    n_keys = n_loc + t_x
    nf = ATTN_DIM // 4
    half = LANE // 2

    def put(dst, lo, x):
        dst[0, lo:lo + x.shape[0], :] = x.astype(BF16)
        dst[1, lo:lo + x.shape[0], :] = pltpu.roll(x, half, 1).astype(BF16)

    if local:
        for j, (k_ref, v_ref) in enumerate(((kp_ref, vp_ref), (kc_ref, vc_ref), (kn_ref, vn_ref))):
            blk = jnp.clip(i + (j - 1), 0, nq - 1)
            rows = pl.ds(pl.multiple_of(blk * BLOCK, BLOCK), BLOCK)
            put(k_scr, j * BLOCK, _rope(k_ref[0], cos_ref[rows, :], sin_ref[rows, :], nf))
            put(v_scr, j * BLOCK, v_ref[0])
    put(k_scr, n_loc, kx_ref[0])
    put(v_scr, n_loc, vx_ref[0])

    lane = lax.broadcasted_iota(jnp.int32, (BLOCK, LANE), 1)
    if local:
        rel = (lax.broadcasted_iota(jnp.int32, (BLOCK, n_keys), 1)
               - lax.broadcasted_iota(jnp.int32, (BLOCK, n_keys), 0))
        kpos = (i - 1) * BLOCK + lax.broadcasted_iota(jnp.int32, (BLOCK, n_keys), 1)
        ok = (rel >= 0) & (rel <= 2 * WINDOW) & (kpos >= 0) & (kpos < nq * BLOCK)
        ok = ok | (lax.broadcasted_iota(jnp.int32, (BLOCK, n_keys), 1) >= n_loc)
        bias = jnp.where(ok, 0.0, -jnp.inf)
        qrows = pl.ds(pl.multiple_of(i * BLOCK, BLOCK), BLOCK)
        cos_q, sin_q = cos_ref[qrows, :], sin_ref[qrows, :]
    nt = (((1,), (1,)), ((), ()))
    for p in range(ATTN_HEADS // 2):
        qp = q_ref[0, :, p * LANE:(p + 1) * LANE]
        if local:
            qp = _rope(qp, cos_q, sin_q, nf)
        qp = qp * (ATTN_DIM ** -0.5)
        outs = []
        for sub in range(2):
            hd = 2 * p + sub
            kv_half = hd // (ATTN_HEADS // ATTN_KV_HEADS)
            sel = (lane >= half) if sub else (lane < half)
            qm = jnp.where(sel, qp, 0.0).astype(BF16)
            swap = 0 if kv_half == sub else 1
            s = lax.dot_general(qm, k_scr[swap], nt, preferred_element_type=F32)
            if local:
                s = s + bias
            sink = jnp.full((BLOCK, 1), sink_ref[hd], F32)
            m = jnp.maximum(jnp.max(s, axis=1, keepdims=True), sink)
            e = jnp.exp(s - m)
            inv = 1.0 / (jnp.sum(e, axis=1, keepdims=True) + jnp.exp(sink - m))
            outs.append(jnp.dot((e * inv).astype(BF16), v_scr[swap], preferred_element_type=F32))
        o_ref[0, :, p * LANE:(p + 1) * LANE] = jnp.where(lane < half, outs[0], outs[1])


def attention_branch(proj_q, proj_c, sink, cos, sin, local):
    bsz, t, _ = proj_q.shape
    t_c = proj_c.shape[1]
    nq = t // BLOCK
    n_qcol = ATTN_HEADS * ATTN_DIM
    q_blk = OFF_ATTN // n_qcol
    k_blk = (OFF_ATTN + n_qcol) // LANE
    v_blk = k_blk + 1
    n_keys = (3 * BLOCK if local else 0) + t_c

    def kv(blk, shift):
        return pl.BlockSpec((1, BLOCK, LANE), lambda b, i, s: (b, jnp.clip(i + shift, 0, nq - 1), blk))

    in_specs = [pl.BlockSpec((1, BLOCK, n_qcol), lambda b, i, s: (b, i, q_blk))]
    args = [proj_q]
    if local:
        in_specs += [kv(k_blk, -1), kv(k_blk, 0), kv(k_blk, 1), kv(v_blk, -1), kv(v_blk, 0), kv(v_blk, 1)]
        args += [proj_q] * 6
    in_specs += [pl.BlockSpec((1, t_c, LANE), lambda b, i, s: (b, 0, k_blk)),
                 pl.BlockSpec((1, t_c, LANE), lambda b, i, s: (b, 0, v_blk))]
    args += [proj_c, proj_c]
    if local:
        in_specs += [pl.BlockSpec((t, LANE), lambda b, i, s: (0, 0))] * 2
        args += [cos, sin]
    grid_spec = pltpu.PrefetchScalarGridSpec(
        num_scalar_prefetch=1,
        grid=(bsz, nq),
        in_specs=in_specs,
        out_specs=pl.BlockSpec((1, BLOCK, n_qcol), lambda b, i, s: (b, i, 0)),
        scratch_shapes=[pltpu.VMEM((2, n_keys, LANE), BF16), pltpu.VMEM((2, n_keys, LANE), BF16)])
    return pl.pallas_call(
        functools.partial(_attn_kernel, local=local),
        grid_spec=grid_spec,
        out_shape=jax.ShapeDtypeStruct((bsz, t, n_qcol), F32),
        compiler_params=_cparams(("parallel", "parallel")),
        name="attention_branch",
    )(sink, *args)


def _pack_w_in(w):
    d = w.shape[0]
    used = w.shape[1] - MLSTM_GATES + LANE
    cols = [w[:, :REF_MGATE_OFF], w[:, REF_GATE_OFF:], w[:, REF_ATTN_OFF:REF_GATE_OFF],
            w[:, REF_MGATE_OFF:REF_ATTN_OFF],
            jnp.zeros((d, _round_up(used, PROJ_TILE_N) - w.shape[1]), w.dtype)]
    return jnp.concatenate(cols, axis=1).astype(BF16)


def _final_norm(x, g):
    y = x * lax.rsqrt(jnp.mean(x * x, axis=-1, keepdims=True) + EPS)
    return y * g


def kernel(x, c, ctx, c_ctx, mod_w, mod_b, norm1_g, norm2_g, w_in, conv_w, conv_b, conv_ln_g, conv_ln_b,
           ret_decay, mlstm_conv_w, mlstm_conv_b, mlstm_gate_b, attn_sink, w_branch, w_out,
           peer_wq, peer_keys, peer_u, peer_v, final_g):
    bsz, seq, d = x.shape
    assert d == D_MODEL
    depth = mod_w.shape[0]
    cos_r, sin_r = rope_tables(seq, BR_W // RET_HEADS)
    cos_a, sin_a = rope_tables(seq, ATTN_DIM)
    n_cond = _round_up(bsz + 1, 8)
    cond = jnp.concatenate([c, c_ctx[None], jnp.zeros((n_cond - bsz - 1, d), F32)], axis=0)
    for l in range(depth):
        last = l == depth - 1
        mod = mod_vectors(cond, mod_w[l], mod_b[l])
        sh1, sc1, ga1, sh2, sc2, ga2 = (m[:, None, :] for m in jnp.split(mod[:bsz], 6, axis=-1))
        csh1, csc1, cga1, csh2, csc2, cga2 = (jnp.broadcast_to(m[None], (bsz, 1, d))
                                              for m in jnp.split(mod[bsz:bsz + 1], 6, axis=-1))
        w_in_p = _pack_w_in(w_in[l])
        proj_l = norm_matmul(x, norm1_g[l], sh1, sc1, w_in_p)
        proj_c = norm_matmul(ctx, norm1_g[l], csh1, csc1, w_in_p)
        ret_l, ret_c = retention_branch(proj_l, proj_c, ret_decay[l], cos_r, sin_r)
        mls_l, mls_c = mlstm_branch(proj_l, proj_c, mlstm_conv_w[l], mlstm_conv_b[l], mlstm_gate_b[l])
        conv_args = (conv_w[l], conv_b[l], conv_ln_g[l], conv_ln_b[l])
        ys_l = (conv_branch(proj_l, *conv_args), ret_l, mls_l,
                attention_branch(proj_l, proj_c, attn_sink[l], cos_a, sin_a, True))
        if not last:
            ys_c = (conv_branch(proj_c, *conv_args), ret_c, mls_c,
                    attention_branch(proj_c, proj_c, attn_sink[l], cos_a, sin_a, False))
        wb, wo = w_branch[l].astype(BF16), w_out[l].astype(BF16)
        wqt = peer_wq[l].T.astype(BF16)
        u = peer_u[l].astype(BF16)
        vt = peer_v[l].T.astype(BF16)

        def peer(xx, sh, sc, ga):
            xm, *tabs = peer_prep(xx, norm2_g[l], sh, sc, wqt, peer_keys[l])
            return peer_dense(xm, *tabs, u, vt, xx, ga)

        x = merge(ys_l, proj_l, wb, wo, x, ga1)
        x = peer(x, sh2, sc2, ga2)
        if not last:
            ctx = merge(ys_c, proj_c, wb, wo, ctx, cga1)
            ctx = peer(ctx, csh2, csc2, cga2)
    return _final_norm(x, final_g)
```

```python
import functools

import jax
import jax.numpy as jnp
import numpy as np
from jax import lax
from jax.experimental import pallas as pl
from jax.experimental.pallas import tpu as pltpu

F32 = jnp.float32
BF16 = jnp.bfloat16

GRID_W = 64
EPS = 1e-6
N_BRANCH = 4
BR_W = 512
RET_HEADS = 4
MLSTM_HEADS = 4
MLSTM_GATES = 4 * MLSTM_HEADS
ATTN_HEADS = 8
ATTN_KV_HEADS = 2
ATTN_DIM = BR_W // ATTN_HEADS
WINDOW = 128
BLOCK = 128
CHUNK = 128
ROPE_BASE = 10000.0
NEG_INIT = -1e30
PEER_HEADS = 8
PEER_KEYS = 128
PEER_TOPK = 16

CONV_IN = 2 * BR_W
RET_IN = 4 * BR_W
MLSTM_QKVO = 4 * BR_W
ATTN_IN = ATTN_HEADS * ATTN_DIM + 2 * ATTN_KV_HEADS * ATTN_DIM
REF_MGATE_OFF = CONV_IN + RET_IN + MLSTM_QKVO
REF_ATTN_OFF = REF_MGATE_OFF + MLSTM_GATES
REF_GATE_OFF = REF_ATTN_OFF + ATTN_IN
OFF_CONV = 0
OFF_RET = OFF_CONV + CONV_IN
OFF_MLSTM = OFF_RET + RET_IN
OFF_GATE = OFF_MLSTM + MLSTM_QKVO
D_MODEL = 1024
OFF_ATTN = OFF_GATE + N_BRANCH * D_MODEL
OFF_MGATE = OFF_ATTN + ATTN_IN
LANE = 128
BF16_ROWS = 16
PROJ_TILE_N = 1024


def _round_up(a, m):
    return (a + m - 1) // m * m


VMEM_LIMIT = 56 * 1024 * 1024


def _cparams(sem):
    return pltpu.CompilerParams(dimension_semantics=sem, vmem_limit_bytes=VMEM_LIMIT)


def _mod_kernel(s_ref, w_ref, b_ref, o_ref):
    s = s_ref[...]
    s = s * jax.nn.sigmoid(s)
    o_ref[...] = jnp.dot(s, w_ref[...], preferred_element_type=F32,
                         precision=lax.Precision.HIGHEST) + b_ref[...]


def mod_vectors(cond, w, b):
    r, d = cond.shape
    n = w.shape[1]
    tn = 512
    return pl.pallas_call(
        _mod_kernel,
        grid=(n // tn,),
        in_specs=[pl.BlockSpec((r, d), lambda j: (0, 0)),
                  pl.BlockSpec((d, tn), lambda j: (0, j)),
                  pl.BlockSpec((1, tn), lambda j: (0, j))],
        out_specs=pl.BlockSpec((r, tn), lambda j: (0, j)),
        out_shape=jax.ShapeDtypeStruct((r, n), F32),
        compiler_params=_cparams(("parallel",)),
        name="mod_vectors",
    )(cond, w, b.reshape(1, n))


def _norm_matmul_kernel(x_ref, g_ref, sh_ref, sc_ref, w_ref, o_ref, *rest, emit_h):
    h_ref = rest[-1]

    @pl.when(pl.program_id(2) == 0)
    def _():
        x = x_ref[0]
        y = x * lax.rsqrt(jnp.mean(x * x, axis=-1, keepdims=True) + EPS)
        h = (y * g_ref[...]) * (1.0 + sc_ref[0]) + sh_ref[0]
        h_ref[...] = h.astype(BF16)
        if emit_h:
            rest[0][0] = h.astype(BF16)

    o_ref[0] = jnp.dot(h_ref[...], w_ref[...], preferred_element_type=F32)


def norm_matmul(x, g, shift, scale, w, emit_h=False):
    bsz, t, d = x.shape
    n = w.shape[1]
    tm = min(t, 1024)
    tn = min(n, PROJ_TILE_N)
    out_shape = [jax.ShapeDtypeStruct((bsz, t, n), F32)]
    out_specs = [pl.BlockSpec((1, tm, tn), lambda b, i, j: (b, i, j))]
    if emit_h:
        out_shape.append(jax.ShapeDtypeStruct((bsz, t, d), BF16))
        out_specs.append(pl.BlockSpec((1, tm, d), lambda b, i, j: (b, i, 0)))
    res = pl.pallas_call(
        functools.partial(_norm_matmul_kernel, emit_h=emit_h),
        grid=(bsz, t // tm, n // tn),
        in_specs=[pl.BlockSpec((1, tm, d), lambda b, i, j: (b, i, 0)),
                  pl.BlockSpec((1, d), lambda b, i, j: (0, 0)),
                  pl.BlockSpec((1, 1, d), lambda b, i, j: (b, 0, 0)),
                  pl.BlockSpec((1, 1, d), lambda b, i, j: (b, 0, 0)),
                  pl.BlockSpec((d, tn), lambda b, i, j: (0, j))],
        out_specs=out_specs,
        out_shape=out_shape,
        scratch_shapes=[pltpu.VMEM((tm, d), BF16)],
        compiler_params=_cparams(("parallel", "parallel", "arbitrary")),
        name="norm_matmul",
    )(x, g.reshape(1, d), shift, scale, w)
    return res if emit_h else res[0]


def _merge_kernel(y0, y1, y2, y3, g0, g1, g2, g3, wb_ref, wo_ref, x_ref, ga_ref, o_ref):
    acc = None
    for i, (y, g) in enumerate(((y0, g0), (y1, g1), (y2, g2), (y3, g3))):
        p = jnp.dot(y[0].astype(BF16), wb_ref[i], preferred_element_type=F32)
        t = jax.nn.sigmoid(g[0]) * p
        acc = t if acc is None else acc + t
    o = jnp.dot(acc.astype(BF16), wo_ref[...], preferred_element_type=F32)
    o_ref[0] = x_ref[0] + ga_ref[0] * o


def merge(ys, proj, w_branch, w_out, x, ga):
    bsz, t, d = x.shape
    tm = min(t, 512)
    gate_blk = OFF_GATE // d
    y_spec = pl.BlockSpec((1, tm, BR_W), lambda b, i: (b, i, 0))
    g_specs = [pl.BlockSpec((1, tm, d), functools.partial(lambda b, i, k: (b, i, gate_blk + k), k=k))
               for k in range(N_BRANCH)]
    return pl.pallas_call(
        _merge_kernel,
        grid=(bsz, t // tm),
        in_specs=[y_spec] * N_BRANCH + g_specs + [
            pl.BlockSpec((N_BRANCH, BR_W, d), lambda b, i: (0, 0, 0)),
            pl.BlockSpec((d, d), lambda b, i: (0, 0)),
            pl.BlockSpec((1, tm, d), lambda b, i: (b, i, 0)),
            pl.BlockSpec((1, 1, d), lambda b, i: (b, 0, 0))],
        out_specs=pl.BlockSpec((1, tm, d), lambda b, i: (b, i, 0)),
        out_shape=jax.ShapeDtypeStruct((bsz, t, d), F32),
        compiler_params=_cparams(("parallel", "parallel")),
        name="merge",
    )(*ys, proj, proj, proj, proj, w_branch, w_out, x, ga)


PEER_TE = 1024
PEER_SUB = 512
PEER_SECOND = 1
PEER_GROUP = 4


def _gelu_exact(x):
    return 0.5 * x * (1.0 + lax.erf(x * np.float32(1.0 / np.sqrt(2.0))))


def _peer_kernel(xm_ref, a_ref, lb_ref, b_ref, r2_ref, u_ref, vt_ref, x_ref, ga_ref, o_ref,
                 acc_ref, w_ref, h_ref):
    j = pl.program_id(2)

    @pl.when(j == 0)
    def _():
        acc_ref[...] = jnp.zeros_like(acc_ref)

    tn = xm_ref.shape[1]

    def spread(row):
        return jnp.broadcast_to(row, (BF16_ROWS, LANE)).astype(BF16)[None]

    def slabs(words):
        return pltpu.bitcast(words, BF16).reshape(PEER_KEYS // BF16_ROWS, BF16_ROWS, LANE)

    n_sub = PEER_TE // PEER_SUB

    def first_matmul(sub):
        h_ref[sub % 2] = lax.dot_general(u_ref[pl.ds(sub * PEER_SUB, PEER_SUB), :], xm_ref[0],
                                         (((1,), (1,)), ((), ())), preferred_element_type=F32)

    first_matmul(0)
    for sub in range(n_sub):
        if sub + 1 < n_sub:
            first_matmul(sub + 1)
        for grp in range(PEER_SUB // (PEER_GROUP * PEER_KEYS)):
            row0 = sub * (PEER_SUB // PEER_KEYS) + grp * PEER_GROUP
            first_rows = [pl.ds(j * (PEER_TE // PEER_KEYS) + row0 + a, 1) for a in range(PEER_GROUP)]
            for blk in range(tn // LANE):
                lanes = pl.ds(blk * LANE, LANE)
                lo = blk * LANE
                g = [None] * PEER_GROUP
                for h in range(PEER_HEADS):
                    r2 = slabs(r2_ref[0, h, :, lanes])
                    b = slabs(b_ref[0, h, :, lanes])
                    for a in range(PEER_GROUP):
                        limit = spread(lb_ref[0, h, first_rows[a], :][:, lo:lo + LANE])
                        t = jnp.where(r2 < limit, b, 0.0) * spread(a_ref[0, h, first_rows[a], :][:, lo:lo + LANE])
                        g[a] = t if g[a] is None else g[a] + t
                for a in range(PEER_GROUP):
                    local = pl.ds((grp * PEER_GROUP + a) * PEER_KEYS, PEER_KEYS)
                    act = _gelu_exact(h_ref[sub % 2, local, lanes]).astype(BF16)
                    w_ref[pl.ds((row0 + a) * PEER_KEYS, PEER_KEYS), lanes] = g[a].reshape(PEER_KEYS, LANE) * act
        if sub % PEER_SECOND == PEER_SECOND - 1:
            experts = pl.ds((sub + 1 - PEER_SECOND) * PEER_SUB, PEER_SECOND * PEER_SUB)
            acc_ref[...] += jnp.dot(vt_ref[:, experts], w_ref[experts, :], preferred_element_type=F32)

    @pl.when(j == pl.num_programs(2) - 1)
    def _():
        o_ref[0] = x_ref[0] + ga_ref[0] * acc_ref[...].T


def peer_dense(xm, a_t, lb_t, b_t, r2_t, u, vt, x, ga):
    bsz, t, d = x.shape
    n_exp = u.shape[0]
    tn = min(t, 512)
    tab = pl.BlockSpec((1, PEER_HEADS, PEER_KEYS, tn), lambda b, i, j: (b, 0, 0, i))
    pair = pl.BlockSpec((1, PEER_HEADS, PEER_KEYS // 2, tn), lambda b, i, j: (b, 0, 0, i))
    return pl.pallas_call(
        _peer_kernel,
        grid=(bsz, t // tn, n_exp // PEER_TE),
        in_specs=[pl.BlockSpec((1, tn, d), lambda b, i, j: (b, i, 0)),
                  tab, tab, pair, pair,
                  pl.BlockSpec((PEER_TE, d), lambda b, i, j: (j, 0)),
                  pl.BlockSpec((d, PEER_TE), lambda b, i, j: (0, j)),
                  pl.BlockSpec((1, tn, d), lambda b, i, j: (b, i, 0)),
                  pl.BlockSpec((1, 1, d), lambda b, i, j: (b, 0, 0))],
        out_specs=pl.BlockSpec((1, tn, d), lambda b, i, j: (b, i, 0)),
        out_shape=jax.ShapeDtypeStruct((bsz, t, d), F32),
        scratch_shapes=[pltpu.VMEM((d, tn), F32), pltpu.VMEM((PEER_TE, tn), BF16),
                        pltpu.VMEM((2, PEER_SUB, tn), F32)],
        compiler_params=_cparams(("parallel", "parallel", "arbitrary")),
        name="peer_dense",
    )(xm, a_t, lb_t, b_t, r2_t, u, vt, x, ga)


def _extract_top(s, row, n_pick, lowest_row_only):
    rank = jnp.full(s.shape, float(n_pick), F32)
    tops = []
    for k in range(n_pick):
        m = jnp.max(s, axis=0, keepdims=True)
        hit = s == m
        if lowest_row_only:
            hit = row == jnp.min(jnp.where(hit, row, s.shape[0]), axis=0, keepdims=True)
        rank = jnp.where(hit, float(k), rank)
        s = jnp.where(hit, -jnp.inf, s)
        tops.append(m)
    n_marked = jnp.sum(jnp.where(rank < float(n_pick), 1.0, 0.0), axis=0, keepdims=True)
    n_bad = jnp.sum(jnp.where(n_marked != float(n_pick), 1.0, 0.0))
    return rank, jnp.concatenate(tops, axis=0), n_bad


def _peer_prep_kernel(x_ref, g_ref, sh_ref, sc_ref, wqt_ref, keys_ref,
                      xm_ref, a_ref, lb_ref, b_ref, r2_ref, qt_ref):
    tn = x_ref.shape[1]
    x = x_ref[0]
    y = x * lax.rsqrt(jnp.mean(x * x, axis=-1, keepdims=True) + EPS)
    xm = ((y * g_ref[...]) * (1.0 + sc_ref[0]) + sh_ref[0]).astype(BF16)
    xm_ref[0] = xm
    qt_ref[...] = lax.dot_general(wqt_ref[...], xm, (((1,), (1,)), ((), ())), preferred_element_type=F32)
    half = keys_ref.shape[2]
    row = lax.broadcasted_iota(jnp.int32, (PEER_KEYS, LANE), 0)
    sub = lax.broadcasted_iota(jnp.int32, (8, LANE), 0)
    n_cand_rows = PEER_TOPK + 8 * (PEER_TOPK // 2 - 1) + PEER_TOPK // 2
    crow = lax.broadcasted_iota(jnp.int32, (n_cand_rows, LANE), 0)

    def head_tables(h, exact):
        n_bad = 0.0
        for blk in range(tn // LANE):
            lanes = pl.ds(blk * LANE, LANE)
            s, rank, tops = [], [], []
            for c in range(2):
                qs = qt_ref[pl.ds(pl.multiple_of((2 * h + c) * half, half), half), lanes]
                sc_ = jnp.dot(keys_ref[c], qs, preferred_element_type=F32, precision=lax.Precision.HIGHEST)
                r, t, bad = _extract_top(sc_, row, PEER_TOPK, exact)
                n_bad = n_bad + bad
                s.append(sc_)
                rank.append(r)
                tops.append(t)
            t1, t2 = tops
            groups = [t1[0:1] + t2]
            for a in range(1, PEER_TOPK // 2):
                n_valid = PEER_TOPK // (a + 1)
                grp = t1[a:a + 1] + t2[0:8]
                groups.append(grp if n_valid >= 8 else jnp.where(sub < n_valid, grp, -jnp.inf))
            groups.append(t1[PEER_TOPK // 2:] + t2[0:1])
            cand = jnp.concatenate(groups, axis=0)
            crank, _, bad = _extract_top(cand, crow, PEER_TOPK, exact)
            n_bad = n_bad + bad
            picked = crank < float(PEER_TOPK)
            z = jnp.sum(jnp.where(picked, jnp.exp(cand - cand[0:1]), 0.0), axis=0, keepdims=True)
            pk = picked.astype(F32)
            counts = [jnp.sum(pk[0:PEER_TOPK], axis=0, keepdims=True)]
            for a in range(1, PEER_TOPK // 2):
                lo = PEER_TOPK + 8 * (a - 1)
                counts.append(jnp.sum(pk[lo:lo + 8], axis=0, keepdims=True))
            lo = PEER_TOPK + 8 * (PEER_TOPK // 2 - 1)
            for a in range(PEER_TOPK // 2):
                counts.append(pk[lo + a:lo + a + 1])
            lb = jnp.zeros((PEER_KEYS, LANE), F32)
            for a in range(PEER_TOPK):
                lb = jnp.where(rank[0] == float(a), counts[a], lb)
            a_ref[0, h, :, lanes] = jnp.exp(s[0] - t1[0:1])
            lb_ref[0, h, :, lanes] = lb
            b_ref[0, h, :, lanes] = pltpu.bitcast((jnp.exp(s[1] - t2[0:1]) / z).astype(BF16), jnp.uint32)
            r2_ref[0, h, :, lanes] = pltpu.bitcast(rank[1].astype(BF16), jnp.uint32)
        return n_bad

    def head_body(h, carry):
        n_bad = head_tables(h, False)

        @pl.when(n_bad > 0.0)
        def _():
            head_tables(h, True)

        return carry

    lax.fori_loop(0, PEER_HEADS, head_body, 0)


def peer_prep(x, g, shift, scale, wqt, keys):
    bsz, t, d = x.shape
    tn = min(t, 256)
    nq = wqt.shape[0]
    tab_shape = jax.ShapeDtypeStruct((bsz, PEER_HEADS, PEER_KEYS, t), F32)
    tab_spec = pl.BlockSpec((1, PEER_HEADS, PEER_KEYS, tn), lambda b, i: (b, 0, 0, i))
    pair_shape = jax.ShapeDtypeStruct((bsz, PEER_HEADS, PEER_KEYS // 2, t), jnp.uint32)
    pair_spec = pl.BlockSpec((1, PEER_HEADS, PEER_KEYS // 2, tn), lambda b, i: (b, 0, 0, i))
    return pl.pallas_call(
        _peer_prep_kernel,
        grid=(bsz, t // tn),
        in_specs=[pl.BlockSpec((1, tn, d), lambda b, i: (b, i, 0)),
                  pl.BlockSpec((1, d), lambda b, i: (0, 0)),
                  pl.BlockSpec((1, 1, d), lambda b, i: (b, 0, 0)),
                  pl.BlockSpec((1, 1, d), lambda b, i: (b, 0, 0)),
                  pl.BlockSpec((nq, d), lambda b, i: (0, 0)),
                  pl.BlockSpec(keys.shape, lambda b, i: (0, 0, 0))],
        out_specs=[pl.BlockSpec((1, tn, d), lambda b, i: (b, i, 0)), tab_spec, tab_spec, pair_spec, pair_spec],
        out_shape=[jax.ShapeDtypeStruct((bsz, t, d), BF16), tab_shape, tab_shape, pair_shape, pair_shape],
        scratch_shapes=[pltpu.VMEM((nq, tn), F32)],
        compiler_params=_cparams(("parallel", "parallel")),
        name="peer_prep",
    )(x, g.reshape(1, d), shift, scale, wqt, keys)


ROW_TILE = 128
CONV_PAD = 16


def _conv_kernel(p_ref, w_ref, b_ref, lg_ref, lb_ref, o_ref, pad_ref, y_ref):
    t, c = o_ref.shape[1], o_ref.shape[2]
    k_taps = w_ref.shape[0]
    first = CONV_PAD - (k_taps - 1) // 2
    lane_w = 256
    pad_ref[0:CONV_PAD, :] = jnp.zeros((CONV_PAD, c), F32)
    pad_ref[CONV_PAD + t:, :] = jnp.zeros((CONV_PAD, c), F32)

    def glu_body(i, carry):
        rows = pl.ds(pl.multiple_of(i * ROW_TILE, ROW_TILE), ROW_TILE)
        a = p_ref[0, rows, 0:c]
        g = p_ref[0, rows, c:2 * c]
        pad_ref[pl.ds(pl.multiple_of(CONV_PAD + i * ROW_TILE, 8), ROW_TILE), :] = a * jax.nn.sigmoid(g)
        return carry

    lax.fori_loop(0, t // ROW_TILE, glu_body, 0)

    def conv_body(i, carry):
        base = pl.multiple_of(i * ROW_TILE, ROW_TILE)
        for lb in range(c // lane_w):
            lanes = pl.ds(lb * lane_w, lane_w)
            win = pad_ref[pl.ds(base, ROW_TILE + 2 * CONV_PAD), lanes]
            acc = jnp.broadcast_to(b_ref[:, lanes], (ROW_TILE, lane_w))
            for k in range(k_taps):
                acc = acc + w_ref[k:k + 1, lanes] * win[first + k:first + k + ROW_TILE]
            y_ref[:, lanes] = acc
        y = y_ref[...]
        yc = y - jnp.mean(y, axis=-1, keepdims=True)
        z = yc * lax.rsqrt(jnp.mean(yc * yc, axis=-1, keepdims=True) + EPS) * lg_ref[...] + lb_ref[...]
        o_ref[0, pl.ds(base, ROW_TILE), :] = z * jax.nn.sigmoid(z)
        return carry

    lax.fori_loop(0, t // ROW_TILE, conv_body, 0)


def conv_branch(proj, w, b, ln_g, ln_b):
    bsz, t, _ = proj.shape
    k_taps, c = w.shape
    vec = pl.BlockSpec((1, c), lambda i: (0, 0))
    return pl.pallas_call(
        _conv_kernel,
        grid=(bsz,),
        in_specs=[pl.BlockSpec((1, t, 2 * c), lambda i: (i, 0, OFF_CONV // (2 * c))),
                  pl.BlockSpec((k_taps, c), lambda i: (0, 0)), vec, vec, vec],
        out_specs=pl.BlockSpec((1, t, c), lambda i: (i, 0, 0)),
        out_shape=jax.ShapeDtypeStruct((bsz, t, c), F32),
        scratch_shapes=[pltpu.VMEM((t + 2 * CONV_PAD, c), F32), pltpu.VMEM((ROW_TILE, c), F32)],
        compiler_params=_cparams(("parallel",)),
        name="conv_branch",
    )(proj, w, b.reshape(1, c), ln_g.reshape(1, c), ln_b.reshape(1, c))


def rope_tables(n_tok, head_dim):
    nf = head_dim // 4
    lane = np.arange(LANE)
    d_idx = lane % head_dim
    by_col = d_idx >= head_dim // 2
    freqs = ROPE_BASE ** (-(d_idx % nf).astype(np.float32) / nf)
    tok = np.arange(n_tok)
    pos = np.where(by_col[None, :], (tok % GRID_W)[:, None], (tok // GRID_W)[:, None]).astype(np.float32)
    ang = jnp.asarray(pos) * jnp.asarray(freqs, F32)[None, :]
    sign = np.where(d_idx % (2 * nf) < nf, -1.0, 1.0).astype(np.float32)
    return jnp.cos(ang), jnp.sin(ang) * sign[None, :]


def _rope(x, cos, sin_signed, nf):
    lane = lax.broadcasted_iota(jnp.int32, x.shape, 1)
    partner = jnp.where(lane % (2 * nf) < nf, pltpu.roll(x, LANE - nf, 1), pltpu.roll(x, nf, 1))
    return x * cos + partner * sin_signed


FINISH_ROWS = 256


def _head_norm_rows(o):
    oc = o - jnp.mean(o, axis=-1, keepdims=True)
    return oc * lax.rsqrt(jnp.mean(oc * oc, axis=-1, keepdims=True) + EPS)


def _ret_kernel(dec_ref, ql_ref, kl_ref, vl_ref, gl_ref, qc_ref, kc_ref, vc_ref, gc_ref, cos_ref, sin_ref,
                yl_ref, yc_ref, ol_ref, oc_ref, olb_ref, ocb_ref):
    h = pl.program_id(1)
    n_l, n_c = ql_ref.shape[1] // CHUNK, qc_ref.shape[1] // CHUNK
    d = ql_ref.shape[2]
    row = lax.broadcasted_iota(jnp.int32, (CHUNK, CHUNK), 0).astype(F32)
    col = lax.broadcasted_iota(jnp.int32, (CHUNK, CHUNK), 1).astype(F32)
    lg_f = jax.nn.log_sigmoid(jnp.full((CHUNK, CHUNK), dec_ref[0, h], F32))
    lg_b = jax.nn.log_sigmoid(jnp.full((CHUNK, CHUNK), dec_ref[1, h], F32))
    dmat = (jnp.where(row >= col, jnp.exp((row - col) * lg_f), 0.0)
            + jnp.where(col >= row, jnp.exp((col - row) * lg_b), 0.0))
    qdec_f, kdec_f, cdec_f = jnp.exp((row + 1.0) * lg_f), jnp.exp((CHUNK - 1.0 - row) * lg_f), jnp.exp(CHUNK * lg_f)
    qdec_b, kdec_b, cdec_b = jnp.exp((CHUNK - row) * lg_b), jnp.exp(row * lg_b), jnp.exp(CHUNK * lg_b)
    scale = d ** -0.5
    nt = (((1,), (1,)), ((), ()))
    tn = (((0,), (0,)), ((), ()))

    def load(refs, c, rotate):
        q_ref, k_ref, v_ref = refs
        rows = pl.ds(pl.multiple_of(c * CHUNK, CHUNK), CHUNK)
        q, k, v = q_ref[0, rows, :], k_ref[0, rows, :], v_ref[0, rows, :]
        if rotate:
            cos, sin = cos_ref[rows, :], sin_ref[rows, :]
            q, k = _rope(q, cos, sin, d // 4), _rope(k, cos, sin, d // 4)
        return q.astype(BF16), k * scale, v.astype(BF16), rows

    def both(refs, of_ref, ob_ref, rotate, n):
        def body(i, carry):
            s_f, s_b = carry
            q, k, v, rows = load(refs, i, rotate)
            att = lax.dot_general(q, k.astype(BF16), nt, preferred_element_type=F32) * dmat
            o = jnp.dot(att.astype(BF16), v, preferred_element_type=F32)
            of_ref[rows, :] = o + qdec_f * jnp.dot(q, s_f.astype(BF16), preferred_element_type=F32)
            s_f = cdec_f * s_f + lax.dot_general((k * kdec_f).astype(BF16), v, tn, preferred_element_type=F32)
            q, k, v, rows = load(refs, n - 1 - i, rotate)
            ob_ref[rows, :] = qdec_b * jnp.dot(q, s_b.astype(BF16), preferred_element_type=F32)
            s_b = cdec_b * s_b + lax.dot_general((k * kdec_b).astype(BF16), v, tn, preferred_element_type=F32)
            return s_f, s_b
        return body

    lat, ctx = (ql_ref, kl_ref, vl_ref), (qc_ref, kc_ref, vc_ref)
    s0 = jnp.zeros((d, d), F32)
    state = lax.fori_loop(0, n_c, both(ctx, oc_ref, ocb_ref, False, n_c), (s0, s0))
    lax.fori_loop(0, n_l, both(lat, ol_ref, olb_ref, True, n_l), state)

    def finish(of_ref, ob_ref, g_ref, y_ref):
        def body(c, carry):
            rows = pl.ds(pl.multiple_of(c * FINISH_ROWS, FINISH_ROWS), FINISH_ROWS)
            g = g_ref[0, rows, :]
            y_ref[0, rows, :] = _head_norm_rows(of_ref[rows, :] + ob_ref[rows, :]) * (g * jax.nn.sigmoid(g))
            return carry
        lax.fori_loop(0, of_ref.shape[0] // FINISH_ROWS, body, 0)

    finish(ol_ref, olb_ref, gl_ref, yl_ref)
    finish(oc_ref, ocb_ref, gc_ref, yc_ref)


def retention_branch(proj_l, proj_c, ret_decay, cos, sin):
    bsz, t_l, _ = proj_l.shape
    t_c = proj_c.shape[1]
    d = BR_W // RET_HEADS
    blk0 = OFF_RET // d

    def col(t, part):
        return pl.BlockSpec((1, t, d), functools.partial(lambda b, h, dec, p: (b, 0, blk0 + p * RET_HEADS + h), p=part))

    table = pl.BlockSpec((t_l, LANE), lambda b, h, dec: (0, 0))
    grid_spec = pltpu.PrefetchScalarGridSpec(
        num_scalar_prefetch=1,
        grid=(bsz, RET_HEADS),
        in_specs=[col(t_l, p) for p in range(4)] + [col(t_c, p) for p in range(4)] + [table, table],
        out_specs=[pl.BlockSpec((1, t_l, d), lambda b, h, dec: (b, 0, h)),
                   pl.BlockSpec((1, t_c, d), lambda b, h, dec: (b, 0, h))],
        scratch_shapes=[pltpu.VMEM((t_l, d), F32), pltpu.VMEM((t_c, d), F32),
                        pltpu.VMEM((t_l, d), F32), pltpu.VMEM((t_c, d), F32)])
    return pl.pallas_call(
        _ret_kernel,
        grid_spec=grid_spec,
        out_shape=[jax.ShapeDtypeStruct((bsz, t_l, BR_W), F32), jax.ShapeDtypeStruct((bsz, t_c, BR_W), F32)],
        compiler_params=_cparams(("parallel", "parallel")),
        name="retention_branch",
    )(ret_decay, proj_l, proj_l, proj_l, proj_l, proj_c, proj_c, proj_c, proj_c, cos, sin)


SHORT_PAD = 8


def _mlstm_kernel(ql_ref, kl_ref, vl_ref, ol_ref, gl_ref, qc_ref, kc_ref, vc_ref, oc_ref, gc_ref,
                  wq_ref, wk_ref, bq_ref, bk_ref, gb_ref, yl_ref, yc_ref,
                  pad_ref, qsl_ref, ksl_ref, qsc_ref, ksc_ref, hl_ref, hc_ref, hlb_ref, hcb_ref):
    h = pl.program_id(1)
    t_l, t_c = ql_ref.shape[1], qc_ref.shape[1]
    d = ql_ref.shape[2]
    row = lax.broadcasted_iota(jnp.int32, (CHUNK, CHUNK), 0)
    col = lax.broadcasted_iota(jnp.int32, (CHUNK, CHUNK), 1)
    eye = row == col
    scale = d ** -0.5
    nt = (((1,), (1,)), ((), ()))
    tn = (((0,), (0,)), ((), ()))

    def short_conv(x_ref, w_ref, b_ref, out_ref, t):
        k_taps = w_ref.shape[0]
        first = SHORT_PAD - (k_taps - 1) // 2
        pad_ref[0:SHORT_PAD, :] = jnp.zeros((SHORT_PAD, d), F32)
        pad_ref[SHORT_PAD + t:SHORT_PAD + t + SHORT_PAD, :] = jnp.zeros((SHORT_PAD, d), F32)

        def copy(i, carry):
            pad_ref[pl.ds(pl.multiple_of(SHORT_PAD + i * CHUNK, 8), CHUNK), :] = \
                x_ref[0, pl.ds(pl.multiple_of(i * CHUNK, CHUNK), CHUNK), :]
            return carry

        lax.fori_loop(0, t // CHUNK, copy, 0)

        def conv(i, carry):
            base = pl.multiple_of(i * CHUNK, CHUNK)
            win = pad_ref[pl.ds(base, CHUNK + 2 * SHORT_PAD), :]
            acc = jnp.broadcast_to(b_ref[...], (CHUNK, d))
            for k in range(k_taps):
                acc = acc + w_ref[k:k + 1, :] * win[first + k:first + k + CHUNK]
            out_ref[pl.ds(base, CHUNK), :] = acc * jax.nn.sigmoid(acc)
            return carry

        lax.fori_loop(0, t // CHUNK, conv, 0)

    short_conv(ql_ref, wq_ref, bq_ref, qsl_ref, t_l)
    short_conv(kl_ref, wk_ref, bk_ref, ksl_ref, t_l)
    short_conv(qc_ref, wq_ref, bq_ref, qsc_ref, t_c)
    short_conv(kc_ref, wk_ref, bk_ref, ksc_ref, t_c)

    def gate_cols(g_ref, rows, i_lane, f_lane):
        x = g_ref[0, rows, :] + gb_ref[...]
        i_col = jnp.sum(jnp.where(col == i_lane, x, 0.0), axis=1, keepdims=True)
        f_col = jax.nn.log_sigmoid(jnp.sum(jnp.where(col == f_lane, x, 0.0), axis=1, keepdims=True))
        i_row = jnp.sum(jnp.where(eye, i_col, 0.0), axis=0, keepdims=True)
        f_row = jnp.sum(jnp.where(eye, f_col, 0.0), axis=0, keepdims=True)
        return i_col, f_col, i_row, f_row

    def chunk_step(seq, c, state, forward):
        qs_ref, ks_ref, v_ref, g_ref = seq[:4]
        h_ref = seq[4] if forward else seq[5]
        cmat, nvec, m = state
        rows = pl.ds(pl.multiple_of(c * CHUNK, CHUNK), CHUNK)
        q = qs_ref[rows, :]
        k = ks_ref[rows, :] * scale
        qb, vb = q.astype(BF16), v_ref[0, rows, :].astype(BF16)
        i_col, f_col, i_row, f_row = gate_cols(g_ref, rows, (0 if forward else 2 * MLSTM_HEADS) + h,
                                               (MLSTM_HEADS if forward else 3 * MLSTM_HEADS) + h)
        seen = (col <= row) if forward else (col >= row)
        seen_t = (row <= col) if forward else (row >= col)
        b_col = jnp.sum(jnp.where(seen, f_row, 0.0), axis=1, keepdims=True)
        b_row = jnp.sum(jnp.where(seen_t, f_col, 0.0), axis=0, keepdims=True)
        b_end = jnp.sum(f_row, axis=1, keepdims=True)
        logw = jnp.where(seen, b_col - b_row + i_row, -jnp.inf)
        inter = b_col + m
        m_t = jnp.maximum(inter, jnp.max(logw, axis=1, keepdims=True))
        s = lax.dot_general(qb, k.astype(BF16), nt, preferred_element_type=F32) * jnp.exp(logw - m_t)
        w_prev = jnp.exp(inter - m_t)
        num = (jnp.dot(s.astype(BF16), vb, preferred_element_type=F32)
               + w_prev * jnp.dot(qb, cmat.astype(BF16), preferred_element_type=F32))
        den = jnp.sum(s, axis=1, keepdims=True) + w_prev * jnp.sum(q * nvec, axis=1, keepdims=True)
        hc = num * (1.0 / jnp.maximum(jnp.abs(den), jnp.exp(-m_t)))
        h_ref[rows, :] = hc
        log_wk = b_end - b_col + i_col
        m_new = jnp.maximum(b_end + m, jnp.max(log_wk, axis=0, keepdims=True))
        decay = jnp.exp(b_end + m - m_new)
        kw = k * jnp.exp(log_wk - m_new)
        cmat = decay * cmat + lax.dot_general(kw.astype(BF16), vb, tn, preferred_element_type=F32)
        nvec = decay * nvec + jnp.sum(kw, axis=0, keepdims=True)
        return cmat, nvec, m_new

    lat = (qsl_ref, ksl_ref, vl_ref, gl_ref, hl_ref, hlb_ref)
    ctx = (qsc_ref, ksc_ref, vc_ref, gc_ref, hc_ref, hcb_ref)
    n_l, n_c = t_l // CHUNK, t_c // CHUNK
    state0 = (jnp.zeros((d, d), F32), jnp.zeros((1, d), F32), jnp.full((1, 1), NEG_INIT, F32))

    def both(seq, n):
        def body(i, carry):
            return chunk_step(seq, i, carry[0], True), chunk_step(seq, n - 1 - i, carry[1], False)
        return body

    st = lax.fori_loop(0, n_c, both(ctx, n_c), (state0, state0))
    lax.fori_loop(0, n_l, both(lat, n_l), st)

    def finish(hf_ref, hb_ref, o_ref, y_ref):
        def body(c, carry):
            rows = pl.ds(pl.multiple_of(c * FINISH_ROWS, FINISH_ROWS), FINISH_ROWS)
            y_ref[0, rows, :] = (_head_norm_rows(hf_ref[rows, :] + hb_ref[rows, :])
                                 * jax.nn.sigmoid(o_ref[0, rows, :]))
            return carry
        lax.fori_loop(0, hf_ref.shape[0] // FINISH_ROWS, body, 0)

    finish(hl_ref, hlb_ref, ol_ref, yl_ref)
    finish(hc_ref, hcb_ref, oc_ref, yc_ref)


def mlstm_branch(proj_l, proj_c, conv_w, conv_b, gate_b):
    bsz, t_l, _ = proj_l.shape
    t_c = proj_c.shape[1]
    d = BR_W // MLSTM_HEADS
    blk0 = OFF_MLSTM // d
    k_taps = conv_w.shape[0]

    def col(t, part):
        return pl.BlockSpec((1, t, d), functools.partial(lambda b, h, p: (b, 0, blk0 + p * MLSTM_HEADS + h), p=part))

    def gates(t):
        return pl.BlockSpec((1, t, LANE), lambda b, h: (b, 0, OFF_MGATE // LANE))

    gb = jnp.zeros((1, LANE), F32).at[0, :MLSTM_GATES].set(gate_b.reshape(-1))
    return pl.pallas_call(
        _mlstm_kernel,
        grid=(bsz, MLSTM_HEADS),
        in_specs=[col(t_l, p) for p in range(4)] + [gates(t_l)] + [col(t_c, p) for p in range(4)] + [gates(t_c)] + [
            pl.BlockSpec((k_taps, d), lambda b, h: (0, h)),
            pl.BlockSpec((k_taps, d), lambda b, h: (0, MLSTM_HEADS + h)),
            pl.BlockSpec((1, d), lambda b, h: (0, h)),
            pl.BlockSpec((1, d), lambda b, h: (0, MLSTM_HEADS + h)),
            pl.BlockSpec((1, LANE), lambda b, h: (0, 0))],
        out_specs=[pl.BlockSpec((1, t_l, d), lambda b, h: (b, 0, h)),
                   pl.BlockSpec((1, t_c, d), lambda b, h: (b, 0, h))],
        out_shape=[jax.ShapeDtypeStruct((bsz, t_l, BR_W), F32), jax.ShapeDtypeStruct((bsz, t_c, BR_W), F32)],
        scratch_shapes=[pltpu.VMEM((t_l + 2 * SHORT_PAD, d), F32),
                        pltpu.VMEM((t_l, d), F32), pltpu.VMEM((t_l, d), F32),
                        pltpu.VMEM((t_c, d), F32), pltpu.VMEM((t_c, d), F32),
                        pltpu.VMEM((t_l, d), F32), pltpu.VMEM((t_c, d), F32),
                        pltpu.VMEM((t_l, d), F32), pltpu.VMEM((t_c, d), F32)],
        compiler_params=_cparams(("parallel", "parallel")),
        name="mlstm_branch",
    )(proj_l, proj_l, proj_l, proj_l, proj_l, proj_c, proj_c, proj_c, proj_c, proj_c,
      conv_w, conv_w, conv_b.reshape(1, -1), conv_b.reshape(1, -1), gb)


def _attn_kernel(sink_ref, q_ref, *refs, local):
    if local:
        kp_ref, kc_ref, kn_ref, vp_ref, vc_ref, vn_ref, kx_ref, vx_ref, cos_ref, sin_ref, o_ref, k_scr, v_scr = refs
    else:
        kx_ref, vx_ref, o_ref, k_scr, v_scr = refs
    i = pl.program_id(1)
    nq = pl.num_programs(1)
    t_x = kx_ref.shape[1]
    n_loc = 3 * BLOCK if local else 0
    n_keys = n_loc + t_x
    nf = ATTN_DIM // 4
    half = LANE // 2

    def put(dst, lo, x):
        dst[0, lo:lo + x.shape[0], :] = x.astype(BF16)
        dst[1, lo:lo + x.shape[0], :] = pltpu.roll(x, half, 1).astype(BF16)

    if local:
        for j, (k_ref, v_ref) in enumerate(((kp_ref, vp_ref), (kc_ref, vc_ref), (kn_ref, vn_ref))):
            blk = jnp.clip(i + (j - 1), 0, nq - 1)
            rows = pl.ds(pl.multiple_of(blk * BLOCK, BLOCK), BLOCK)
            put(k_scr, j * BLOCK, _rope(k_ref[0], cos_ref[rows, :], sin_ref[rows, :], nf))
            put(v_scr, j * BLOCK, v_ref[0])
    put(k_scr, n_loc, kx_ref[0])
    put(v_scr, n_loc, vx_ref[0])

    lane = lax.broadcasted_iota(jnp.int32, (BLOCK, LANE), 1)
    if local:
        rel = (lax.broadcasted_iota(jnp.int32, (BLOCK, n_keys), 1)
               - lax.broadcasted_iota(jnp.int32, (BLOCK, n_keys), 0))
        kpos = (i - 1) * BLOCK + lax.broadcasted_iota(jnp.int32, (BLOCK, n_keys), 1)
        ok = (rel >= 0) & (rel <= 2 * WINDOW) & (kpos >= 0) & (kpos < nq * BLOCK)
        ok = ok | (lax.broadcasted_iota(jnp.int32, (BLOCK, n_keys), 1) >= n_loc)
        bias = jnp.where(ok, 0.0, -jnp.inf)
        qrows = pl.ds(pl.multiple_of(i * BLOCK, BLOCK), BLOCK)
        cos_q, sin_q = cos_ref[qrows, :], sin_ref[qrows, :]
    nt = (((1,), (1,)), ((), ()))
    for p in range(ATTN_HEADS // 2):
        qp = q_ref[0, :, p * LANE:(p + 1) * LANE]
        if local:
            qp = _rope(qp, cos_q, sin_q, nf)
        qp = qp * (ATTN_DIM ** -0.5)
        outs = []
        for sub in range(2):
            hd = 2 * p + sub
            kv_half = hd // (ATTN_HEADS // ATTN_KV_HEADS)
            sel = (lane >= half) if sub else (lane < half)
            qm = jnp.where(sel, qp, 0.0).astype(BF16)
            swap = 0 if kv_half == sub else 1
            s = lax.dot_general(qm, k_scr[swap], nt, preferred_element_type=F32)
            if local:
                s = s + bias
            sink = jnp.full((BLOCK, 1), sink_ref[hd], F32)
            m = jnp.maximum(jnp.max(s, axis=1, keepdims=True), sink)
            e = jnp.exp(s - m)
            inv = 1.0 / (jnp.sum(e, axis=1, keepdims=True) + jnp.exp(sink - m))
            outs.append(jnp.dot((e * inv).astype(BF16), v_scr[swap], preferred_element_type=F32))
        o_ref[0, :, p * LANE:(p + 1) * LANE] = jnp.where(lane < half, outs[0], outs[1])


def attention_branch(proj_q, proj_c, sink, cos, sin, local):
    bsz, t, _ = proj_q.shape
    t_c = proj_c.shape[1]
    nq = t // BLOCK
    n_qcol = ATTN_HEADS * ATTN_DIM
    q_blk = OFF_ATTN // n_qcol
    k_blk = (OFF_ATTN + n_qcol) // LANE
    v_blk = k_blk + 1
    n_keys = (3 * BLOCK if local else 0) + t_c

    def kv(blk, shift):
        return pl.BlockSpec((1, BLOCK, LANE), lambda b, i, s: (b, jnp.clip(i + shift, 0, nq - 1), blk))

    in_specs = [pl.BlockSpec((1, BLOCK, n_qcol), lambda b, i, s: (b, i, q_blk))]
    args = [proj_q]
    if local:
        in_specs += [kv(k_blk, -1), kv(k_blk, 0), kv(k_blk, 1), kv(v_blk, -1), kv(v_blk, 0), kv(v_blk, 1)]
        args += [proj_q] * 6
    in_specs += [pl.BlockSpec((1, t_c, LANE), lambda b, i, s: (b, 0, k_blk)),
                 pl.BlockSpec((1, t_c, LANE), lambda b, i, s: (b, 0, v_blk))]
    args += [proj_c, proj_c]
    if local:
        in_specs += [pl.BlockSpec((t, LANE), lambda b, i, s: (0, 0))] * 2
        args += [cos, sin]
    grid_spec = pltpu.PrefetchScalarGridSpec(
        num_scalar_prefetch=1,
        grid=(bsz, nq),
        in_specs=in_specs,
        out_specs=pl.BlockSpec((1, BLOCK, n_qcol), lambda b, i, s: (b, i, 0)),
        scratch_shapes=[pltpu.VMEM((2, n_keys, LANE), BF16), pltpu.VMEM((2, n_keys, LANE), BF16)])
    return pl.pallas_call(
        functools.partial(_attn_kernel, local=local),
        grid_spec=grid_spec,
        out_shape=jax.ShapeDtypeStruct((bsz, t, n_qcol), F32),
        compiler_params=_cparams(("parallel", "parallel")),
        name="attention_branch",
    )(sink, *args)


def _pack_w_in(w):
    d = w.shape[0]
    used = w.shape[1] - MLSTM_GATES + LANE
    cols = [w[:, :REF_MGATE_OFF], w[:, REF_GATE_OFF:], w[:, REF_ATTN_OFF:REF_GATE_OFF],
            w[:, REF_MGATE_OFF:REF_ATTN_OFF],
            jnp.zeros((d, _round_up(used, PROJ_TILE_N) - w.shape[1]), w.dtype)]
    return jnp.concatenate(cols, axis=1).astype(BF16)


def _final_norm(x, g):
    y = x * lax.rsqrt(jnp.mean(x * x, axis=-1, keepdims=True) + EPS)
    return y * g


def kernel(x, c, ctx, c_ctx, mod_w, mod_b, norm1_g, norm2_g, w_in, conv_w, conv_b, conv_ln_g, conv_ln_b,
           ret_decay, mlstm_conv_w, mlstm_conv_b, mlstm_gate_b, attn_sink, w_branch, w_out,
           peer_wq, peer_keys, peer_u, peer_v, final_g):
    bsz, seq, d = x.shape
    assert d == D_MODEL
    depth = mod_w.shape[0]
    cos_r, sin_r = rope_tables(seq, BR_W // RET_HEADS)
    cos_a, sin_a = rope_tables(seq, ATTN_DIM)
    n_cond = _round_up(bsz + 1, 8)
    cond = jnp.concatenate([c, c_ctx[None], jnp.zeros((n_cond - bsz - 1, d), F32)], axis=0)
    for l in range(depth):
        last = l == depth - 1
        mod = mod_vectors(cond, mod_w[l], mod_b[l])
        sh1, sc1, ga1, sh2, sc2, ga2 = (m[:, None, :] for m in jnp.split(mod[:bsz], 6, axis=-1))
        csh1, csc1, cga1, csh2, csc2, cga2 = (jnp.broadcast_to(m[None], (bsz, 1, d))
                                              for m in jnp.split(mod[bsz:bsz + 1], 6, axis=-1))
        w_in_p = _pack_w_in(w_in[l])
        proj_l = norm_matmul(x, norm1_g[l], sh1, sc1, w_in_p)
        proj_c = norm_matmul(ctx, norm1_g[l], csh1, csc1, w_in_p)
        ret_l, ret_c = retention_branch(proj_l, proj_c, ret_decay[l], cos_r, sin_r)
        mls_l, mls_c = mlstm_branch(proj_l, proj_c, mlstm_conv_w[l], mlstm_conv_b[l], mlstm_gate_b[l])
        conv_args = (conv_w[l], conv_b[l], conv_ln_g[l], conv_ln_b[l])
        ys_l = (conv_branch(proj_l, *conv_args), ret_l, mls_l,
                attention_branch(proj_l, proj_c, attn_sink[l], cos_a, sin_a, True))
        if not last:
            ys_c = (conv_branch(proj_c, *conv_args), ret_c, mls_c,
                    attention_branch(proj_c, proj_c, attn_sink[l], cos_a, sin_a, False))
        wb, wo = w_branch[l].astype(BF16), w_out[l].astype(BF16)
        wqt = peer_wq[l].T.astype(BF16)
        u = peer_u[l].astype(BF16)
        vt = peer_v[l].T.astype(BF16)

        def peer(xx, sh, sc, ga):
            xm, *tabs = peer_prep(xx, norm2_g[l], sh, sc, wqt, peer_keys[l])
            return peer_dense(xm, *tabs, u, vt, xx, ga)

        x = merge(ys_l, proj_l, wb, wo, x, ga1)
        x = peer(x, sh2, sc2, ga2)
        if not last:
            ctx = merge(ys_c, proj_c, wb, wo, ctx, cga1)
            ctx = peer(ctx, csh2, csc2, cga2)
    return _final_norm(x, final_g)
```

```python
import functools

import jax
import jax.numpy as jnp
import numpy as np
from jax import lax
from jax.experimental import pallas as pl
from jax.experimental.pallas import tpu as pltpu

F32 = jnp.float32
BF16 = jnp.bfloat16

GRID_W = 64
EPS = 1e-6
N_BRANCH = 4
BR_W = 512
RET_HEADS = 4
MLSTM_HEADS = 4
MLSTM_GATES = 4 * MLSTM_HEADS
ATTN_HEADS = 8
ATTN_KV_HEADS = 2
ATTN_DIM = BR_W // ATTN_HEADS
WINDOW = 128
BLOCK = 128
CHUNK = 128
ROPE_BASE = 10000.0
NEG_INIT = -1e30
PEER_HEADS = 8
PEER_KEYS = 128
PEER_TOPK = 16

CONV_IN = 2 * BR_W
RET_IN = 4 * BR_W
MLSTM_QKVO = 4 * BR_W
ATTN_IN = ATTN_HEADS * ATTN_DIM + 2 * ATTN_KV_HEADS * ATTN_DIM
REF_MGATE_OFF = CONV_IN + RET_IN + MLSTM_QKVO
REF_ATTN_OFF = REF_MGATE_OFF + MLSTM_GATES
REF_GATE_OFF = REF_ATTN_OFF + ATTN_IN
OFF_CONV = 0
OFF_RET = OFF_CONV + CONV_IN
OFF_MLSTM = OFF_RET + RET_IN
OFF_GATE = OFF_MLSTM + MLSTM_QKVO
D_MODEL = 1024
OFF_ATTN = OFF_GATE + N_BRANCH * D_MODEL
OFF_MGATE = OFF_ATTN + ATTN_IN
LANE = 128
BF16_ROWS = 16
PROJ_TILE_N = 1024


def _round_up(a, m):
    return (a + m - 1) // m * m


VMEM_LIMIT = 56 * 1024 * 1024


def _cparams(sem):
    return pltpu.CompilerParams(dimension_semantics=sem, vmem_limit_bytes=VMEM_LIMIT)


def _mod_kernel(s_ref, w_ref, b_ref, o_ref):
    s = s_ref[...]
    s = s * jax.nn.sigmoid(s)
    o_ref[...] = jnp.dot(s, w_ref[...], preferred_element_type=F32,
                         precision=lax.Precision.HIGHEST) + b_ref[...]


def mod_vectors(cond, w, b):
    r, d = cond.shape
    n = w.shape[1]
    tn = 512
    return pl.pallas_call(
        _mod_kernel,
        grid=(n // tn,),
        in_specs=[pl.BlockSpec((r, d), lambda j: (0, 0)),
                  pl.BlockSpec((d, tn), lambda j: (0, j)),
                  pl.BlockSpec((1, tn), lambda j: (0, j))],
        out_specs=pl.BlockSpec((r, tn), lambda j: (0, j)),
        out_shape=jax.ShapeDtypeStruct((r, n), F32),
        compiler_params=_cparams(("parallel",)),
        name="mod_vectors",
    )(cond, w, b.reshape(1, n))


def _norm_matmul_kernel(x_ref, g_ref, sh_ref, sc_ref, w_ref, o_ref, *rest, emit_h):
    h_ref = rest[-1]

    @pl.when(pl.program_id(2) == 0)
    def _():
        x = x_ref[0]
        y = x * lax.rsqrt(jnp.mean(x * x, axis=-1, keepdims=True) + EPS)
        h = (y * g_ref[...]) * (1.0 + sc_ref[0]) + sh_ref[0]
        h_ref[...] = h.astype(BF16)
        if emit_h:
            rest[0][0] = h.astype(BF16)

    o_ref[0] = jnp.dot(h_ref[...], w_ref[...], preferred_element_type=F32)


def norm_matmul(x, g, shift, scale, w, emit_h=False):
    bsz, t, d = x.shape
    n = w.shape[1]
    tm = min(t, 1024)
    tn = min(n, PROJ_TILE_N)
    out_shape = [jax.ShapeDtypeStruct((bsz, t, n), F32)]
    out_specs = [pl.BlockSpec((1, tm, tn), lambda b, i, j: (b, i, j))]
    if emit_h:
        out_shape.append(jax.ShapeDtypeStruct((bsz, t, d), BF16))
        out_specs.append(pl.BlockSpec((1, tm, d), lambda b, i, j: (b, i, 0)))
    res = pl.pallas_call(
        functools.partial(_norm_matmul_kernel, emit_h=emit_h),
        grid=(bsz, t // tm, n // tn),
        in_specs=[pl.BlockSpec((1, tm, d), lambda b, i, j: (b, i, 0)),
                  pl.BlockSpec((1, d), lambda b, i, j: (0, 0)),
                  pl.BlockSpec((1, 1, d), lambda b, i, j: (b, 0, 0)),
                  pl.BlockSpec((1, 1, d), lambda b, i, j: (b, 0, 0)),
                  pl.BlockSpec((d, tn), lambda b, i, j: (0, j))],
        out_specs=out_specs,
        out_shape=out_shape,
        scratch_shapes=[pltpu.VMEM((tm, d), BF16)],
        compiler_params=_cparams(("parallel", "parallel", "arbitrary")),
        name="norm_matmul",
    )(x, g.reshape(1, d), shift, scale, w)
    return res if emit_h else res[0]


def _merge_kernel(y0, y1, y2, y3, g0, g1, g2, g3, wb_ref, wo_ref, x_ref, ga_ref, o_ref):
    acc = None
    for i, (y, g) in enumerate(((y0, g0), (y1, g1), (y2, g2), (y3, g3))):
        p = jnp.dot(y[0].astype(BF16), wb_ref[i], preferred_element_type=F32)
        t = jax.nn.sigmoid(g[0]) * p
        acc = t if acc is None else acc + t
    o = jnp.dot(acc.astype(BF16), wo_ref[...], preferred_element_type=F32)
    o_ref[0] = x_ref[0] + ga_ref[0] * o


def merge(ys, proj, w_branch, w_out, x, ga):
    bsz, t, d = x.shape
    tm = min(t, 512)
    gate_blk = OFF_GATE // d
    y_spec = pl.BlockSpec((1, tm, BR_W), lambda b, i: (b, i, 0))
    g_specs = [pl.BlockSpec((1, tm, d), functools.partial(lambda b, i, k: (b, i, gate_blk + k), k=k))
               for k in range(N_BRANCH)]
    return pl.pallas_call(
        _merge_kernel,
        grid=(bsz, t // tm),
        in_specs=[y_spec] * N_BRANCH + g_specs + [
            pl.BlockSpec((N_BRANCH, BR_W, d), lambda b, i: (0, 0, 0)),
            pl.BlockSpec((d, d), lambda b, i: (0, 0)),
            pl.BlockSpec((1, tm, d), lambda b, i: (b, i, 0)),
            pl.BlockSpec((1, 1, d), lambda b, i: (b, 0, 0))],
        out_specs=pl.BlockSpec((1, tm, d), lambda b, i: (b, i, 0)),
        out_shape=jax.ShapeDtypeStruct((bsz, t, d), F32),
        compiler_params=_cparams(("parallel", "parallel")),
        name="merge",
    )(*ys, proj, proj, proj, proj, w_branch, w_out, x, ga)


PEER_TE = 2048
PEER_SUB = 512
PEER_SECOND = 1
PEER_GROUP = 4


def _gelu_exact(x):
    return 0.5 * x * (1.0 + lax.erf(x * float(1.0 / np.sqrt(2.0))))


def _peer_kernel(xm_ref, a_ref, lb_ref, b_ref, r2_ref, u_ref, vt_ref, x_ref, ga_ref, o_ref,
                 acc_ref, w_ref, h_ref):
    j = pl.program_id(2)

    @pl.when(j == 0)
    def _():
        acc_ref[...] = jnp.zeros_like(acc_ref)

    tn = xm_ref.shape[1]

    def spread(row):
        return jnp.broadcast_to(row, (BF16_ROWS, LANE)).astype(BF16)[None]

    def slabs(words):
        return pltpu.bitcast(words, BF16).reshape(PEER_KEYS // BF16_ROWS, BF16_ROWS, LANE)

    n_sub = PEER_TE // PEER_SUB
    n_slot = tn // LANE
    rows_per_sub = PEER_SUB // PEER_KEYS

    def first_matmul(sub):
        h_ref[sub % 2] = lax.dot_general(u_ref[pl.ds(sub * PEER_SUB, PEER_SUB), :], xm_ref[0],
                                         (((1,), (1,)), ((), ())), preferred_element_type=F32)

    def second_matmul(sub):
        experts = pl.ds(sub * PEER_SUB, PEER_SUB)
        acc_ref[...] += jnp.dot(vt_ref[:, experts], w_ref[experts, :], preferred_element_type=F32)

    def gate_slice(sub, blk):
        lanes = pl.ds(blk * LANE, LANE)
        lo = blk * LANE
        for grp in range(rows_per_sub // PEER_GROUP):
            row0 = sub * rows_per_sub + grp * PEER_GROUP
            first_rows = [pl.ds(j * (PEER_TE // PEER_KEYS) + row0 + a, 1) for a in range(PEER_GROUP)]
            g = [None] * PEER_GROUP
            for h in range(PEER_HEADS):
                r2 = slabs(r2_ref[0, h, :, lanes])
                b = slabs(b_ref[0, h, :, lanes])
                for a in range(PEER_GROUP):
                    limit = spread(lb_ref[0, h, first_rows[a], :][:, lo:lo + LANE])
                    t = jnp.where(r2 < limit, b, 0.0) * spread(a_ref[0, h, first_rows[a], :][:, lo:lo + LANE])
                    g[a] = t if g[a] is None else g[a] + t
            for a in range(PEER_GROUP):
                local = pl.ds((grp * PEER_GROUP + a) * PEER_KEYS, PEER_KEYS)
                act = _gelu_exact(h_ref[sub % 2, local, lanes]).astype(BF16)
                w_ref[pl.ds((row0 + a) * PEER_KEYS, PEER_KEYS), lanes] = g[a].reshape(PEER_KEYS, LANE) * act

    first_matmul(0)
    for sub in range(n_sub):
        if sub + 1 < n_sub:
            first_matmul(sub + 1)
        if sub >= 1:
            second_matmul(sub - 1)
        for k in range(n_slot):
            gate_slice(sub, k)
    second_matmul(n_sub - 1)

    @pl.when(j == pl.num_programs(2) - 1)
    def _():
        o_ref[0] = x_ref[0] + ga_ref[0] * acc_ref[...].T


def peer_dense(xm, a_t, lb_t, b_t, r2_t, u, vt, x, ga):
    bsz, t, d = x.shape
    n_exp = u.shape[0]
    tn = min(t, 512)
    tab = pl.BlockSpec((1, PEER_HEADS, PEER_KEYS, tn), lambda b, i, j: (b, 0, 0, i))
    pair = pl.BlockSpec((1, PEER_HEADS, PEER_KEYS // 2, tn), lambda b, i, j: (b, 0, 0, i))
    return pl.pallas_call(
        _peer_kernel,
        grid=(bsz, t // tn, n_exp // PEER_TE),
        in_specs=[pl.BlockSpec((1, tn, d), lambda b, i, j: (b, i, 0)),
                  tab, tab, pair, pair,
                  pl.BlockSpec((PEER_TE, d), lambda b, i, j: (j, 0)),
                  pl.BlockSpec((d, PEER_TE), lambda b, i, j: (0, j)),
                  pl.BlockSpec((1, tn, d), lambda b, i, j: (b, i, 0)),
                  pl.BlockSpec((1, 1, d), lambda b, i, j: (b, 0, 0))],
        out_specs=pl.BlockSpec((1, tn, d), lambda b, i, j: (b, i, 0)),
        out_shape=jax.ShapeDtypeStruct((bsz, t, d), F32),
        scratch_shapes=[pltpu.VMEM((d, tn), F32), pltpu.VMEM((PEER_TE, tn), BF16),
                        pltpu.VMEM((2, PEER_SUB, tn), F32)],
        compiler_params=_cparams(("parallel", "parallel", "arbitrary")),
        name="peer_dense",
    )(xm, a_t, lb_t, b_t, r2_t, u, vt, x, ga)


def _extract_top(s, row, n_pick, lowest_row_only):
    rank = jnp.full(s.shape, float(n_pick), F32)
    tops = []
    for k in range(n_pick):
        m = jnp.max(s, axis=0, keepdims=True)
        hit = s == m
        if lowest_row_only:
            hit = row == jnp.min(jnp.where(hit, row, s.shape[0]), axis=0, keepdims=True)
        rank = jnp.where(hit, float(k), rank)
        s = jnp.where(hit, -jnp.inf, s)
        tops.append(m)
    n_marked = jnp.sum(jnp.where(rank < float(n_pick), 1.0, 0.0), axis=0, keepdims=True)
    n_bad = jnp.sum(jnp.where(n_marked != float(n_pick), 1.0, 0.0))
    return rank, jnp.concatenate(tops, axis=0), n_bad


def _peer_prep_kernel(x_ref, g_ref, sh_ref, sc_ref, wqt_ref, keys_ref,
                      xm_ref, a_ref, lb_ref, b_ref, r2_ref, qt_ref):
    tn = x_ref.shape[1]
    x = x_ref[0]
    y = x * lax.rsqrt(jnp.mean(x * x, axis=-1, keepdims=True) + EPS)
    xm = ((y * g_ref[...]) * (1.0 + sc_ref[0]) + sh_ref[0]).astype(BF16)
    xm_ref[0] = xm
    qt_ref[...] = lax.dot_general(wqt_ref[...], xm, (((1,), (1,)), ((), ())), preferred_element_type=F32)
    half = keys_ref.shape[2]
    row = lax.broadcasted_iota(jnp.int32, (PEER_KEYS, LANE), 0)
    sub = lax.broadcasted_iota(jnp.int32, (8, LANE), 0)
    n_cand_rows = PEER_TOPK + 8 * (PEER_TOPK // 2 - 1) + PEER_TOPK // 2
    crow = lax.broadcasted_iota(jnp.int32, (n_cand_rows, LANE), 0)

    def head_tables(h, exact):
        n_bad = 0.0
        for blk in range(tn // LANE):
            lanes = pl.ds(blk * LANE, LANE)
            s, rank, tops = [], [], []
            for c in range(2):
                qs = qt_ref[pl.ds(pl.multiple_of((2 * h + c) * half, half), half), lanes]
                sc_ = jnp.dot(keys_ref[c], qs, preferred_element_type=F32, precision=lax.Precision.HIGHEST)
                r, t, bad = _extract_top(sc_, row, PEER_TOPK, exact)
                n_bad = n_bad + bad
                s.append(sc_)
                rank.append(r)
                tops.append(t)
            t1, t2 = tops
            groups = [t1[0:1] + t2]
            for a in range(1, PEER_TOPK // 2):
                n_valid = PEER_TOPK // (a + 1)
                grp = t1[a:a + 1] + t2[0:8]
                groups.append(grp if n_valid >= 8 else jnp.where(sub < n_valid, grp, -jnp.inf))
            groups.append(t1[PEER_TOPK // 2:] + t2[0:1])
            cand = jnp.concatenate(groups, axis=0)
            crank, _, bad = _extract_top(cand, crow, PEER_TOPK, exact)
            n_bad = n_bad + bad
            picked = crank < float(PEER_TOPK)
            z = jnp.sum(jnp.where(picked, jnp.exp(cand - cand[0:1]), 0.0), axis=0, keepdims=True)
            pk = picked.astype(F32)
            counts = [jnp.sum(pk[0:PEER_TOPK], axis=0, keepdims=True)]
            for a in range(1, PEER_TOPK // 2):
                lo = PEER_TOPK + 8 * (a - 1)
                counts.append(jnp.sum(pk[lo:lo + 8], axis=0, keepdims=True))
            lo = PEER_TOPK + 8 * (PEER_TOPK // 2 - 1)
            for a in range(PEER_TOPK // 2):
                counts.append(pk[lo + a:lo + a + 1])
            lb = jnp.zeros((PEER_KEYS, LANE), F32)
            for a in range(PEER_TOPK):
                lb = jnp.where(rank[0] == float(a), counts[a], lb)
            a_ref[0, h, :, lanes] = jnp.exp(s[0] - t1[0:1])
            lb_ref[0, h, :, lanes] = lb
            b_ref[0, h, :, lanes] = pltpu.bitcast((jnp.exp(s[1] - t2[0:1]) / z).astype(BF16), jnp.uint32)
            r2_ref[0, h, :, lanes] = pltpu.bitcast(rank[1].astype(BF16), jnp.uint32)
        return n_bad

    def head_body(h, carry):
        n_bad = head_tables(h, False)

        @pl.when(n_bad > 0.0)
        def _():
            head_tables(h, True)

        return carry

    lax.fori_loop(0, PEER_HEADS, head_body, 0)


def peer_prep(x, g, shift, scale, wqt, keys):
    bsz, t, d = x.shape
    tn = min(t, 256)
    nq = wqt.shape[0]
    tab_shape = jax.ShapeDtypeStruct((bsz, PEER_HEADS, PEER_KEYS, t), F32)
    tab_spec = pl.BlockSpec((1, PEER_HEADS, PEER_KEYS, tn), lambda b, i: (b, 0, 0, i))
    pair_shape = jax.ShapeDtypeStruct((bsz, PEER_HEADS, PEER_KEYS // 2, t), jnp.uint32)
    pair_spec = pl.BlockSpec((1, PEER_HEADS, PEER_KEYS // 2, tn), lambda b, i: (b, 0, 0, i))
    return pl.pallas_call(
        _peer_prep_kernel,
        grid=(bsz, t // tn),
        in_specs=[pl.BlockSpec((1, tn, d), lambda b, i: (b, i, 0)),
                  pl.BlockSpec((1, d), lambda b, i: (0, 0)),
                  pl.BlockSpec((1, 1, d), lambda b, i: (b, 0, 0)),
                  pl.BlockSpec((1, 1, d), lambda b, i: (b, 0, 0)),
                  pl.BlockSpec((nq, d), lambda b, i: (0, 0)),
                  pl.BlockSpec(keys.shape, lambda b, i: (0, 0, 0))],
        out_specs=[pl.BlockSpec((1, tn, d), lambda b, i: (b, i, 0)), tab_spec, tab_spec, pair_spec, pair_spec],
        out_shape=[jax.ShapeDtypeStruct((bsz, t, d), BF16), tab_shape, tab_shape, pair_shape, pair_shape],
        scratch_shapes=[pltpu.VMEM((nq, tn), F32)],
        compiler_params=_cparams(("parallel", "parallel")),
        name="peer_prep",
    )(x, g.reshape(1, d), shift, scale, wqt, keys)


ROW_TILE = 128
CONV_PAD = 16


def _conv_kernel(p_ref, w_ref, b_ref, lg_ref, lb_ref, o_ref, pad_ref, y_ref):
    t, c = o_ref.shape[1], o_ref.shape[2]
    k_taps = w_ref.shape[0]
    first = CONV_PAD - (k_taps - 1) // 2
    lane_w = 256
    pad_ref[0:CONV_PAD, :] = jnp.zeros((CONV_PAD, c), F32)
    pad_ref[CONV_PAD + t:, :] = jnp.zeros((CONV_PAD, c), F32)

    def glu_body(i, carry):
        rows = pl.ds(pl.multiple_of(i * ROW_TILE, ROW_TILE), ROW_TILE)
        a = p_ref[0, rows, 0:c]
        g = p_ref[0, rows, c:2 * c]
        pad_ref[pl.ds(pl.multiple_of(CONV_PAD + i * ROW_TILE, 8), ROW_TILE), :] = a * jax.nn.sigmoid(g)
        return carry

    lax.fori_loop(0, t // ROW_TILE, glu_body, 0)

    def conv_body(i, carry):
        base = pl.multiple_of(i * ROW_TILE, ROW_TILE)
        for lb in range(c // lane_w):
            lanes = pl.ds(lb * lane_w, lane_w)
            win = pad_ref[pl.ds(base, ROW_TILE + 2 * CONV_PAD), lanes]
            acc = jnp.broadcast_to(b_ref[:, lanes], (ROW_TILE, lane_w))
            for k in range(k_taps):
                acc = acc + w_ref[k:k + 1, lanes] * win[first + k:first + k + ROW_TILE]
            y_ref[:, lanes] = acc
        y = y_ref[...]
        yc = y - jnp.mean(y, axis=-1, keepdims=True)
        z = yc * lax.rsqrt(jnp.mean(yc * yc, axis=-1, keepdims=True) + EPS) * lg_ref[...] + lb_ref[...]
        o_ref[0, pl.ds(base, ROW_TILE), :] = z * jax.nn.sigmoid(z)
        return carry

    lax.fori_loop(0, t // ROW_TILE, conv_body, 0)


def conv_branch(proj, w, b, ln_g, ln_b):
    bsz, t, _ = proj.shape
    k_taps, c = w.shape
    vec = pl.BlockSpec((1, c), lambda i: (0, 0))
    return pl.pallas_call(
        _conv_kernel,
        grid=(bsz,),
        in_specs=[pl.BlockSpec((1, t, 2 * c), lambda i: (i, 0, OFF_CONV // (2 * c))),
                  pl.BlockSpec((k_taps, c), lambda i: (0, 0)), vec, vec, vec],
        out_specs=pl.BlockSpec((1, t, c), lambda i: (i, 0, 0)),
        out_shape=jax.ShapeDtypeStruct((bsz, t, c), F32),
        scratch_shapes=[pltpu.VMEM((t + 2 * CONV_PAD, c), F32), pltpu.VMEM((ROW_TILE, c), F32)],
        compiler_params=_cparams(("parallel",)),
        name="conv_branch",
    )(proj, w, b.reshape(1, c), ln_g.reshape(1, c), ln_b.reshape(1, c))


def rope_tables(n_tok, head_dim):
    nf = head_dim // 4
    lane = np.arange(LANE)
    d_idx = lane % head_dim
    by_col = d_idx >= head_dim // 2
    freqs = ROPE_BASE ** (-(d_idx % nf).astype(np.float32) / nf)
    tok = np.arange(n_tok)
    pos = np.where(by_col[None, :], (tok % GRID_W)[:, None], (tok // GRID_W)[:, None]).astype(np.float32)
    ang = jnp.asarray(pos) * jnp.asarray(freqs, F32)[None, :]
    sign = np.where(d_idx % (2 * nf) < nf, -1.0, 1.0).astype(np.float32)
    return jnp.cos(ang), jnp.sin(ang) * sign[None, :]


def _rope(x, cos, sin_signed, nf):
    lane = lax.broadcasted_iota(jnp.int32, x.shape, 1)
    partner = jnp.where(lane % (2 * nf) < nf, pltpu.roll(x, LANE - nf, 1), pltpu.roll(x, nf, 1))
    return x * cos + partner * sin_signed


FINISH_ROWS = 256


def _head_norm_rows(o):
    oc = o - jnp.mean(o, axis=-1, keepdims=True)
    return oc * lax.rsqrt(jnp.mean(oc * oc, axis=-1, keepdims=True) + EPS)


def _ret_kernel(dec_ref, ql_ref, kl_ref, vl_ref, gl_ref, qc_ref, kc_ref, vc_ref, gc_ref, cos_ref, sin_ref,
                yl_ref, yc_ref, ol_ref, oc_ref, olb_ref, ocb_ref):
    h = pl.program_id(1)
    n_l, n_c = ql_ref.shape[1] // CHUNK, qc_ref.shape[1] // CHUNK
    d = ql_ref.shape[2]
    row = lax.broadcasted_iota(jnp.int32, (CHUNK, CHUNK), 0).astype(F32)
    col = lax.broadcasted_iota(jnp.int32, (CHUNK, CHUNK), 1).astype(F32)
    lg_f = jax.nn.log_sigmoid(jnp.full((CHUNK, CHUNK), dec_ref[0, h], F32))
    lg_b = jax.nn.log_sigmoid(jnp.full((CHUNK, CHUNK), dec_ref[1, h], F32))
    dmat = (jnp.where(row >= col, jnp.exp((row - col) * lg_f), 0.0)
            + jnp.where(col >= row, jnp.exp((col - row) * lg_b), 0.0))
    qdec_f, kdec_f, cdec_f = jnp.exp((row + 1.0) * lg_f), jnp.exp((CHUNK - 1.0 - row) * lg_f), jnp.exp(CHUNK * lg_f)
    qdec_b, kdec_b, cdec_b = jnp.exp((CHUNK - row) * lg_b), jnp.exp(row * lg_b), jnp.exp(CHUNK * lg_b)
    scale = d ** -0.5
    nt = (((1,), (1,)), ((), ()))
    tn = (((0,), (0,)), ((), ()))

    def load(refs, c, rotate):
        q_ref, k_ref, v_ref = refs
        rows = pl.ds(pl.multiple_of(c * CHUNK, CHUNK), CHUNK)
        q, k, v = q_ref[0, rows, :], k_ref[0, rows, :], v_ref[0, rows, :]
        if rotate:
            cos, sin = cos_ref[rows, :], sin_ref[rows, :]
            q, k = _rope(q, cos, sin, d // 4), _rope(k, cos, sin, d // 4)
        return q.astype(BF16), k * scale, v.astype(BF16), rows

    def both(refs, of_ref, ob_ref, rotate, n):
        def body(i, carry):
            s_f, s_b = carry
            q, k, v, rows = load(refs, i, rotate)
            att = lax.dot_general(q, k.astype(BF16), nt, preferred_element_type=F32) * dmat
            o = jnp.dot(att.astype(BF16), v, preferred_element_type=F32)
            of_ref[rows, :] = o + qdec_f * jnp.dot(q, s_f.astype(BF16), preferred_element_type=F32)
            s_f = cdec_f * s_f + lax.dot_general((k * kdec_f).astype(BF16), v, tn, preferred_element_type=F32)
            q, k, v, rows = load(refs, n - 1 - i, rotate)
            ob_ref[rows, :] = qdec_b * jnp.dot(q, s_b.astype(BF16), preferred_element_type=F32)
            s_b = cdec_b * s_b + lax.dot_general((k * kdec_b).astype(BF16), v, tn, preferred_element_type=F32)
            return s_f, s_b
        return body

    lat, ctx = (ql_ref, kl_ref, vl_ref), (qc_ref, kc_ref, vc_ref)
    s0 = jnp.zeros((d, d), F32)
    state = lax.fori_loop(0, n_c, both(ctx, oc_ref, ocb_ref, False, n_c), (s0, s0))
    lax.fori_loop(0, n_l, both(lat, ol_ref, olb_ref, True, n_l), state, unroll=2)

    def finish(of_ref, ob_ref, g_ref, y_ref):
        def body(c, carry):
            rows = pl.ds(pl.multiple_of(c * FINISH_ROWS, FINISH_ROWS), FINISH_ROWS)
            g = g_ref[0, rows, :]
            y_ref[0, rows, :] = _head_norm_rows(of_ref[rows, :] + ob_ref[rows, :]) * (g * jax.nn.sigmoid(g))
            return carry
        lax.fori_loop(0, of_ref.shape[0] // FINISH_ROWS, body, 0)

    finish(ol_ref, olb_ref, gl_ref, yl_ref)
    finish(oc_ref, ocb_ref, gc_ref, yc_ref)


def retention_branch(proj_l, proj_c, ret_decay, cos, sin):
    bsz, t_l, _ = proj_l.shape
    t_c = proj_c.shape[1]
    d = BR_W // RET_HEADS
    blk0 = OFF_RET // d

    def col(t, part):
        return pl.BlockSpec((1, t, d), functools.partial(lambda b, h, dec, p: (b, 0, blk0 + p * RET_HEADS + h), p=part))

    table = pl.BlockSpec((t_l, LANE), lambda b, h, dec: (0, 0))
    grid_spec = pltpu.PrefetchScalarGridSpec(
        num_scalar_prefetch=1,
        grid=(bsz, RET_HEADS),
        in_specs=[col(t_l, p) for p in range(4)] + [col(t_c, p) for p in range(4)] + [table, table],
        out_specs=[pl.BlockSpec((1, t_l, d), lambda b, h, dec: (b, 0, h)),
                   pl.BlockSpec((1, t_c, d), lambda b, h, dec: (b, 0, h))],
        scratch_shapes=[pltpu.VMEM((t_l, d), F32), pltpu.VMEM((t_c, d), F32),
                        pltpu.VMEM((t_l, d), F32), pltpu.VMEM((t_c, d), F32)])
    return pl.pallas_call(
        _ret_kernel,
        grid_spec=grid_spec,
        out_shape=[jax.ShapeDtypeStruct((bsz, t_l, BR_W), F32), jax.ShapeDtypeStruct((bsz, t_c, BR_W), F32)],
        compiler_params=_cparams(("parallel", "parallel")),
        name="retention_branch",
    )(ret_decay, proj_l, proj_l, proj_l, proj_l, proj_c, proj_c, proj_c, proj_c, cos, sin)


SHORT_PAD = 8


def _mlstm_kernel(ql_ref, kl_ref, vl_ref, ol_ref, gl_ref, qc_ref, kc_ref, vc_ref, oc_ref, gc_ref,
                  wq_ref, wk_ref, bq_ref, bk_ref, gb_ref, yl_ref, yc_ref,
                  pad_ref, qsl_ref, ksl_ref, qsc_ref, ksc_ref, hl_ref, hc_ref, hlb_ref, hcb_ref):
    h = pl.program_id(1)
    t_l, t_c = ql_ref.shape[1], qc_ref.shape[1]
    d = ql_ref.shape[2]
    row = lax.broadcasted_iota(jnp.int32, (CHUNK, CHUNK), 0)
    col = lax.broadcasted_iota(jnp.int32, (CHUNK, CHUNK), 1)
    eye = row == col
    scale = d ** -0.5
    nt = (((1,), (1,)), ((), ()))
    tn = (((0,), (0,)), ((), ()))

    def short_conv(x_ref, w_ref, b_ref, out_ref, t):
        k_taps = w_ref.shape[0]
        first = SHORT_PAD - (k_taps - 1) // 2
        pad_ref[0:SHORT_PAD, :] = jnp.zeros((SHORT_PAD, d), F32)
        pad_ref[SHORT_PAD + t:SHORT_PAD + t + SHORT_PAD, :] = jnp.zeros((SHORT_PAD, d), F32)

        def copy(i, carry):
            pad_ref[pl.ds(pl.multiple_of(SHORT_PAD + i * CHUNK, 8), CHUNK), :] = \
                x_ref[0, pl.ds(pl.multiple_of(i * CHUNK, CHUNK), CHUNK), :]
            return carry

        lax.fori_loop(0, t // CHUNK, copy, 0)

        def conv(i, carry):
            base = pl.multiple_of(i * CHUNK, CHUNK)
            win = pad_ref[pl.ds(base, CHUNK + 2 * SHORT_PAD), :]
            acc = jnp.broadcast_to(b_ref[...], (CHUNK, d))
            for k in range(k_taps):
                acc = acc + w_ref[k:k + 1, :] * win[first + k:first + k + CHUNK]
            out_ref[pl.ds(base, CHUNK), :] = acc * jax.nn.sigmoid(acc)
            return carry

        lax.fori_loop(0, t // CHUNK, conv, 0)

    short_conv(ql_ref, wq_ref, bq_ref, qsl_ref, t_l)
    short_conv(kl_ref, wk_ref, bk_ref, ksl_ref, t_l)
    short_conv(qc_ref, wq_ref, bq_ref, qsc_ref, t_c)
    short_conv(kc_ref, wk_ref, bk_ref, ksc_ref, t_c)

    def gate_cols(g_ref, rows, i_lane, f_lane):
        x = g_ref[0, rows, :] + gb_ref[...]
        i_col = jnp.sum(jnp.where(col == i_lane, x, 0.0), axis=1, keepdims=True)
        f_col = jax.nn.log_sigmoid(jnp.sum(jnp.where(col == f_lane, x, 0.0), axis=1, keepdims=True))
        i_row = jnp.sum(jnp.where(eye, i_col, 0.0), axis=0, keepdims=True)
        f_row = jnp.sum(jnp.where(eye, f_col, 0.0), axis=0, keepdims=True)
        return i_col, f_col, i_row, f_row

    def chunk_step(seq, c, state, forward):
        qs_ref, ks_ref, v_ref, g_ref = seq[:4]
        h_ref = seq[4] if forward else seq[5]
        cmat, nvec, m = state
        rows = pl.ds(pl.multiple_of(c * CHUNK, CHUNK), CHUNK)
        q = qs_ref[rows, :]
        k = ks_ref[rows, :] * scale
        qb, vb = q.astype(BF16), v_ref[0, rows, :].astype(BF16)
        i_col, f_col, i_row, f_row = gate_cols(g_ref, rows, (0 if forward else 2 * MLSTM_HEADS) + h,
                                               (MLSTM_HEADS if forward else 3 * MLSTM_HEADS) + h)
        seen = (col <= row) if forward else (col >= row)
        seen_t = (row <= col) if forward else (row >= col)
        b_col = jnp.sum(jnp.where(seen, f_row, 0.0), axis=1, keepdims=True)
        b_row = jnp.sum(jnp.where(seen_t, f_col, 0.0), axis=0, keepdims=True)
        b_end = jnp.sum(f_row, axis=1, keepdims=True)
        logw = jnp.where(seen, b_col - b_row + i_row, -jnp.inf)
        inter = b_col + m
        m_t = jnp.maximum(inter, jnp.max(logw, axis=1, keepdims=True))
        s = lax.dot_general(qb, k.astype(BF16), nt, preferred_element_type=F32) * jnp.exp(logw - m_t)
        w_prev = jnp.exp(inter - m_t)
        num = (jnp.dot(s.astype(BF16), vb, preferred_element_type=F32)
               + w_prev * jnp.dot(qb, cmat.astype(BF16), preferred_element_type=F32))
        den = jnp.sum(s, axis=1, keepdims=True) + w_prev * jnp.sum(q * nvec, axis=1, keepdims=True)
        hc = num * (1.0 / jnp.maximum(jnp.abs(den), jnp.exp(-m_t)))
        h_ref[rows, :] = hc
        log_wk = b_end - b_col + i_col
        m_new = jnp.maximum(b_end + m, jnp.max(log_wk, axis=0, keepdims=True))
        decay = jnp.exp(b_end + m - m_new)
        kw = k * jnp.exp(log_wk - m_new)
        cmat = decay * cmat + lax.dot_general(kw.astype(BF16), vb, tn, preferred_element_type=F32)
        nvec = decay * nvec + jnp.sum(kw, axis=0, keepdims=True)
        return cmat, nvec, m_new

    lat = (qsl_ref, ksl_ref, vl_ref, gl_ref, hl_ref, hlb_ref)
    ctx = (qsc_ref, ksc_ref, vc_ref, gc_ref, hc_ref, hcb_ref)
    n_l, n_c = t_l // CHUNK, t_c // CHUNK
    state0 = (jnp.zeros((d, d), F32), jnp.zeros((1, d), F32), jnp.full((1, 1), NEG_INIT, F32))

    def both(seq, n):
        def body(i, carry):
            return chunk_step(seq, i, carry[0], True), chunk_step(seq, n - 1 - i, carry[1], False)
        return body

    st = lax.fori_loop(0, n_c, both(ctx, n_c), (state0, state0))
    lax.fori_loop(0, n_l, both(lat, n_l), st, unroll=2)

    def finish(hf_ref, hb_ref, o_ref, y_ref):
        def body(c, carry):
            rows = pl.ds(pl.multiple_of(c * FINISH_ROWS, FINISH_ROWS), FINISH_ROWS)
            y_ref[0, rows, :] = (_head_norm_rows(hf_ref[rows, :] + hb_ref[rows, :])
                                 * jax.nn.sigmoid(o_ref[0, rows, :]))
            return carry
        lax.fori_loop(0, hf_ref.shape[0] // FINISH_ROWS, body, 0)

    finish(hl_ref, hlb_ref, ol_ref, yl_ref)
    finish(hc_ref, hcb_ref, oc_ref, yc_ref)


def mlstm_branch(proj_l, proj_c, conv_w, conv_b, gate_b):
    bsz, t_l, _ = proj_l.shape
    t_c = proj_c.shape[1]
    d = BR_W // MLSTM_HEADS
    blk0 = OFF_MLSTM // d
    k_taps = conv_w.shape[0]

    def col(t, part):
        return pl.BlockSpec((1, t, d), functools.partial(lambda b, h, p: (b, 0, blk0 + p * MLSTM_HEADS + h), p=part))

    def gates(t):
        return pl.BlockSpec((1, t, LANE), lambda b, h: (b, 0, OFF_MGATE // LANE))

    gb = jnp.zeros((1, LANE), F32).at[0, :MLSTM_GATES].set(gate_b.reshape(-1))
    return pl.pallas_call(
        _mlstm_kernel,
        grid=(bsz, MLSTM_HEADS),
        in_specs=[col(t_l, p) for p in range(4)] + [gates(t_l)] + [col(t_c, p) for p in range(4)] + [gates(t_c)] + [
            pl.BlockSpec((k_taps, d), lambda b, h: (0, h)),
            pl.BlockSpec((k_taps, d), lambda b, h: (0, MLSTM_HEADS + h)),
            pl.BlockSpec((1, d), lambda b, h: (0, h)),
            pl.BlockSpec((1, d), lambda b, h: (0, MLSTM_HEADS + h)),
            pl.BlockSpec((1, LANE), lambda b, h: (0, 0))],
        out_specs=[pl.BlockSpec((1, t_l, d), lambda b, h: (b, 0, h)),
                   pl.BlockSpec((1, t_c, d), lambda b, h: (b, 0, h))],
        out_shape=[jax.ShapeDtypeStruct((bsz, t_l, BR_W), F32), jax.ShapeDtypeStruct((bsz, t_c, BR_W), F32)],
        scratch_shapes=[pltpu.VMEM((t_l + 2 * SHORT_PAD, d), F32),
                        pltpu.VMEM((t_l, d), F32), pltpu.VMEM((t_l, d), F32),
                        pltpu.VMEM((t_c, d), F32), pltpu.VMEM((t_c, d), F32),
                        pltpu.VMEM((t_l, d), F32), pltpu.VMEM((t_c, d), F32),
                        pltpu.VMEM((t_l, d), F32), pltpu.VMEM((t_c, d), F32)],
        compiler_params=_cparams(("parallel", "parallel")),
        name="mlstm_branch",
    )(proj_l, proj_l, proj_l, proj_l, proj_l, proj_c, proj_c, proj_c, proj_c, proj_c,
      conv_w, conv_w, conv_b.reshape(1, -1), conv_b.reshape(1, -1), gb)


def _attn_kernel(sink_ref, q_ref, *refs, local):
    if local:
        kp_ref, kc_ref, kn_ref, vp_ref, vc_ref, vn_ref, kx_ref, vx_ref, cos_ref, sin_ref, o_ref, k_scr, v_scr = refs
    else:
        kx_ref, vx_ref, o_ref, k_scr, v_scr = refs
    i = pl.program_id(1)
    nq = pl.num_programs(1)
    t_x = kx_ref.shape[1]
    n_loc = 3 * BLOCK if local else 0
    n_keys = n_loc + t_x
    nf = ATTN_DIM // 4
    half = LANE // 2

    def put(dst, lo, x):
        dst[0, lo:lo + x.shape[0], :] = x.astype(BF16)
        dst[1, lo:lo + x.shape[0], :] = pltpu.roll(x, half, 1).astype(BF16)

    if local:
        for j, (k_ref, v_ref) in enumerate(((kp_ref, vp_ref), (kc_ref, vc_ref), (kn_ref, vn_ref))):
            blk = jnp.clip(i + (j - 1), 0, nq - 1)
            rows = pl.ds(pl.multiple_of(blk * BLOCK, BLOCK), BLOCK)
            put(k_scr, j * BLOCK, _rope(k_ref[0], cos_ref[rows, :], sin_ref[rows, :], nf))
            put(v_scr, j * BLOCK, v_ref[0])
    put(k_scr, n_loc, kx_ref[0])
    put(v_scr, n_loc, vx_ref[0])

    lane = lax.broadcasted_iota(jnp.int32, (BLOCK, LANE), 1)
    if local:
        rel = (lax.broadcasted_iota(jnp.int32, (BLOCK, n_keys), 1)
               - lax.broadcasted_iota(jnp.int32, (BLOCK, n_keys), 0))
        kpos = (i - 1) * BLOCK + lax.broadcasted_iota(jnp.int32, (BLOCK, n_keys), 1)
        ok = (rel >= 0) & (rel <= 2 * WINDOW) & (kpos >= 0) & (kpos < nq * BLOCK)
        ok = ok | (lax.broadcasted_iota(jnp.int32, (BLOCK, n_keys), 1) >= n_loc)
        bias = jnp.where(ok, 0.0, -jnp.inf)
        qrows = pl.ds(pl.multiple_of(i * BLOCK, BLOCK), BLOCK)
        cos_q, sin_q = cos_ref[qrows, :], sin_ref[qrows, :]
    nt = (((1,), (1,)), ((), ()))
    swaps, scores, sinks = [], [], []
    for p in range(ATTN_HEADS // 2):
        qp = q_ref[0, :, p * LANE:(p + 1) * LANE]
        if local:
            qp = _rope(qp, cos_q, sin_q, nf)
        qp = qp * (ATTN_DIM ** -0.5)
        for sub in range(2):
            hd = 2 * p + sub
            kv_half = hd // (ATTN_HEADS // ATTN_KV_HEADS)
            sel = (lane >= half) if sub else (lane < half)
            qm = jnp.where(sel, qp, 0.0).astype(BF16)
            swaps.append(0 if kv_half == sub else 1)
            scores.append(lax.dot_general(qm, k_scr[swaps[-1]], nt, preferred_element_type=F32))
            sinks.append(jnp.full((BLOCK, 1), sink_ref[hd], F32))
    s = jnp.stack(scores)
    sink = jnp.stack(sinks)
    if local:
        s = s + bias[None]
    m = jnp.maximum(jnp.max(s, axis=2, keepdims=True), sink)
    e = jnp.exp(s - m)
    inv = 1.0 / (jnp.sum(e, axis=2, keepdims=True) + jnp.exp(sink - m))
    prob = (e * inv).astype(BF16)
    for p in range(ATTN_HEADS // 2):
        outs = [jnp.dot(prob[2 * p + sub], v_scr[swaps[2 * p + sub]], preferred_element_type=F32)
                for sub in range(2)]
        o_ref[0, :, p * LANE:(p + 1) * LANE] = jnp.where(lane < half, outs[0], outs[1])


def attention_branch(proj_q, proj_c, sink, cos, sin, local):
    bsz, t, _ = proj_q.shape
    t_c = proj_c.shape[1]
    nq = t // BLOCK
    n_qcol = ATTN_HEADS * ATTN_DIM
    q_blk = OFF_ATTN // n_qcol
    k_blk = (OFF_ATTN + n_qcol) // LANE
    v_blk = k_blk + 1
    n_keys = (3 * BLOCK if local else 0) + t_c

    def kv(blk, shift):
        return pl.BlockSpec((1, BLOCK, LANE), lambda b, i, s: (b, jnp.clip(i + shift, 0, nq - 1), blk))

    in_specs = [pl.BlockSpec((1, BLOCK, n_qcol), lambda b, i, s: (b, i, q_blk))]
    args = [proj_q]
    if local:
        in_specs += [kv(k_blk, -1), kv(k_blk, 0), kv(k_blk, 1), kv(v_blk, -1), kv(v_blk, 0), kv(v_blk, 1)]
        args += [proj_q] * 6
    in_specs += [pl.BlockSpec((1, t_c, LANE), lambda b, i, s: (b, 0, k_blk)),
                 pl.BlockSpec((1, t_c, LANE), lambda b, i, s: (b, 0, v_blk))]
    args += [proj_c, proj_c]
    if local:
        in_specs += [pl.BlockSpec((t, LANE), lambda b, i, s: (0, 0))] * 2
        args += [cos, sin]
    grid_spec = pltpu.PrefetchScalarGridSpec(
        num_scalar_prefetch=1,
        grid=(bsz, nq),
        in_specs=in_specs,
        out_specs=pl.BlockSpec((1, BLOCK, n_qcol), lambda b, i, s: (b, i, 0)),
        scratch_shapes=[pltpu.VMEM((2, n_keys, LANE), BF16), pltpu.VMEM((2, n_keys, LANE), BF16)])
    return pl.pallas_call(
        functools.partial(_attn_kernel, local=local),
        grid_spec=grid_spec,
        out_shape=jax.ShapeDtypeStruct((bsz, t, n_qcol), F32),
        compiler_params=_cparams(("parallel", "parallel")),
        name="attention_branch",
    )(sink, *args)


def _pack_w_in(w):
    d = w.shape[0]
    used = w.shape[1] - MLSTM_GATES + LANE
    cols = [w[:, :REF_MGATE_OFF], w[:, REF_GATE_OFF:], w[:, REF_ATTN_OFF:REF_GATE_OFF],
            w[:, REF_MGATE_OFF:REF_ATTN_OFF],
            jnp.zeros((d, _round_up(used, PROJ_TILE_N) - w.shape[1]), w.dtype)]
    return jnp.concatenate(cols, axis=1).astype(BF16)


def _final_norm(x, g):
    y = x * lax.rsqrt(jnp.mean(x * x, axis=-1, keepdims=True) + EPS)
    return y * g


def kernel(x, c, ctx, c_ctx, mod_w, mod_b, norm1_g, norm2_g, w_in, conv_w, conv_b, conv_ln_g, conv_ln_b,
           ret_decay, mlstm_conv_w, mlstm_conv_b, mlstm_gate_b, attn_sink, w_branch, w_out,
           peer_wq, peer_keys, peer_u, peer_v, final_g):
    bsz, seq, d = x.shape
    assert d == D_MODEL
    depth = mod_w.shape[0]
    cos_r, sin_r = rope_tables(seq, BR_W // RET_HEADS)
    cos_a, sin_a = rope_tables(seq, ATTN_DIM)
    n_cond = _round_up(bsz + 1, 8)
    cond = jnp.concatenate([c, c_ctx[None], jnp.zeros((n_cond - bsz - 1, d), F32)], axis=0)
    for l in range(depth):
        last = l == depth - 1
        mod = mod_vectors(cond, mod_w[l], mod_b[l])
        sh1, sc1, ga1, sh2, sc2, ga2 = (m[:, None, :] for m in jnp.split(mod[:bsz], 6, axis=-1))
        csh1, csc1, cga1, csh2, csc2, cga2 = (jnp.broadcast_to(m[None], (bsz, 1, d))
                                              for m in jnp.split(mod[bsz:bsz + 1], 6, axis=-1))
        w_in_p = _pack_w_in(w_in[l])
        proj_l = norm_matmul(x, norm1_g[l], sh1, sc1, w_in_p)
        proj_c = norm_matmul(ctx, norm1_g[l], csh1, csc1, w_in_p)
        ret_l, ret_c = retention_branch(proj_l, proj_c, ret_decay[l], cos_r, sin_r)
        mls_l, mls_c = mlstm_branch(proj_l, proj_c, mlstm_conv_w[l], mlstm_conv_b[l], mlstm_gate_b[l])
        conv_args = (conv_w[l], conv_b[l], conv_ln_g[l], conv_ln_b[l])
        ys_l = (conv_branch(proj_l, *conv_args), ret_l, mls_l,
                attention_branch(proj_l, proj_c, attn_sink[l], cos_a, sin_a, True))
        if not last:
            ys_c = (conv_branch(proj_c, *conv_args), ret_c, mls_c,
                    attention_branch(proj_c, proj_c, attn_sink[l], cos_a, sin_a, False))
        wb, wo = w_branch[l].astype(BF16), w_out[l].astype(BF16)
        wqt = peer_wq[l].T.astype(BF16)
        u = peer_u[l].astype(BF16)
        vt = peer_v[l].T.astype(BF16)

        def peer(xx, sh, sc, ga):
            xm, *tabs = peer_prep(xx, norm2_g[l], sh, sc, wqt, peer_keys[l])
            return peer_dense(xm, *tabs, u, vt, xx, ga)

        x = merge(ys_l, proj_l, wb, wo, x, ga1)
        x = peer(x, sh2, sc2, ga2)
        if not last:
            ctx = merge(ys_c, proj_c, wb, wo, ctx, cga1)
            ctx = peer(ctx, csh2, csc2, cga2)
    return _final_norm(x, final_g)
```

```python
import functools

import jax
import jax.numpy as jnp
import numpy as np
from jax import lax
from jax.experimental import pallas as pl
from jax.experimental.pallas import tpu as pltpu

F32 = jnp.float32
BF16 = jnp.bfloat16

GRID_W = 64
EPS = 1e-6
N_BRANCH = 4
BR_W = 512
RET_HEADS = 4
MLSTM_HEADS = 4
MLSTM_GATES = 4 * MLSTM_HEADS
ATTN_HEADS = 8
ATTN_KV_HEADS = 2
ATTN_DIM = BR_W // ATTN_HEADS
WINDOW = 128
BLOCK = 128
CHUNK = 128
ROPE_BASE = 10000.0
NEG_INIT = -1e30
PEER_HEADS = 8
PEER_KEYS = 128
PEER_TOPK = 16

CONV_IN = 2 * BR_W
RET_IN = 4 * BR_W
MLSTM_QKVO = 4 * BR_W
ATTN_IN = ATTN_HEADS * ATTN_DIM + 2 * ATTN_KV_HEADS * ATTN_DIM
REF_MGATE_OFF = CONV_IN + RET_IN + MLSTM_QKVO
REF_ATTN_OFF = REF_MGATE_OFF + MLSTM_GATES
REF_GATE_OFF = REF_ATTN_OFF + ATTN_IN
OFF_CONV = 0
OFF_RET = OFF_CONV + CONV_IN
OFF_MLSTM = OFF_RET + RET_IN
OFF_GATE = OFF_MLSTM + MLSTM_QKVO
D_MODEL = 1024
OFF_ATTN = OFF_GATE + N_BRANCH * D_MODEL
OFF_MGATE = OFF_ATTN + ATTN_IN
LANE = 128
BF16_ROWS = 16
PROJ_TILE_N = 1024


def _round_up(a, m):
    return (a + m - 1) // m * m


VMEM_LIMIT = 56 * 1024 * 1024


def _cparams(sem):
    return pltpu.CompilerParams(dimension_semantics=sem, vmem_limit_bytes=VMEM_LIMIT)


def _mod_kernel(s_ref, w_ref, b_ref, o_ref):
    s = s_ref[...]
    s = s * jax.nn.sigmoid(s)
    o_ref[...] = jnp.dot(s, w_ref[...], preferred_element_type=F32,
                         precision=lax.Precision.HIGHEST) + b_ref[...]


def mod_vectors(cond, w, b):
    r, d = cond.shape
    n = w.shape[1]
    tn = 512
    return pl.pallas_call(
        _mod_kernel,
        grid=(n // tn,),
        in_specs=[pl.BlockSpec((r, d), lambda j: (0, 0)),
                  pl.BlockSpec((d, tn), lambda j: (0, j)),
                  pl.BlockSpec((1, tn), lambda j: (0, j))],
        out_specs=pl.BlockSpec((r, tn), lambda j: (0, j)),
        out_shape=jax.ShapeDtypeStruct((r, n), F32),
        compiler_params=_cparams(("parallel",)),
        name="mod_vectors",
    )(cond, w, b.reshape(1, n))


def _norm_matmul_kernel(x_ref, g_ref, sh_ref, sc_ref, w_ref, o_ref, h_ref):
    @pl.when(pl.program_id(2) == 0)
    def _():
        x = x_ref[0]
        y = x * lax.rsqrt(jnp.mean(x * x, axis=-1, keepdims=True) + EPS)
        h_ref[...] = ((y * g_ref[...]) * (1.0 + sc_ref[0]) + sh_ref[0]).astype(BF16)

    o_ref[0] = jnp.dot(h_ref[...], w_ref[...], preferred_element_type=F32)


def norm_matmul(x, g, shift, scale, w):
    bsz, t, d = x.shape
    n = w.shape[1]
    tm = min(t, 2048)
    tn = min(n, PROJ_TILE_N)
    return pl.pallas_call(
        _norm_matmul_kernel,
        grid=(bsz, t // tm, n // tn),
        in_specs=[pl.BlockSpec((1, tm, d), lambda b, i, j: (b, i, 0)),
                  pl.BlockSpec((1, d), lambda b, i, j: (0, 0)),
                  pl.BlockSpec((1, 1, d), lambda b, i, j: (b, 0, 0)),
                  pl.BlockSpec((1, 1, d), lambda b, i, j: (b, 0, 0)),
                  pl.BlockSpec((d, tn), lambda b, i, j: (0, j))],
        out_specs=pl.BlockSpec((1, tm, tn), lambda b, i, j: (b, i, j)),
        out_shape=jax.ShapeDtypeStruct((bsz, t, n), F32),
        scratch_shapes=[pltpu.VMEM((tm, d), BF16)],
        compiler_params=_cparams(("parallel", "parallel", "arbitrary")),
        name="norm_matmul",
    )(x, g.reshape(1, d), shift, scale, w)


def _merge_kernel(y0, y1, y2, y3, g0, g1, g2, g3, wb_ref, wo_ref, x_ref, ga_ref, o_ref):
    acc = None
    for i, (y, g) in enumerate(((y0, g0), (y1, g1), (y2, g2), (y3, g3))):
        p = jnp.dot(y[0].astype(BF16), wb_ref[i], preferred_element_type=F32)
        t = jax.nn.sigmoid(g[0]) * p
        acc = t if acc is None else acc + t
    o = jnp.dot(acc.astype(BF16), wo_ref[...], preferred_element_type=F32)
    o_ref[0] = x_ref[0] + ga_ref[0] * o


def merge(ys, proj, w_branch, w_out, x, ga):
    bsz, t, d = x.shape
    tm = min(t, 512)
    gate_blk = OFF_GATE // d
    y_spec = pl.BlockSpec((1, tm, BR_W), lambda b, i: (b, i, 0))
    g_specs = [pl.BlockSpec((1, tm, d), functools.partial(lambda b, i, k: (b, i, gate_blk + k), k=k))
               for k in range(N_BRANCH)]
    return pl.pallas_call(
        _merge_kernel,
        grid=(bsz, t // tm),
        in_specs=[y_spec] * N_BRANCH + g_specs + [
            pl.BlockSpec((N_BRANCH, BR_W, d), lambda b, i: (0, 0, 0)),
            pl.BlockSpec((d, d), lambda b, i: (0, 0)),
            pl.BlockSpec((1, tm, d), lambda b, i: (b, i, 0)),
            pl.BlockSpec((1, 1, d), lambda b, i: (b, 0, 0))],
        out_specs=pl.BlockSpec((1, tm, d), lambda b, i: (b, i, 0)),
        out_shape=jax.ShapeDtypeStruct((bsz, t, d), F32),
        compiler_params=_cparams(("parallel", "parallel")),
        name="merge",
    )(*ys, proj, proj, proj, proj, w_branch, w_out, x, ga)


PEER_TE = 2048
PEER_SUB = 512
PEER_SECOND = 1
PEER_GROUP = 4


def _gelu_exact(x):
    return 0.5 * x * (1.0 + lax.erf(x * float(1.0 / np.sqrt(2.0))))


def _peer_kernel(xm_ref, a_ref, lb_ref, b_ref, r2_ref, u_ref, vt_ref, x_ref, ga_ref, gain_ref, o_ref,
                 acc_ref, w_ref, h_ref, *, final_norm):
    j = pl.program_id(2)

    @pl.when(j == 0)
    def _():
        acc_ref[...] = jnp.zeros_like(acc_ref)

    tn = xm_ref.shape[1]

    def spread(row):
        return jnp.broadcast_to(row, (BF16_ROWS, LANE)).astype(BF16)[None]

    def slabs(words):
        return pltpu.bitcast(words, BF16).reshape(PEER_KEYS // BF16_ROWS, BF16_ROWS, LANE)

    n_sub = PEER_TE // PEER_SUB
    n_slot = tn // LANE
    rows_per_sub = PEER_SUB // PEER_KEYS

    def first_matmul(sub):
        h_ref[sub % 2] = lax.dot_general(u_ref[pl.ds(sub * PEER_SUB, PEER_SUB), :], xm_ref[0],
                                         (((1,), (1,)), ((), ())), preferred_element_type=F32)

    def second_matmul(sub):
        experts = pl.ds(sub * PEER_SUB, PEER_SUB)
        acc_ref[...] += jnp.dot(vt_ref[:, experts], w_ref[experts, :], preferred_element_type=F32)

    def gate_slice(sub, blk):
        lanes = pl.ds(blk * LANE, LANE)
        lo = blk * LANE
        for grp in range(rows_per_sub // PEER_GROUP):
            row0 = sub * rows_per_sub + grp * PEER_GROUP
            first_rows = [pl.ds(j * (PEER_TE // PEER_KEYS) + row0 + a, 1) for a in range(PEER_GROUP)]
            g = [None] * PEER_GROUP
            for h in range(PEER_HEADS):
                r2 = slabs(r2_ref[0, h, :, lanes])
                b = slabs(b_ref[0, h, :, lanes])
                for a in range(PEER_GROUP):
                    limit = spread(lb_ref[0, h, first_rows[a], :][:, lo:lo + LANE])
                    t = jnp.where(r2 < limit, b, 0.0) * spread(a_ref[0, h, first_rows[a], :][:, lo:lo + LANE])
                    g[a] = t if g[a] is None else g[a] + t
            for a in range(PEER_GROUP):
                local = pl.ds((grp * PEER_GROUP + a) * PEER_KEYS, PEER_KEYS)
                act = _gelu_exact(h_ref[sub % 2, local, lanes]).astype(BF16)
                w_ref[pl.ds((row0 + a) * PEER_KEYS, PEER_KEYS), lanes] = g[a].reshape(PEER_KEYS, LANE) * act

    first_matmul(0)
    for sub in range(n_sub):
        if sub + 1 < n_sub:
            first_matmul(sub + 1)
        if sub >= 1:
            second_matmul(sub - 1)
        for k in range(n_slot):
            gate_slice(sub, k)
    second_matmul(n_sub - 1)

    @pl.when(j == pl.num_programs(2) - 1)
    def _():
        y = x_ref[0] + ga_ref[0] * acc_ref[...].T
        if final_norm:
            y = y * lax.rsqrt(jnp.mean(y * y, axis=-1, keepdims=True) + EPS) * gain_ref[...]
        o_ref[0] = y


def peer_dense(xm, a_t, lb_t, b_t, r2_t, u, vt, x, ga, out_gain=None):
    bsz, t, d = x.shape
    final_norm = out_gain is not None
    gain = (out_gain if final_norm else jnp.ones((d,), F32)).reshape(1, d)
    n_exp = u.shape[0]
    tn = min(t, 512)
    tab = pl.BlockSpec((1, PEER_HEADS, PEER_KEYS, tn), lambda b, i, j: (b, 0, 0, i))
    pair = pl.BlockSpec((1, PEER_HEADS, PEER_KEYS // 2, tn), lambda b, i, j: (b, 0, 0, i))
    return pl.pallas_call(
        functools.partial(_peer_kernel, final_norm=final_norm),
        grid=(bsz, t // tn, n_exp // PEER_TE),
        in_specs=[pl.BlockSpec((1, tn, d), lambda b, i, j: (b, i, 0)),
                  tab, tab, pair, pair,
                  pl.BlockSpec((PEER_TE, d), lambda b, i, j: (j, 0)),
                  pl.BlockSpec((d, PEER_TE), lambda b, i, j: (0, j)),
                  pl.BlockSpec((1, tn, d), lambda b, i, j: (b, i, 0)),
                  pl.BlockSpec((1, 1, d), lambda b, i, j: (b, 0, 0)),
                  pl.BlockSpec((1, d), lambda b, i, j: (0, 0))],
        out_specs=pl.BlockSpec((1, tn, d), lambda b, i, j: (b, i, 0)),
        out_shape=jax.ShapeDtypeStruct((bsz, t, d), F32),
        scratch_shapes=[pltpu.VMEM((d, tn), F32), pltpu.VMEM((PEER_TE, tn), BF16),
                        pltpu.VMEM((2, PEER_SUB, tn), F32)],
        compiler_params=_cparams(("parallel", "parallel", "arbitrary")),
        name="peer_dense",
    )(xm, a_t, lb_t, b_t, r2_t, u, vt, x, ga, gain)


def _extract_top(s, row, n_pick, lowest_row_only):
    rank = jnp.full(s.shape, float(n_pick), F32)
    tops = []
    for k in range(n_pick):
        m = jnp.max(s, axis=0, keepdims=True)
        hit = s == m
        if lowest_row_only:
            hit = row == jnp.min(jnp.where(hit, row, s.shape[0]), axis=0, keepdims=True)
        rank = jnp.where(hit, float(k), rank)
        s = jnp.where(hit, -jnp.inf, s)
        tops.append(m)
    n_marked = jnp.sum(jnp.where(rank < float(n_pick), 1.0, 0.0), axis=0, keepdims=True)
    n_bad = jnp.sum(jnp.where(n_marked != float(n_pick), 1.0, 0.0))
    return rank, jnp.concatenate(tops, axis=0), n_bad


def _peer_prep_kernel(x_ref, g_ref, sh_ref, sc_ref, wqt_ref, keys_ref,
                      xm_ref, a_ref, lb_ref, b_ref, r2_ref, qt_ref):
    tn = x_ref.shape[1]
    x = x_ref[0]
    y = x * lax.rsqrt(jnp.mean(x * x, axis=-1, keepdims=True) + EPS)
    xm = ((y * g_ref[...]) * (1.0 + sc_ref[0]) + sh_ref[0]).astype(BF16)
    xm_ref[0] = xm
    qt_ref[...] = lax.dot_general(wqt_ref[...], xm, (((1,), (1,)), ((), ())), preferred_element_type=F32)
    half = keys_ref.shape[2]
    row = lax.broadcasted_iota(jnp.int32, (PEER_KEYS, LANE), 0)
    sub = lax.broadcasted_iota(jnp.int32, (8, LANE), 0)
    n_cand_rows = PEER_TOPK + 8 * (PEER_TOPK // 2 - 1) + PEER_TOPK // 2
    crow = lax.broadcasted_iota(jnp.int32, (n_cand_rows, LANE), 0)

    def head_tables(h, exact):
        n_bad = 0.0
        for blk in range(tn // LANE):
            lanes = pl.ds(blk * LANE, LANE)
            s, rank, tops = [], [], []
            for c in range(2):
                qs = qt_ref[pl.ds(pl.multiple_of((2 * h + c) * half, half), half), lanes]
                sc_ = jnp.dot(keys_ref[c], qs, preferred_element_type=F32, precision=lax.Precision.HIGHEST)
                r, t, bad = _extract_top(sc_, row, PEER_TOPK, exact)
                n_bad = n_bad + bad
                s.append(sc_)
                rank.append(r)
                tops.append(t)
            t1, t2 = tops
            groups = [t1[0:1] + t2]
            for a in range(1, PEER_TOPK // 2):
                n_valid = PEER_TOPK // (a + 1)
                grp = t1[a:a + 1] + t2[0:8]
                groups.append(grp if n_valid >= 8 else jnp.where(sub < n_valid, grp, -jnp.inf))
            groups.append(t1[PEER_TOPK // 2:] + t2[0:1])
            cand = jnp.concatenate(groups, axis=0)
            crank, _, bad = _extract_top(cand, crow, PEER_TOPK, exact)
            n_bad = n_bad + bad
            picked = crank < float(PEER_TOPK)
            z = jnp.sum(jnp.where(picked, jnp.exp(cand - cand[0:1]), 0.0), axis=0, keepdims=True)
            pk = picked.astype(F32)
            counts = [jnp.sum(pk[0:PEER_TOPK], axis=0, keepdims=True)]
            for a in range(1, PEER_TOPK // 2):
                lo = PEER_TOPK + 8 * (a - 1)
                counts.append(jnp.sum(pk[lo:lo + 8], axis=0, keepdims=True))
            lo = PEER_TOPK + 8 * (PEER_TOPK // 2 - 1)
            for a in range(PEER_TOPK // 2):
                counts.append(pk[lo + a:lo + a + 1])
            lb = jnp.zeros((PEER_KEYS, LANE), F32)
            for a in range(PEER_TOPK):
                lb = jnp.where(rank[0] == float(a), counts[a], lb)
            a_ref[0, h, :, lanes] = jnp.exp(s[0] - t1[0:1])
            lb_ref[0, h, :, lanes] = lb
            b_ref[0, h, :, lanes] = pltpu.bitcast((jnp.exp(s[1] - t2[0:1]) / z).astype(BF16), jnp.uint32)
            r2_ref[0, h, :, lanes] = pltpu.bitcast(rank[1].astype(BF16), jnp.uint32)
        return n_bad

    def head_body(h, carry):
        n_bad = head_tables(h, False)

        @pl.when(n_bad > 0.0)
        def _():
            head_tables(h, True)

        return carry

    lax.fori_loop(0, PEER_HEADS, head_body, 0)


def peer_prep(x, g, shift, scale, wqt, keys):
    bsz, t, d = x.shape
    tn = min(t, 256)
    nq = wqt.shape[0]
    tab_shape = jax.ShapeDtypeStruct((bsz, PEER_HEADS, PEER_KEYS, t), F32)
    tab_spec = pl.BlockSpec((1, PEER_HEADS, PEER_KEYS, tn), lambda b, i: (b, 0, 0, i))
    pair_shape = jax.ShapeDtypeStruct((bsz, PEER_HEADS, PEER_KEYS // 2, t), jnp.uint32)
    pair_spec = pl.BlockSpec((1, PEER_HEADS, PEER_KEYS // 2, tn), lambda b, i: (b, 0, 0, i))
    return pl.pallas_call(
        _peer_prep_kernel,
        grid=(bsz, t // tn),
        in_specs=[pl.BlockSpec((1, tn, d), lambda b, i: (b, i, 0)),
                  pl.BlockSpec((1, d), lambda b, i: (0, 0)),
                  pl.BlockSpec((1, 1, d), lambda b, i: (b, 0, 0)),
                  pl.BlockSpec((1, 1, d), lambda b, i: (b, 0, 0)),
                  pl.BlockSpec((nq, d), lambda b, i: (0, 0)),
                  pl.BlockSpec(keys.shape, lambda b, i: (0, 0, 0))],
        out_specs=[pl.BlockSpec((1, tn, d), lambda b, i: (b, i, 0)), tab_spec, tab_spec, pair_spec, pair_spec],
        out_shape=[jax.ShapeDtypeStruct((bsz, t, d), BF16), tab_shape, tab_shape, pair_shape, pair_shape],
        scratch_shapes=[pltpu.VMEM((nq, tn), F32)],
        compiler_params=_cparams(("parallel", "parallel")),
        name="peer_prep",
    )(x, g.reshape(1, d), shift, scale, wqt, keys)


ROW_TILE = 128
CONV_PAD = 16


def _conv_kernel(p_ref, w_ref, b_ref, lg_ref, lb_ref, o_ref, pad_ref, y_ref):
    t, c = o_ref.shape[1], o_ref.shape[2]
    k_taps = w_ref.shape[0]
    first = CONV_PAD - (k_taps - 1) // 2
    lane_w = 256
    pad_ref[0:CONV_PAD, :] = jnp.zeros((CONV_PAD, c), F32)
    pad_ref[CONV_PAD + t:, :] = jnp.zeros((CONV_PAD, c), F32)

    def glu_body(i, carry):
        rows = pl.ds(pl.multiple_of(i * ROW_TILE, ROW_TILE), ROW_TILE)
        a = p_ref[0, rows, 0:c]
        g = p_ref[0, rows, c:2 * c]
        pad_ref[pl.ds(pl.multiple_of(CONV_PAD + i * ROW_TILE, 8), ROW_TILE), :] = a * jax.nn.sigmoid(g)
        return carry

    lax.fori_loop(0, t // ROW_TILE, glu_body, 0)

    def conv_body(i, carry):
        base = pl.multiple_of(i * ROW_TILE, ROW_TILE)
        for lb in range(c // lane_w):
            lanes = pl.ds(lb * lane_w, lane_w)
            win = pad_ref[pl.ds(base, ROW_TILE + 2 * CONV_PAD), lanes]
            acc = jnp.broadcast_to(b_ref[:, lanes], (ROW_TILE, lane_w))
            for k in range(k_taps):
                acc = acc + w_ref[k:k + 1, lanes] * win[first + k:first + k + ROW_TILE]
            y_ref[:, lanes] = acc
        y = y_ref[...]
        yc = y - jnp.mean(y, axis=-1, keepdims=True)
        z = yc * lax.rsqrt(jnp.mean(yc * yc, axis=-1, keepdims=True) + EPS) * lg_ref[...] + lb_ref[...]
        o_ref[0, pl.ds(base, ROW_TILE), :] = z * jax.nn.sigmoid(z)
        return carry

    lax.fori_loop(0, t // ROW_TILE, conv_body, 0)


def conv_branch(proj, w, b, ln_g, ln_b):
    bsz, t, _ = proj.shape
    k_taps, c = w.shape
    vec = pl.BlockSpec((1, c), lambda i: (0, 0))
    return pl.pallas_call(
        _conv_kernel,
        grid=(bsz,),
        in_specs=[pl.BlockSpec((1, t, 2 * c), lambda i: (i, 0, OFF_CONV // (2 * c))),
                  pl.BlockSpec((k_taps, c), lambda i: (0, 0)), vec, vec, vec],
        out_specs=pl.BlockSpec((1, t, c), lambda i: (i, 0, 0)),
        out_shape=jax.ShapeDtypeStruct((bsz, t, c), F32),
        scratch_shapes=[pltpu.VMEM((t + 2 * CONV_PAD, c), F32), pltpu.VMEM((ROW_TILE, c), F32)],
        compiler_params=_cparams(("parallel",)),
        name="conv_branch",
    )(proj, w, b.reshape(1, c), ln_g.reshape(1, c), ln_b.reshape(1, c))


def rope_tables(n_tok, head_dim):
    nf = head_dim // 4
    lane = np.arange(LANE)
    d_idx = lane % head_dim
    by_col = d_idx >= head_dim // 2
    freqs = ROPE_BASE ** (-(d_idx % nf).astype(np.float32) / nf)
    tok = np.arange(n_tok)
    pos = np.where(by_col[None, :], (tok % GRID_W)[:, None], (tok // GRID_W)[:, None]).astype(np.float32)
    ang = jnp.asarray(pos) * jnp.asarray(freqs, F32)[None, :]
    sign = np.where(d_idx % (2 * nf) < nf, -1.0, 1.0).astype(np.float32)
    return jnp.cos(ang), jnp.sin(ang) * sign[None, :]


def _rope(x, cos, sin_signed, nf):
    lane = lax.broadcasted_iota(jnp.int32, x.shape, 1)
    partner = jnp.where(lane % (2 * nf) < nf, pltpu.roll(x, LANE - nf, 1), pltpu.roll(x, nf, 1))
    return x * cos + partner * sin_signed


FINISH_ROWS = 256


def _head_norm_rows(o):
    oc = o - jnp.mean(o, axis=-1, keepdims=True)
    return oc * lax.rsqrt(jnp.mean(oc * oc, axis=-1, keepdims=True) + EPS)


def _ret_kernel(dec_ref, ql_ref, kl_ref, vl_ref, gl_ref, qc_ref, kc_ref, vc_ref, gc_ref, cos_ref, sin_ref,
                yl_ref, yc_ref, ol_ref, oc_ref, olb_ref, ocb_ref):
    h = pl.program_id(1)
    n_l, n_c = ql_ref.shape[1] // CHUNK, qc_ref.shape[1] // CHUNK
    d = ql_ref.shape[2]
    row = lax.broadcasted_iota(jnp.int32, (CHUNK, CHUNK), 0).astype(F32)
    col = lax.broadcasted_iota(jnp.int32, (CHUNK, CHUNK), 1).astype(F32)
    lg_f = jax.nn.log_sigmoid(jnp.full((CHUNK, CHUNK), dec_ref[0, h], F32))
    lg_b = jax.nn.log_sigmoid(jnp.full((CHUNK, CHUNK), dec_ref[1, h], F32))
    dmat = (jnp.where(row >= col, jnp.exp((row - col) * lg_f), 0.0)
            + jnp.where(col >= row, jnp.exp((col - row) * lg_b), 0.0))
    qdec_f, kdec_f, cdec_f = jnp.exp((row + 1.0) * lg_f), jnp.exp((CHUNK - 1.0 - row) * lg_f), jnp.exp(CHUNK * lg_f)
    qdec_b, kdec_b, cdec_b = jnp.exp((CHUNK - row) * lg_b), jnp.exp(row * lg_b), jnp.exp(CHUNK * lg_b)
    scale = d ** -0.5
    nt = (((1,), (1,)), ((), ()))
    tn = (((0,), (0,)), ((), ()))

    def load(refs, c, rotate):
        q_ref, k_ref, v_ref = refs
        rows = pl.ds(pl.multiple_of(c * CHUNK, CHUNK), CHUNK)
        q, k, v = q_ref[0, rows, :], k_ref[0, rows, :], v_ref[0, rows, :]
        if rotate:
            cos, sin = cos_ref[rows, :], sin_ref[rows, :]
            q, k = _rope(q, cos, sin, d // 4), _rope(k, cos, sin, d // 4)
        return q.astype(BF16), k * scale, v.astype(BF16), rows

    def both(refs, of_ref, ob_ref, rotate, n):
        def body(i, carry):
            s_f, s_b = carry
            q, k, v, rows = load(refs, i, rotate)
            att = lax.dot_general(q, k.astype(BF16), nt, preferred_element_type=F32) * dmat
            o = jnp.dot(att.astype(BF16), v, preferred_element_type=F32)
            of_ref[rows, :] = o + qdec_f * jnp.dot(q, s_f.astype(BF16), preferred_element_type=F32)
            s_f = cdec_f * s_f + lax.dot_general((k * kdec_f).astype(BF16), v, tn, preferred_element_type=F32)
            q, k, v, rows = load(refs, n - 1 - i, rotate)
            ob_ref[rows, :] = qdec_b * jnp.dot(q, s_b.astype(BF16), preferred_element_type=F32)
            s_b = cdec_b * s_b + lax.dot_general((k * kdec_b).astype(BF16), v, tn, preferred_element_type=F32)
            return s_f, s_b
        return body

    lat, ctx = (ql_ref, kl_ref, vl_ref), (qc_ref, kc_ref, vc_ref)
    s0 = jnp.zeros((d, d), F32)
    state = lax.fori_loop(0, n_c, both(ctx, oc_ref, ocb_ref, False, n_c), (s0, s0))
    lax.fori_loop(0, n_l, both(lat, ol_ref, olb_ref, True, n_l), state, unroll=2)

    def finish(of_ref, ob_ref, g_ref, y_ref):
        def body(c, carry):
            rows = pl.ds(pl.multiple_of(c * FINISH_ROWS, FINISH_ROWS), FINISH_ROWS)
            g = g_ref[0, rows, :]
            y_ref[0, rows, :] = _head_norm_rows(of_ref[rows, :] + ob_ref[rows, :]) * (g * jax.nn.sigmoid(g))
            return carry
        lax.fori_loop(0, of_ref.shape[0] // FINISH_ROWS, body, 0)

    finish(ol_ref, olb_ref, gl_ref, yl_ref)
    finish(oc_ref, ocb_ref, gc_ref, yc_ref)


def retention_branch(proj_l, proj_c, ret_decay, cos, sin):
    bsz, t_l, _ = proj_l.shape
    t_c = proj_c.shape[1]
    d = BR_W // RET_HEADS
    blk0 = OFF_RET // d

    def col(t, part):
        return pl.BlockSpec((1, t, d), functools.partial(lambda b, h, dec, p: (b, 0, blk0 + p * RET_HEADS + h), p=part))

    table = pl.BlockSpec((t_l, LANE), lambda b, h, dec: (0, 0))
    grid_spec = pltpu.PrefetchScalarGridSpec(
        num_scalar_prefetch=1,
        grid=(bsz, RET_HEADS),
        in_specs=[col(t_l, p) for p in range(4)] + [col(t_c, p) for p in range(4)] + [table, table],
        out_specs=[pl.BlockSpec((1, t_l, d), lambda b, h, dec: (b, 0, h)),
                   pl.BlockSpec((1, t_c, d), lambda b, h, dec: (b, 0, h))],
        scratch_shapes=[pltpu.VMEM((t_l, d), F32), pltpu.VMEM((t_c, d), F32),
                        pltpu.VMEM((t_l, d), F32), pltpu.VMEM((t_c, d), F32)])
    return pl.pallas_call(
        _ret_kernel,
        grid_spec=grid_spec,
        out_shape=[jax.ShapeDtypeStruct((bsz, t_l, BR_W), F32), jax.ShapeDtypeStruct((bsz, t_c, BR_W), F32)],
        compiler_params=_cparams(("parallel", "parallel")),
        name="retention_branch",
    )(ret_decay, proj_l, proj_l, proj_l, proj_l, proj_c, proj_c, proj_c, proj_c, cos, sin)


SHORT_PAD = 8
MLSTM_HP = 1


def _mlstm_kernel(ql_ref, kl_ref, vl_ref, ol_ref, gl_ref, qc_ref, kc_ref, vc_ref, oc_ref, gc_ref,
                  wq_ref, wk_ref, bq_ref, bk_ref, gb_ref, yl_ref, yc_ref,
                  pad_ref, qsl_ref, ksl_ref, qsc_ref, ksc_ref, hl_ref, hc_ref, hlb_ref, hcb_ref):
    head0 = pl.program_id(1) * MLSTM_HP
    t_l, t_c = ql_ref.shape[1], qc_ref.shape[1]
    width = ql_ref.shape[2]
    d = width // MLSTM_HP
    row = lax.broadcasted_iota(jnp.int32, (CHUNK, CHUNK), 0)
    col = lax.broadcasted_iota(jnp.int32, (CHUNK, CHUNK), 1)
    eye = row == col
    scale = d ** -0.5
    nt = (((1,), (1,)), ((), ()))
    tn = (((0,), (0,)), ((), ()))

    def short_conv(x_ref, w_ref, b_ref, out_ref, t):
        k_taps = w_ref.shape[0]
        first = SHORT_PAD - (k_taps - 1) // 2
        pad_ref[0:SHORT_PAD, :] = jnp.zeros((SHORT_PAD, width), F32)
        pad_ref[SHORT_PAD + t:SHORT_PAD + t + SHORT_PAD, :] = jnp.zeros((SHORT_PAD, width), F32)

        def copy(i, carry):
            pad_ref[pl.ds(pl.multiple_of(SHORT_PAD + i * CHUNK, 8), CHUNK), :] = \
                x_ref[0, pl.ds(pl.multiple_of(i * CHUNK, CHUNK), CHUNK), :]
            return carry

        lax.fori_loop(0, t // CHUNK, copy, 0)

        def conv(i, carry):
            base = pl.multiple_of(i * CHUNK, CHUNK)
            win = pad_ref[pl.ds(base, CHUNK + 2 * SHORT_PAD), :]
            acc = jnp.broadcast_to(b_ref[...], (CHUNK, width))
            for k in range(k_taps):
                acc = acc + w_ref[k:k + 1, :] * win[first + k:first + k + CHUNK]
            out_ref[pl.ds(base, CHUNK), :] = acc * jax.nn.sigmoid(acc)
            return carry

        lax.fori_loop(0, t // CHUNK, conv, 0)

    short_conv(ql_ref, wq_ref, bq_ref, qsl_ref, t_l)
    short_conv(kl_ref, wk_ref, bk_ref, ksl_ref, t_l)
    short_conv(qc_ref, wq_ref, bq_ref, qsc_ref, t_c)
    short_conv(kc_ref, wk_ref, bk_ref, ksc_ref, t_c)

    def gate_cols(g_ref, rows, i_lane, f_lane):
        x = g_ref[0, rows, :] + gb_ref[...]
        i_col = jnp.sum(jnp.where(col == i_lane, x, 0.0), axis=1, keepdims=True)
        f_col = jax.nn.log_sigmoid(jnp.sum(jnp.where(col == f_lane, x, 0.0), axis=1, keepdims=True))
        i_row = jnp.sum(jnp.where(eye, i_col, 0.0), axis=0, keepdims=True)
        f_row = jnp.sum(jnp.where(eye, f_col, 0.0), axis=0, keepdims=True)
        return i_col, f_col, i_row, f_row

    def chunk_step(seq, c, state, forward, hk):
        qs_ref, ks_ref, v_ref, g_ref = seq[:4]
        h_ref = seq[4] if forward else seq[5]
        cmat, nvec, m = state
        rows = pl.ds(pl.multiple_of(c * CHUNK, CHUNK), CHUNK)
        lanes = pl.ds(hk * d, d)
        h = head0 + hk
        q = qs_ref[rows, lanes]
        k = ks_ref[rows, lanes] * scale
        qb, vb = q.astype(BF16), v_ref[0, rows, lanes].astype(BF16)
        i_col, f_col, i_row, f_row = gate_cols(g_ref, rows, (0 if forward else 2 * MLSTM_HEADS) + h,
                                               (MLSTM_HEADS if forward else 3 * MLSTM_HEADS) + h)
        seen = (col <= row) if forward else (col >= row)
        seen_t = (row <= col) if forward else (row >= col)
        b_col = jnp.sum(jnp.where(seen, f_row, 0.0), axis=1, keepdims=True)
        b_row = jnp.sum(jnp.where(seen_t, f_col, 0.0), axis=0, keepdims=True)
        b_end = jnp.sum(f_row, axis=1, keepdims=True)
        logw = jnp.where(seen, b_col - b_row + i_row, -jnp.inf)
        inter = b_col + m
        m_t = jnp.maximum(inter, jnp.max(logw, axis=1, keepdims=True))
        s = lax.dot_general(qb, k.astype(BF16), nt, preferred_element_type=F32) * jnp.exp(logw - m_t)
        w_prev = jnp.exp(inter - m_t)
        num = (jnp.dot(s.astype(BF16), vb, preferred_element_type=F32)
               + w_prev * jnp.dot(qb, cmat.astype(BF16), preferred_element_type=F32))
        den = jnp.sum(s, axis=1, keepdims=True) + w_prev * jnp.sum(q * nvec, axis=1, keepdims=True)
        hc = num * (1.0 / jnp.maximum(jnp.abs(den), jnp.exp(-m_t)))
        h_ref[rows, lanes] = hc
        log_wk = b_end - b_col + i_col
        m_new = jnp.maximum(b_end + m, jnp.max(log_wk, axis=0, keepdims=True))
        decay = jnp.exp(b_end + m - m_new)
        kw = k * jnp.exp(log_wk - m_new)
        cmat = decay * cmat + lax.dot_general(kw.astype(BF16), vb, tn, preferred_element_type=F32)
        nvec = decay * nvec + jnp.sum(kw, axis=0, keepdims=True)
        return cmat, nvec, m_new

    lat = (qsl_ref, ksl_ref, vl_ref, gl_ref, hl_ref, hlb_ref)
    ctx = (qsc_ref, ksc_ref, vc_ref, gc_ref, hc_ref, hcb_ref)
    n_l, n_c = t_l // CHUNK, t_c // CHUNK
    state0 = (jnp.zeros((d, d), F32), jnp.zeros((1, d), F32), jnp.full((1, 1), NEG_INIT, F32))

    def both(seq, n):
        def body(i, carry):
            fwd = tuple(chunk_step(seq, i, carry[0][hk], True, hk) for hk in range(MLSTM_HP))
            bwd = tuple(chunk_step(seq, n - 1 - i, carry[1][hk], False, hk) for hk in range(MLSTM_HP))
            return fwd, bwd
        return body

    st = lax.fori_loop(0, n_c, both(ctx, n_c), ((state0,) * MLSTM_HP, (state0,) * MLSTM_HP))
    lax.fori_loop(0, n_l, both(lat, n_l), st, unroll=2)

    def finish(hf_ref, hb_ref, o_ref, y_ref):
        def body(c, carry):
            rows = pl.ds(pl.multiple_of(c * FINISH_ROWS, FINISH_ROWS), FINISH_ROWS)
            for hk in range(MLSTM_HP):
                lanes = pl.ds(hk * d, d)
                y_ref[0, rows, lanes] = (_head_norm_rows(hf_ref[rows, lanes] + hb_ref[rows, lanes])
                                         * jax.nn.sigmoid(o_ref[0, rows, lanes]))
            return carry
        lax.fori_loop(0, hf_ref.shape[0] // FINISH_ROWS, body, 0)

    finish(hl_ref, hlb_ref, ol_ref, yl_ref)
    finish(hc_ref, hcb_ref, oc_ref, yc_ref)


def mlstm_branch(proj_l, proj_c, conv_w, conv_b, gate_b):
    bsz, t_l, _ = proj_l.shape
    t_c = proj_c.shape[1]
    w = MLSTM_HP * (BR_W // MLSTM_HEADS)
    blk0 = OFF_MLSTM // w
    n_grp = MLSTM_HEADS // MLSTM_HP
    k_taps = conv_w.shape[0]

    def col(t, part):
        return pl.BlockSpec((1, t, w), functools.partial(lambda b, h, p: (b, 0, blk0 + p * n_grp + h), p=part))

    def gates(t):
        return pl.BlockSpec((1, t, LANE), lambda b, h: (b, 0, OFF_MGATE // LANE))

    gb = jnp.zeros((1, LANE), F32).at[0, :MLSTM_GATES].set(gate_b.reshape(-1))
    return pl.pallas_call(
        _mlstm_kernel,
        grid=(bsz, n_grp),
        in_specs=[col(t_l, p) for p in range(4)] + [gates(t_l)] + [col(t_c, p) for p in range(4)] + [gates(t_c)] + [
            pl.BlockSpec((k_taps, w), lambda b, h: (0, h)),
            pl.BlockSpec((k_taps, w), lambda b, h: (0, n_grp + h)),
            pl.BlockSpec((1, w), lambda b, h: (0, h)),
            pl.BlockSpec((1, w), lambda b, h: (0, n_grp + h)),
            pl.BlockSpec((1, LANE), lambda b, h: (0, 0))],
        out_specs=[pl.BlockSpec((1, t_l, w), lambda b, h: (b, 0, h)),
                   pl.BlockSpec((1, t_c, w), lambda b, h: (b, 0, h))],
        out_shape=[jax.ShapeDtypeStruct((bsz, t_l, BR_W), F32), jax.ShapeDtypeStruct((bsz, t_c, BR_W), F32)],
        scratch_shapes=[pltpu.VMEM((t_l + 2 * SHORT_PAD, w), F32),
                        pltpu.VMEM((t_l, w), F32), pltpu.VMEM((t_l, w), F32),
                        pltpu.VMEM((t_c, w), F32), pltpu.VMEM((t_c, w), F32),
                        pltpu.VMEM((t_l, w), F32), pltpu.VMEM((t_c, w), F32),
                        pltpu.VMEM((t_l, w), F32), pltpu.VMEM((t_c, w), F32)],
        compiler_params=_cparams(("parallel", "parallel")),
        name="mlstm_branch",
    )(proj_l, proj_l, proj_l, proj_l, proj_l, proj_c, proj_c, proj_c, proj_c, proj_c,
      conv_w, conv_w, conv_b.reshape(1, -1), conv_b.reshape(1, -1), gb)


def _attn_kernel(sink_ref, q_ref, *refs, local):
    if local:
        kp_ref, kc_ref, kn_ref, vp_ref, vc_ref, vn_ref, kx_ref, vx_ref, cos_ref, sin_ref, o_ref, k_scr, v_scr = refs
    else:
        kx_ref, vx_ref, o_ref, k_scr, v_scr = refs
    i = pl.program_id(1)
    nq = pl.num_programs(1)
    t_x = kx_ref.shape[1]
    n_loc = 3 * BLOCK if local else 0
    n_keys = n_loc + t_x
    nf = ATTN_DIM // 4
    half = LANE // 2

    def put(dst, lo, x):
        dst[0, lo:lo + x.shape[0], :] = x.astype(BF16)
        dst[1, lo:lo + x.shape[0], :] = pltpu.roll(x, half, 1).astype(BF16)

    if local:
        for j, (k_ref, v_ref) in enumerate(((kp_ref, vp_ref), (kc_ref, vc_ref), (kn_ref, vn_ref))):
            blk = jnp.clip(i + (j - 1), 0, nq - 1)
            rows = pl.ds(pl.multiple_of(blk * BLOCK, BLOCK), BLOCK)
            put(k_scr, j * BLOCK, _rope(k_ref[0], cos_ref[rows, :], sin_ref[rows, :], nf))
            put(v_scr, j * BLOCK, v_ref[0])
    put(k_scr, n_loc, kx_ref[0])
    put(v_scr, n_loc, vx_ref[0])

    lane = lax.broadcasted_iota(jnp.int32, (BLOCK, LANE), 1)
    if local:
        rel = (lax.broadcasted_iota(jnp.int32, (BLOCK, n_keys), 1)
               - lax.broadcasted_iota(jnp.int32, (BLOCK, n_keys), 0))
        kpos = (i - 1) * BLOCK + lax.broadcasted_iota(jnp.int32, (BLOCK, n_keys), 1)
        ok = (rel >= 0) & (rel <= 2 * WINDOW) & (kpos >= 0) & (kpos < nq * BLOCK)
        ok = ok | (lax.broadcasted_iota(jnp.int32, (BLOCK, n_keys), 1) >= n_loc)
        bias = jnp.where(ok, 0.0, -jnp.inf)
        qrows = pl.ds(pl.multiple_of(i * BLOCK, BLOCK), BLOCK)
        cos_q, sin_q = cos_ref[qrows, :], sin_ref[qrows, :]
    nt = (((1,), (1,)), ((), ()))
    swaps, scores, sinks = [], [], []
    for p in range(ATTN_HEADS // 2):
        qp = q_ref[0, :, p * LANE:(p + 1) * LANE]
        if local:
            qp = _rope(qp, cos_q, sin_q, nf)
        qp = qp * (ATTN_DIM ** -0.5)
        for sub in range(2):
            hd = 2 * p + sub
            kv_half = hd // (ATTN_HEADS // ATTN_KV_HEADS)
            sel = (lane >= half) if sub else (lane < half)
            qm = jnp.where(sel, qp, 0.0).astype(BF16)
            swaps.append(0 if kv_half == sub else 1)
            scores.append(lax.dot_general(qm, k_scr[swaps[-1]], nt, preferred_element_type=F32))
            sinks.append(jnp.full((BLOCK, 1), sink_ref[hd], F32))
    s = jnp.stack(scores)
    sink = jnp.stack(sinks)
    if local:
        s = s + bias[None]
    m = jnp.maximum(jnp.max(s, axis=2, keepdims=True), sink)
    e = jnp.exp(s - m)
    inv = 1.0 / (jnp.sum(e, axis=2, keepdims=True) + jnp.exp(sink - m))
    prob = (e * inv).astype(BF16)
    for p in range(ATTN_HEADS // 2):
        outs = [jnp.dot(prob[2 * p + sub], v_scr[swaps[2 * p + sub]], preferred_element_type=F32)
                for sub in range(2)]
        o_ref[0, :, p * LANE:(p + 1) * LANE] = jnp.where(lane < half, outs[0], outs[1])


def attention_branch(proj_q, proj_c, sink, cos, sin, local):
    bsz, t, _ = proj_q.shape
    t_c = proj_c.shape[1]
    nq = t // BLOCK
    n_qcol = ATTN_HEADS * ATTN_DIM
    q_blk = OFF_ATTN // n_qcol
    k_blk = (OFF_ATTN + n_qcol) // LANE
    v_blk = k_blk + 1
    n_keys = (3 * BLOCK if local else 0) + t_c

    def kv(blk, shift):
        return pl.BlockSpec((1, BLOCK, LANE), lambda b, i, s: (b, jnp.clip(i + shift, 0, nq - 1), blk))

    in_specs = [pl.BlockSpec((1, BLOCK, n_qcol), lambda b, i, s: (b, i, q_blk))]
    args = [proj_q]
    if local:
        in_specs += [kv(k_blk, -1), kv(k_blk, 0), kv(k_blk, 1), kv(v_blk, -1), kv(v_blk, 0), kv(v_blk, 1)]
        args += [proj_q] * 6
    in_specs += [pl.BlockSpec((1, t_c, LANE), lambda b, i, s: (b, 0, k_blk)),
                 pl.BlockSpec((1, t_c, LANE), lambda b, i, s: (b, 0, v_blk))]
    args += [proj_c, proj_c]
    if local:
        in_specs += [pl.BlockSpec((t, LANE), lambda b, i, s: (0, 0))] * 2
        args += [cos, sin]
    grid_spec = pltpu.PrefetchScalarGridSpec(
        num_scalar_prefetch=1,
        grid=(bsz, nq),
        in_specs=in_specs,
        out_specs=pl.BlockSpec((1, BLOCK, n_qcol), lambda b, i, s: (b, i, 0)),
        scratch_shapes=[pltpu.VMEM((2, n_keys, LANE), BF16), pltpu.VMEM((2, n_keys, LANE), BF16)])
    return pl.pallas_call(
        functools.partial(_attn_kernel, local=local),
        grid_spec=grid_spec,
        out_shape=jax.ShapeDtypeStruct((bsz, t, n_qcol), F32),
        compiler_params=_cparams(("parallel", "parallel")),
        name="attention_branch",
    )(sink, *args)


def _pack_w_in(w):
    d = w.shape[0]
    used = w.shape[1] - MLSTM_GATES + LANE
    cols = [w[:, :REF_MGATE_OFF], w[:, REF_GATE_OFF:], w[:, REF_ATTN_OFF:REF_GATE_OFF],
            w[:, REF_MGATE_OFF:REF_ATTN_OFF],
            jnp.zeros((d, _round_up(used, PROJ_TILE_N) - w.shape[1]), w.dtype)]
    return jnp.concatenate(cols, axis=1).astype(BF16)


def kernel(x, c, ctx, c_ctx, mod_w, mod_b, norm1_g, norm2_g, w_in, conv_w, conv_b, conv_ln_g, conv_ln_b,
           ret_decay, mlstm_conv_w, mlstm_conv_b, mlstm_gate_b, attn_sink, w_branch, w_out,
           peer_wq, peer_keys, peer_u, peer_v, final_g):
    bsz, seq, d = x.shape
    assert d == D_MODEL
    depth = mod_w.shape[0]
    cos_r, sin_r = rope_tables(seq, BR_W // RET_HEADS)
    cos_a, sin_a = rope_tables(seq, ATTN_DIM)
    n_cond = _round_up(bsz + 1, 8)
    cond = jnp.concatenate([c, c_ctx[None], jnp.zeros((n_cond - bsz - 1, d), F32)], axis=0)
    for l in range(depth):
        last = l == depth - 1
        mod = mod_vectors(cond, mod_w[l], mod_b[l])
        sh1, sc1, ga1, sh2, sc2, ga2 = (m[:, None, :] for m in jnp.split(mod[:bsz], 6, axis=-1))
        csh1, csc1, cga1, csh2, csc2, cga2 = (jnp.broadcast_to(m[None], (bsz, 1, d))
                                              for m in jnp.split(mod[bsz:bsz + 1], 6, axis=-1))
        w_in_p = _pack_w_in(w_in[l])
        proj_l = norm_matmul(x, norm1_g[l], sh1, sc1, w_in_p)
        proj_c = norm_matmul(ctx, norm1_g[l], csh1, csc1, w_in_p)
        ret_l, ret_c = retention_branch(proj_l, proj_c, ret_decay[l], cos_r, sin_r)
        mls_l, mls_c = mlstm_branch(proj_l, proj_c, mlstm_conv_w[l], mlstm_conv_b[l], mlstm_gate_b[l])
        conv_args = (conv_w[l], conv_b[l], conv_ln_g[l], conv_ln_b[l])
        ys_l = (conv_branch(proj_l, *conv_args), ret_l, mls_l,
                attention_branch(proj_l, proj_c, attn_sink[l], cos_a, sin_a, True))
        if not last:
            ys_c = (conv_branch(proj_c, *conv_args), ret_c, mls_c,
                    attention_branch(proj_c, proj_c, attn_sink[l], cos_a, sin_a, False))
        wb, wo = w_branch[l].astype(BF16), w_out[l].astype(BF16)
        wqt = peer_wq[l].T.astype(BF16)
        u = peer_u[l].astype(BF16)
        vt = peer_v[l].T.astype(BF16)

        def peer(xx, sh, sc, ga, out_gain=None):
            xm, *tabs = peer_prep(xx, norm2_g[l], sh, sc, wqt, peer_keys[l])
            return peer_dense(xm, *tabs, u, vt, xx, ga, out_gain)

        x = merge(ys_l, proj_l, wb, wo, x, ga1)
        x = peer(x, sh2, sc2, ga2, final_g if last else None)
        if not last:
            ctx = merge(ys_c, proj_c, wb, wo, ctx, cga1)
            ctx = peer(ctx, csh2, csc2, cga2)
    return x
```

```python
import functools

import jax
import jax.numpy as jnp
import numpy as np
from jax import lax
from jax.experimental import pallas as pl
from jax.experimental.pallas import tpu as pltpu

F32 = jnp.float32
BF16 = jnp.bfloat16

GRID_W = 64
EPS = 1e-6
N_BRANCH = 4
BR_W = 512
RET_HEADS = 4
MLSTM_HEADS = 4
MLSTM_GATES = 4 * MLSTM_HEADS
ATTN_HEADS = 8
ATTN_KV_HEADS = 2
ATTN_DIM = BR_W // ATTN_HEADS
WINDOW = 128
BLOCK = 128
CHUNK = 128
ROPE_BASE = 10000.0
NEG_INIT = -1e30
PEER_HEADS = 8
PEER_KEYS = 128
PEER_TOPK = 16

CONV_IN = 2 * BR_W
RET_IN = 4 * BR_W
MLSTM_QKVO = 4 * BR_W
ATTN_IN = ATTN_HEADS * ATTN_DIM + 2 * ATTN_KV_HEADS * ATTN_DIM
REF_MGATE_OFF = CONV_IN + RET_IN + MLSTM_QKVO
REF_ATTN_OFF = REF_MGATE_OFF + MLSTM_GATES
REF_GATE_OFF = REF_ATTN_OFF + ATTN_IN
OFF_CONV = 0
OFF_RET = OFF_CONV + CONV_IN
OFF_MLSTM = OFF_RET + RET_IN
OFF_GATE = OFF_MLSTM + MLSTM_QKVO
D_MODEL = 1024
OFF_ATTN = OFF_GATE + N_BRANCH * D_MODEL
OFF_MGATE = OFF_ATTN + ATTN_IN
LANE = 128
BF16_ROWS = 16
PROJ_TILE_N = 1024


def _round_up(a, m):
    return (a + m - 1) // m * m


V7X_VMEM_BYTES = 64 * 1024 * 1024
VMEM_LIMIT = V7X_VMEM_BYTES * 7 // 8


def _cparams(sem):
    return pltpu.CompilerParams(dimension_semantics=sem, vmem_limit_bytes=VMEM_LIMIT)


def _mod_kernel(s_ref, w_ref, b_ref, o_ref):
    s = s_ref[...]
    s = s * jax.nn.sigmoid(s)
    o_ref[...] = jnp.dot(s, w_ref[...], preferred_element_type=F32,
                         precision=lax.Precision.HIGHEST) + b_ref[...]


def mod_vectors(cond, w, b):
    r, d = cond.shape
    n = w.shape[1]
    tn = 512
    return pl.pallas_call(
        _mod_kernel,
        grid=(n // tn,),
        in_specs=[pl.BlockSpec((r, d), lambda j: (0, 0)),
                  pl.BlockSpec((d, tn), lambda j: (0, j)),
                  pl.BlockSpec((1, tn), lambda j: (0, j))],
        out_specs=pl.BlockSpec((r, tn), lambda j: (0, j)),
        out_shape=jax.ShapeDtypeStruct((r, n), F32),
        compiler_params=_cparams(("parallel",)),
        name="mod_vectors",
    )(cond, w, b.reshape(1, n))


def _norm_matmul_kernel(x_ref, g_ref, sh_ref, sc_ref, w_ref, o_ref, h_ref):
    @pl.when(pl.program_id(2) == 0)
    def _():
        x = x_ref[0]
        y = x * lax.rsqrt(jnp.mean(x * x, axis=-1, keepdims=True) + EPS)
        h_ref[...] = ((y * g_ref[...]) * (1.0 + sc_ref[0]) + sh_ref[0]).astype(BF16)

    o_ref[0] = jnp.dot(h_ref[...], w_ref[...], preferred_element_type=F32)


def norm_matmul(x, g, shift, scale, w):
    bsz, t, d = x.shape
    n = w.shape[1]
    tm = min(t, 2048)
    tn = min(n, PROJ_TILE_N)
    return pl.pallas_call(
        _norm_matmul_kernel,
        grid=(bsz, t // tm, n // tn),
        in_specs=[pl.BlockSpec((1, tm, d), lambda b, i, j: (b, i, 0)),
                  pl.BlockSpec((1, d), lambda b, i, j: (0, 0)),
                  pl.BlockSpec((1, 1, d), lambda b, i, j: (b, 0, 0)),
                  pl.BlockSpec((1, 1, d), lambda b, i, j: (b, 0, 0)),
                  pl.BlockSpec((d, tn), lambda b, i, j: (0, j))],
        out_specs=pl.BlockSpec((1, tm, tn), lambda b, i, j: (b, i, j)),
        out_shape=jax.ShapeDtypeStruct((bsz, t, n), F32),
        scratch_shapes=[pltpu.VMEM((tm, d), BF16)],
        compiler_params=_cparams(("parallel", "parallel", "arbitrary")),
        name="norm_matmul",
    )(x, g.reshape(1, d), shift, scale, w)


def _merge_kernel(y0, y1, y2, y3, g0, g1, g2, g3, wb_ref, wo_ref, x_ref, ga_ref, o_ref):
    acc = None
    for i, (y, g) in enumerate(((y0, g0), (y1, g1), (y2, g2), (y3, g3))):
        p = jnp.dot(y[0].astype(BF16), wb_ref[i], preferred_element_type=F32)
        t = jax.nn.sigmoid(g[0]) * p
        acc = t if acc is None else acc + t
    o = jnp.dot(acc.astype(BF16), wo_ref[...], preferred_element_type=F32)
    o_ref[0] = x_ref[0] + ga_ref[0] * o


def merge(ys, proj, w_branch, w_out, x, ga):
    bsz, t, d = x.shape
    tm = min(t, 512)
    gate_blk = OFF_GATE // d
    y_spec = pl.BlockSpec((1, tm, BR_W), lambda b, i: (b, i, 0))
    g_specs = [pl.BlockSpec((1, tm, d), functools.partial(lambda b, i, k: (b, i, gate_blk + k), k=k))
               for k in range(N_BRANCH)]
    return pl.pallas_call(
        _merge_kernel,
        grid=(bsz, t // tm),
        in_specs=[y_spec] * N_BRANCH + g_specs + [
            pl.BlockSpec((N_BRANCH, BR_W, d), lambda b, i: (0, 0, 0)),
            pl.BlockSpec((d, d), lambda b, i: (0, 0)),
            pl.BlockSpec((1, tm, d), lambda b, i: (b, i, 0)),
            pl.BlockSpec((1, 1, d), lambda b, i: (b, 0, 0))],
        out_specs=pl.BlockSpec((1, tm, d), lambda b, i: (b, i, 0)),
        out_shape=jax.ShapeDtypeStruct((bsz, t, d), F32),
        compiler_params=_cparams(("parallel", "parallel")),
        name="merge",
    )(*ys, proj, proj, proj, proj, w_branch, w_out, x, ga)


PEER_TE = 2048
PEER_SUB = 512
PEER_GROUP = 4


def _gelu_exact(x):
    return 0.5 * x * (1.0 + lax.erf(x * float(1.0 / np.sqrt(2.0))))


def _peer_kernel(xm_ref, a_ref, lb_ref, b_ref, r2_ref, u_ref, vt_ref, x_ref, ga_ref, gain_ref, o_ref,
                 acc_ref, w_ref, h_ref, *, final_norm):
    j = pl.program_id(2)

    @pl.when(j == 0)
    def _():
        acc_ref[...] = jnp.zeros_like(acc_ref)

    tn = xm_ref.shape[1]

    def spread(row):
        return jnp.broadcast_to(row, (BF16_ROWS, LANE)).astype(BF16)[None]

    def slabs(words):
        return pltpu.bitcast(words, BF16).reshape(PEER_KEYS // BF16_ROWS, BF16_ROWS, LANE)

    n_sub = PEER_TE // PEER_SUB
    n_slot = tn // LANE
    rows_per_sub = PEER_SUB // PEER_KEYS

    def first_matmul(sub):
        h_ref[sub % 2] = lax.dot_general(u_ref[pl.ds(sub * PEER_SUB, PEER_SUB), :], xm_ref[0],
                                         (((1,), (1,)), ((), ())), preferred_element_type=F32)

    def second_matmul(sub):
        experts = pl.ds(sub * PEER_SUB, PEER_SUB)
        acc_ref[...] += jnp.dot(vt_ref[:, experts], w_ref[experts, :], preferred_element_type=F32)

    def gate_slice(sub, blk):
        lanes = pl.ds(blk * LANE, LANE)
        lo = blk * LANE
        for grp in range(rows_per_sub // PEER_GROUP):
            row0 = sub * rows_per_sub + grp * PEER_GROUP
            first_rows = [pl.ds(j * (PEER_TE // PEER_KEYS) + row0 + a, 1) for a in range(PEER_GROUP)]
            g = [None] * PEER_GROUP
            for h in range(PEER_HEADS):
                r2 = slabs(r2_ref[0, h, :, lanes])
                b = slabs(b_ref[0, h, :, lanes])
                for a in range(PEER_GROUP):
                    limit = spread(lb_ref[0, h, first_rows[a], :][:, lo:lo + LANE])
                    t = jnp.where(r2 < limit, b, 0.0) * spread(a_ref[0, h, first_rows[a], :][:, lo:lo + LANE])
                    g[a] = t if g[a] is None else g[a] + t
            for a in range(PEER_GROUP):
                local = pl.ds((grp * PEER_GROUP + a) * PEER_KEYS, PEER_KEYS)
                act = _gelu_exact(h_ref[sub % 2, local, lanes]).astype(BF16)
                w_ref[pl.ds((row0 + a) * PEER_KEYS, PEER_KEYS), lanes] = g[a].reshape(PEER_KEYS, LANE) * act

    first_matmul(0)
    for sub in range(n_sub):
        if sub + 1 < n_sub:
            first_matmul(sub + 1)
        if sub >= 1:
            second_matmul(sub - 1)
        for k in range(n_slot):
            gate_slice(sub, k)
    second_matmul(n_sub - 1)

    @pl.when(j == pl.num_programs(2) - 1)
    def _():
        y = x_ref[0] + ga_ref[0] * acc_ref[...].T
        if final_norm:
            y = y * lax.rsqrt(jnp.mean(y * y, axis=-1, keepdims=True) + EPS) * gain_ref[...]
        o_ref[0] = y


def peer_dense(xm, a_t, lb_t, b_t, r2_t, u, vt, x, ga, out_gain=None):
    bsz, t, d = x.shape
    final_norm = out_gain is not None
    gain = (out_gain if final_norm else jnp.ones((d,), F32)).reshape(1, d)
    n_exp = u.shape[0]
    tn = min(t, 512)
    tab = pl.BlockSpec((1, PEER_HEADS, PEER_KEYS, tn), lambda b, i, j: (b, 0, 0, i))
    pair = pl.BlockSpec((1, PEER_HEADS, PEER_KEYS // 2, tn), lambda b, i, j: (b, 0, 0, i))
    return pl.pallas_call(
        functools.partial(_peer_kernel, final_norm=final_norm),
        grid=(bsz, t // tn, n_exp // PEER_TE),
        in_specs=[pl.BlockSpec((1, tn, d), lambda b, i, j: (b, i, 0)),
                  tab, tab, pair, pair,
                  pl.BlockSpec((PEER_TE, d), lambda b, i, j: (j, 0)),
                  pl.BlockSpec((d, PEER_TE), lambda b, i, j: (0, j)),
                  pl.BlockSpec((1, tn, d), lambda b, i, j: (b, i, 0)),
                  pl.BlockSpec((1, 1, d), lambda b, i, j: (b, 0, 0)),
                  pl.BlockSpec((1, d), lambda b, i, j: (0, 0))],
        out_specs=pl.BlockSpec((1, tn, d), lambda b, i, j: (b, i, 0)),
        out_shape=jax.ShapeDtypeStruct((bsz, t, d), F32),
        scratch_shapes=[pltpu.VMEM((d, tn), F32), pltpu.VMEM((PEER_TE, tn), BF16),
                        pltpu.VMEM((2, PEER_SUB, tn), F32)],
        compiler_params=_cparams(("parallel", "parallel", "arbitrary")),
        name="peer_dense",
    )(xm, a_t, lb_t, b_t, r2_t, u, vt, x, ga, gain)


def _extract_top(s, row, n_pick, lowest_row_only, with_rank=True):
    rank = jnp.full(s.shape, float(n_pick), F32) if with_rank else None
    tops = []
    for k in range(n_pick):
        m = jnp.max(s, axis=0, keepdims=True)
        hit = s == m
        if lowest_row_only:
            hit = row == jnp.min(jnp.where(hit, row, s.shape[0]), axis=0, keepdims=True)
        if with_rank:
            rank = jnp.where(hit, float(k), rank)
        s = jnp.where(hit, -jnp.inf, s)
        tops.append(m)
    marked = (rank < float(n_pick)) if with_rank else (s == -jnp.inf)
    n_marked = jnp.sum(jnp.where(marked, 1.0, 0.0), axis=0, keepdims=True)
    n_bad = jnp.sum(jnp.where(n_marked != float(n_pick), 1.0, 0.0))
    return rank, jnp.concatenate(tops, axis=0), n_bad


def _peer_prep_kernel(x_ref, g_ref, sh_ref, sc_ref, wqt_ref, keys_ref,
                      xm_ref, a_ref, lb_ref, b_ref, r2_ref, qt_ref):
    tn = x_ref.shape[1]
    x = x_ref[0]
    y = x * lax.rsqrt(jnp.mean(x * x, axis=-1, keepdims=True) + EPS)
    xm = ((y * g_ref[...]) * (1.0 + sc_ref[0]) + sh_ref[0]).astype(BF16)
    xm_ref[0] = xm
    qt_ref[...] = lax.dot_general(wqt_ref[...], xm, (((1,), (1,)), ((), ())), preferred_element_type=F32)
    half = keys_ref.shape[2]
    row = lax.broadcasted_iota(jnp.int32, (PEER_KEYS, LANE), 0)
    sub = lax.broadcasted_iota(jnp.int32, (8, LANE), 0)
    n_cand_rows = PEER_TOPK + 8 * (PEER_TOPK // 2 - 1) + PEER_TOPK // 2
    crow = lax.broadcasted_iota(jnp.int32, (n_cand_rows, LANE), 0)

    def head_tables(h, exact):
        n_bad = 0.0
        for blk in range(tn // LANE):
            lanes = pl.ds(blk * LANE, LANE)
            s, rank, tops = [], [], []
            for c in range(2):
                qs = qt_ref[pl.ds(pl.multiple_of((2 * h + c) * half, half), half), lanes]
                sc_ = jnp.dot(keys_ref[c], qs, preferred_element_type=F32, precision=lax.Precision.HIGHEST)
                r, t, bad = _extract_top(sc_, row, PEER_TOPK, exact, with_rank=exact or c == 1)
                n_bad = n_bad + bad
                s.append(sc_)
                rank.append(r)
                tops.append(t)
            t1, t2 = tops
            groups = [t1[0:1] + t2]
            for a in range(1, PEER_TOPK // 2):
                n_valid = PEER_TOPK // (a + 1)
                grp = t1[a:a + 1] + t2[0:8]
                groups.append(grp if n_valid >= 8 else jnp.where(sub < n_valid, grp, -jnp.inf))
            groups.append(t1[PEER_TOPK // 2:] + t2[0:1])
            cand = jnp.concatenate(groups, axis=0)
            crank, _, bad = _extract_top(cand, crow, PEER_TOPK, exact)
            n_bad = n_bad + bad
            picked = crank < float(PEER_TOPK)
            z = jnp.sum(jnp.where(picked, jnp.exp(cand - cand[0:1]), 0.0), axis=0, keepdims=True)
            pk = picked.astype(F32)
            counts = [jnp.sum(pk[0:PEER_TOPK], axis=0, keepdims=True)]
            for a in range(1, PEER_TOPK // 2):
                lo = PEER_TOPK + 8 * (a - 1)
                counts.append(jnp.sum(pk[lo:lo + 8], axis=0, keepdims=True))
            lo = PEER_TOPK + 8 * (PEER_TOPK // 2 - 1)
            for a in range(PEER_TOPK // 2):
                counts.append(pk[lo + a:lo + a + 1])
            lb = jnp.zeros((PEER_KEYS, LANE), F32)
            for a in range(PEER_TOPK):
                is_pick = (rank[0] == float(a)) if exact else (s[0] == t1[a:a + 1])
                lb = jnp.where(is_pick, counts[a], lb)
            a_ref[0, h, :, lanes] = jnp.exp(s[0] - t1[0:1])
            lb_ref[0, h, :, lanes] = lb
            b_ref[0, h, :, lanes] = pltpu.bitcast((jnp.exp(s[1] - t2[0:1]) / z).astype(BF16), jnp.uint32)
            r2_ref[0, h, :, lanes] = pltpu.bitcast(rank[1].astype(BF16), jnp.uint32)
        return n_bad

    def head_body(h, carry):
        n_bad = head_tables(h, False)

        @pl.when(n_bad > 0.0)
        def _():
            head_tables(h, True)

        return carry

    lax.fori_loop(0, PEER_HEADS, head_body, 0)


def peer_prep(x, g, shift, scale, wqt, keys):
    bsz, t, d = x.shape
    tn = min(t, 512)
    nq = wqt.shape[0]
    tab_shape = jax.ShapeDtypeStruct((bsz, PEER_HEADS, PEER_KEYS, t), F32)
    tab_spec = pl.BlockSpec((1, PEER_HEADS, PEER_KEYS, tn), lambda b, i: (b, 0, 0, i))
    pair_shape = jax.ShapeDtypeStruct((bsz, PEER_HEADS, PEER_KEYS // 2, t), jnp.uint32)
    pair_spec = pl.BlockSpec((1, PEER_HEADS, PEER_KEYS // 2, tn), lambda b, i: (b, 0, 0, i))
    return pl.pallas_call(
        _peer_prep_kernel,
        grid=(bsz, t // tn),
        in_specs=[pl.BlockSpec((1, tn, d), lambda b, i: (b, i, 0)),
                  pl.BlockSpec((1, d), lambda b, i: (0, 0)),
                  pl.BlockSpec((1, 1, d), lambda b, i: (b, 0, 0)),
                  pl.BlockSpec((1, 1, d), lambda b, i: (b, 0, 0)),
                  pl.BlockSpec((nq, d), lambda b, i: (0, 0)),
                  pl.BlockSpec(keys.shape, lambda b, i: (0, 0, 0))],
        out_specs=[pl.BlockSpec((1, tn, d), lambda b, i: (b, i, 0)), tab_spec, tab_spec, pair_spec, pair_spec],
        out_shape=[jax.ShapeDtypeStruct((bsz, t, d), BF16), tab_shape, tab_shape, pair_shape, pair_shape],
        scratch_shapes=[pltpu.VMEM((nq, tn), F32)],
        compiler_params=_cparams(("parallel", "parallel")),
        name="peer_prep",
    )(x, g.reshape(1, d), shift, scale, wqt, keys)


ROW_TILE = 128
CONV_PAD = 16


def _conv_kernel(p_ref, w_ref, b_ref, lg_ref, lb_ref, o_ref, pad_ref, y_ref):
    t, c = o_ref.shape[1], o_ref.shape[2]
    k_taps = w_ref.shape[0]
    first = CONV_PAD - (k_taps - 1) // 2
    lane_w = 256
    pad_ref[0:CONV_PAD, :] = jnp.zeros((CONV_PAD, c), F32)
    pad_ref[CONV_PAD + t:, :] = jnp.zeros((CONV_PAD, c), F32)

    def glu_body(i, carry):
        rows = pl.ds(pl.multiple_of(i * ROW_TILE, ROW_TILE), ROW_TILE)
        a = p_ref[0, rows, 0:c]
        g = p_ref[0, rows, c:2 * c]
        pad_ref[pl.ds(pl.multiple_of(CONV_PAD + i * ROW_TILE, 8), ROW_TILE), :] = a * jax.nn.sigmoid(g)
        return carry

    lax.fori_loop(0, t // ROW_TILE, glu_body, 0)

    def conv_body(i, carry):
        base = pl.multiple_of(i * ROW_TILE, ROW_TILE)
        for lb in range(c // lane_w):
            lanes = pl.ds(lb * lane_w, lane_w)
            win = pad_ref[pl.ds(base, ROW_TILE + 2 * CONV_PAD), lanes]
            acc = jnp.broadcast_to(b_ref[:, lanes], (ROW_TILE, lane_w))
            for k in range(k_taps):
                acc = acc + w_ref[k:k + 1, lanes] * win[first + k:first + k + ROW_TILE]
            y_ref[:, lanes] = acc
        y = y_ref[...]
        yc = y - jnp.mean(y, axis=-1, keepdims=True)
        z = yc * lax.rsqrt(jnp.mean(yc * yc, axis=-1, keepdims=True) + EPS) * lg_ref[...] + lb_ref[...]
        o_ref[0, pl.ds(base, ROW_TILE), :] = z * jax.nn.sigmoid(z)
        return carry

    lax.fori_loop(0, t // ROW_TILE, conv_body, 0)


def conv_branch(proj, w, b, ln_g, ln_b):
    bsz, t, _ = proj.shape
    k_taps, c = w.shape
    vec = pl.BlockSpec((1, c), lambda i: (0, 0))
    return pl.pallas_call(
        _conv_kernel,
        grid=(bsz,),
        in_specs=[pl.BlockSpec((1, t, 2 * c), lambda i: (i, 0, OFF_CONV // (2 * c))),
                  pl.BlockSpec((k_taps, c), lambda i: (0, 0)), vec, vec, vec],
        out_specs=pl.BlockSpec((1, t, c), lambda i: (i, 0, 0)),
        out_shape=jax.ShapeDtypeStruct((bsz, t, c), F32),
        scratch_shapes=[pltpu.VMEM((t + 2 * CONV_PAD, c), F32), pltpu.VMEM((ROW_TILE, c), F32)],
        compiler_params=_cparams(("parallel",)),
        name="conv_branch",
    )(proj, w, b.reshape(1, c), ln_g.reshape(1, c), ln_b.reshape(1, c))


def rope_tables(n_tok, head_dim):
    nf = head_dim // 4
    lane = np.arange(LANE)
    d_idx = lane % head_dim
    by_col = d_idx >= head_dim // 2
    freqs = ROPE_BASE ** (-(d_idx % nf).astype(np.float32) / nf)
    tok = np.arange(n_tok)
    pos = np.where(by_col[None, :], (tok % GRID_W)[:, None], (tok // GRID_W)[:, None]).astype(np.float32)
    ang = jnp.asarray(pos) * jnp.asarray(freqs, F32)[None, :]
    sign = np.where(d_idx % (2 * nf) < nf, -1.0, 1.0).astype(np.float32)
    return jnp.cos(ang), jnp.sin(ang) * sign[None, :]


def _rope(x, cos, sin_signed, nf):
    lane = lax.broadcasted_iota(jnp.int32, x.shape, 1)
    partner = jnp.where(lane % (2 * nf) < nf, pltpu.roll(x, LANE - nf, 1), pltpu.roll(x, nf, 1))
    return x * cos + partner * sin_signed


FINISH_ROWS = 256


def _head_norm_rows(o):
    oc = o - jnp.mean(o, axis=-1, keepdims=True)
    return oc * lax.rsqrt(jnp.mean(oc * oc, axis=-1, keepdims=True) + EPS)


def _ret_kernel(dec_ref, ql_ref, kl_ref, vl_ref, gl_ref, qc_ref, kc_ref, vc_ref, gc_ref, cos_ref, sin_ref,
                yl_ref, yc_ref, ol_ref, oc_ref, olb_ref, ocb_ref):
    h = pl.program_id(1)
    n_l, n_c = ql_ref.shape[1] // CHUNK, qc_ref.shape[1] // CHUNK
    d = ql_ref.shape[2]
    row = lax.broadcasted_iota(jnp.int32, (CHUNK, CHUNK), 0).astype(F32)
    col = lax.broadcasted_iota(jnp.int32, (CHUNK, CHUNK), 1).astype(F32)
    lg_f = jax.nn.log_sigmoid(jnp.full((CHUNK, CHUNK), dec_ref[0, h], F32))
    lg_b = jax.nn.log_sigmoid(jnp.full((CHUNK, CHUNK), dec_ref[1, h], F32))
    dmat = (jnp.where(row >= col, jnp.exp((row - col) * lg_f), 0.0)
            + jnp.where(col >= row, jnp.exp((col - row) * lg_b), 0.0))
    qdec_f, kdec_f, cdec_f = jnp.exp((row + 1.0) * lg_f), jnp.exp((CHUNK - 1.0 - row) * lg_f), jnp.exp(CHUNK * lg_f)
    qdec_b, kdec_b, cdec_b = jnp.exp((CHUNK - row) * lg_b), jnp.exp(row * lg_b), jnp.exp(CHUNK * lg_b)
    scale = d ** -0.5
    nt = (((1,), (1,)), ((), ()))
    tn = (((0,), (0,)), ((), ()))

    def load(refs, c, rotate):
        q_ref, k_ref, v_ref = refs
        rows = pl.ds(pl.multiple_of(c * CHUNK, CHUNK), CHUNK)
        q, k, v = q_ref[0, rows, :], k_ref[0, rows, :], v_ref[0, rows, :]
        if rotate:
            cos, sin = cos_ref[rows, :], sin_ref[rows, :]
            q, k = _rope(q, cos, sin, d // 4), _rope(k, cos, sin, d // 4)
        return q.astype(BF16), k * scale, v.astype(BF16), rows

    def both(refs, of_ref, ob_ref, rotate, n):
        def body(i, carry):
            s_f, s_b = carry
            q, k, v, rows = load(refs, i, rotate)
            att = lax.dot_general(q, k.astype(BF16), nt, preferred_element_type=F32) * dmat
            o = jnp.dot(att.astype(BF16), v, preferred_element_type=F32)
            of_ref[rows, :] = o + qdec_f * jnp.dot(q, s_f.astype(BF16), preferred_element_type=F32)
            s_f = cdec_f * s_f + lax.dot_general((k * kdec_f).astype(BF16), v, tn, preferred_element_type=F32)
            q, k, v, rows = load(refs, n - 1 - i, rotate)
            ob_ref[rows, :] = qdec_b * jnp.dot(q, s_b.astype(BF16), preferred_element_type=F32)
            s_b = cdec_b * s_b + lax.dot_general((k * kdec_b).astype(BF16), v, tn, preferred_element_type=F32)
            return s_f, s_b
        return body

    lat, ctx = (ql_ref, kl_ref, vl_ref), (qc_ref, kc_ref, vc_ref)
    s0 = jnp.zeros((d, d), F32)
    state = lax.fori_loop(0, n_c, both(ctx, oc_ref, ocb_ref, False, n_c), (s0, s0))
    lax.fori_loop(0, n_l, both(lat, ol_ref, olb_ref, True, n_l), state, unroll=2)

    def finish(of_ref, ob_ref, g_ref, y_ref):
        def body(c, carry):
            rows = pl.ds(pl.multiple_of(c * FINISH_ROWS, FINISH_ROWS), FINISH_ROWS)
            g = g_ref[0, rows, :]
            y_ref[0, rows, :] = _head_norm_rows(of_ref[rows, :] + ob_ref[rows, :]) * (g * jax.nn.sigmoid(g))
            return carry
        lax.fori_loop(0, of_ref.shape[0] // FINISH_ROWS, body, 0)

    finish(ol_ref, olb_ref, gl_ref, yl_ref)
    finish(oc_ref, ocb_ref, gc_ref, yc_ref)


def retention_branch(proj_l, proj_c, ret_decay, cos, sin):
    bsz, t_l, _ = proj_l.shape
    t_c = proj_c.shape[1]
    d = BR_W // RET_HEADS
    blk0 = OFF_RET // d

    def col(t, part):
        return pl.BlockSpec((1, t, d), functools.partial(lambda b, h, dec, p: (b, 0, blk0 + p * RET_HEADS + h), p=part))

    table = pl.BlockSpec((t_l, LANE), lambda b, h, dec: (0, 0))
    grid_spec = pltpu.PrefetchScalarGridSpec(
        num_scalar_prefetch=1,
        grid=(bsz, RET_HEADS),
        in_specs=[col(t_l, p) for p in range(4)] + [col(t_c, p) for p in range(4)] + [table, table],
        out_specs=[pl.BlockSpec((1, t_l, d), lambda b, h, dec: (b, 0, h)),
                   pl.BlockSpec((1, t_c, d), lambda b, h, dec: (b, 0, h))],
        scratch_shapes=[pltpu.VMEM((t_l, d), F32), pltpu.VMEM((t_c, d), F32),
                        pltpu.VMEM((t_l, d), F32), pltpu.VMEM((t_c, d), F32)])
    return pl.pallas_call(
        _ret_kernel,
        grid_spec=grid_spec,
        out_shape=[jax.ShapeDtypeStruct((bsz, t_l, BR_W), F32), jax.ShapeDtypeStruct((bsz, t_c, BR_W), F32)],
        compiler_params=_cparams(("parallel", "parallel")),
        name="retention_branch",
    )(ret_decay, proj_l, proj_l, proj_l, proj_l, proj_c, proj_c, proj_c, proj_c, cos, sin)


SHORT_PAD = 8
MLSTM_HP = 1


def _mlstm_kernel(ql_ref, kl_ref, vl_ref, ol_ref, gl_ref, qc_ref, kc_ref, vc_ref, oc_ref, gc_ref,
                  wq_ref, wk_ref, bq_ref, bk_ref, gb_ref, yl_ref, yc_ref,
                  pad_ref, qsl_ref, ksl_ref, qsc_ref, ksc_ref, hl_ref, hc_ref, hlb_ref, hcb_ref):
    head0 = pl.program_id(1) * MLSTM_HP
    t_l, t_c = ql_ref.shape[1], qc_ref.shape[1]
    width = ql_ref.shape[2]
    d = width // MLSTM_HP
    row = lax.broadcasted_iota(jnp.int32, (CHUNK, CHUNK), 0)
    col = lax.broadcasted_iota(jnp.int32, (CHUNK, CHUNK), 1)
    eye = row == col
    scale = d ** -0.5
    nt = (((1,), (1,)), ((), ()))
    tn = (((0,), (0,)), ((), ()))

    def short_conv(x_ref, w_ref, b_ref, out_ref, t):
        k_taps = w_ref.shape[0]
        first = SHORT_PAD - (k_taps - 1) // 2
        pad_ref[0:SHORT_PAD, :] = jnp.zeros((SHORT_PAD, width), F32)
        pad_ref[SHORT_PAD + t:SHORT_PAD + t + SHORT_PAD, :] = jnp.zeros((SHORT_PAD, width), F32)

        def copy(i, carry):
            pad_ref[pl.ds(pl.multiple_of(SHORT_PAD + i * CHUNK, 8), CHUNK), :] = \
                x_ref[0, pl.ds(pl.multiple_of(i * CHUNK, CHUNK), CHUNK), :]
            return carry

        lax.fori_loop(0, t // CHUNK, copy, 0)

        def conv(i, carry):
            base = pl.multiple_of(i * CHUNK, CHUNK)
            win = pad_ref[pl.ds(base, CHUNK + 2 * SHORT_PAD), :]
            acc = jnp.broadcast_to(b_ref[...], (CHUNK, width))
            for k in range(k_taps):
                acc = acc + w_ref[k:k + 1, :] * win[first + k:first + k + CHUNK]
            out_ref[pl.ds(base, CHUNK), :] = acc * jax.nn.sigmoid(acc)
            return carry

        lax.fori_loop(0, t // CHUNK, conv, 0)

    short_conv(ql_ref, wq_ref, bq_ref, qsl_ref, t_l)
    short_conv(kl_ref, wk_ref, bk_ref, ksl_ref, t_l)
    short_conv(qc_ref, wq_ref, bq_ref, qsc_ref, t_c)
    short_conv(kc_ref, wk_ref, bk_ref, ksc_ref, t_c)

    def gate_cols(g_ref, rows, i_lane, f_lane):
        x = g_ref[0, rows, :] + gb_ref[...]
        i_col = jnp.sum(jnp.where(col == i_lane, x, 0.0), axis=1, keepdims=True)
        f_col = jax.nn.log_sigmoid(jnp.sum(jnp.where(col == f_lane, x, 0.0), axis=1, keepdims=True))
        i_row = jnp.sum(jnp.where(eye, i_col, 0.0), axis=0, keepdims=True)
        f_row = jnp.sum(jnp.where(eye, f_col, 0.0), axis=0, keepdims=True)
        return i_col, f_col, i_row, f_row

    def chunk_step(seq, c, state, forward, hk):
        qs_ref, ks_ref, v_ref, g_ref = seq[:4]
        h_ref = seq[4] if forward else seq[5]
        cmat, nvec, m = state
        rows = pl.ds(pl.multiple_of(c * CHUNK, CHUNK), CHUNK)
        lanes = pl.ds(hk * d, d)
        h = head0 + hk
        q = qs_ref[rows, lanes]
        k = ks_ref[rows, lanes] * scale
        qb, vb = q.astype(BF16), v_ref[0, rows, lanes].astype(BF16)
        i_col, f_col, i_row, f_row = gate_cols(g_ref, rows, (0 if forward else 2 * MLSTM_HEADS) + h,
                                               (MLSTM_HEADS if forward else 3 * MLSTM_HEADS) + h)
        seen = (col <= row) if forward else (col >= row)
        seen_t = (row <= col) if forward else (row >= col)
        b_col = jnp.sum(jnp.where(seen, f_row, 0.0), axis=1, keepdims=True)
        b_row = jnp.sum(jnp.where(seen_t, f_col, 0.0), axis=0, keepdims=True)
        b_end = jnp.sum(f_row, axis=1, keepdims=True)
        logw = jnp.where(seen, b_col - b_row + i_row, -jnp.inf)
        inter = b_col + m
        m_t = jnp.maximum(inter, jnp.max(logw, axis=1, keepdims=True))
        s = lax.dot_general(qb, k.astype(BF16), nt, preferred_element_type=F32) * jnp.exp(logw - m_t)
        w_prev = jnp.exp(inter - m_t)
        num = (jnp.dot(s.astype(BF16), vb, preferred_element_type=F32)
               + w_prev * jnp.dot(qb, cmat.astype(BF16), preferred_element_type=F32))
        den = jnp.sum(s, axis=1, keepdims=True) + w_prev * jnp.sum(q * nvec, axis=1, keepdims=True)
        hc = num * (1.0 / jnp.maximum(jnp.abs(den), jnp.exp(-m_t)))
        h_ref[rows, lanes] = hc
        log_wk = b_end - b_col + i_col
        m_new = jnp.maximum(b_end + m, jnp.max(log_wk, axis=0, keepdims=True))
        decay = jnp.exp(b_end + m - m_new)
        kw = k * jnp.exp(log_wk - m_new)
        cmat = decay * cmat + lax.dot_general(kw.astype(BF16), vb, tn, preferred_element_type=F32)
        nvec = decay * nvec + jnp.sum(kw, axis=0, keepdims=True)
        return cmat, nvec, m_new

    lat = (qsl_ref, ksl_ref, vl_ref, gl_ref, hl_ref, hlb_ref)
    ctx = (qsc_ref, ksc_ref, vc_ref, gc_ref, hc_ref, hcb_ref)
    n_l, n_c = t_l // CHUNK, t_c // CHUNK
    state0 = (jnp.zeros((d, d), F32), jnp.zeros((1, d), F32), jnp.full((1, 1), NEG_INIT, F32))

    def both(seq, n):
        def body(i, carry):
            fwd = tuple(chunk_step(seq, i, carry[0][hk], True, hk) for hk in range(MLSTM_HP))
            bwd = tuple(chunk_step(seq, n - 1 - i, carry[1][hk], False, hk) for hk in range(MLSTM_HP))
            return fwd, bwd
        return body

    st = lax.fori_loop(0, n_c, both(ctx, n_c), ((state0,) * MLSTM_HP, (state0,) * MLSTM_HP))
    lax.fori_loop(0, n_l, both(lat, n_l), st, unroll=2)

    def finish(hf_ref, hb_ref, o_ref, y_ref):
        def body(c, carry):
            rows = pl.ds(pl.multiple_of(c * FINISH_ROWS, FINISH_ROWS), FINISH_ROWS)
            for hk in range(MLSTM_HP):
                lanes = pl.ds(hk * d, d)
                y_ref[0, rows, lanes] = (_head_norm_rows(hf_ref[rows, lanes] + hb_ref[rows, lanes])
                                         * jax.nn.sigmoid(o_ref[0, rows, lanes]))
            return carry
        lax.fori_loop(0, hf_ref.shape[0] // FINISH_ROWS, body, 0)

    finish(hl_ref, hlb_ref, ol_ref, yl_ref)
    finish(hc_ref, hcb_ref, oc_ref, yc_ref)


def mlstm_branch(proj_l, proj_c, conv_w, conv_b, gate_b):
    bsz, t_l, _ = proj_l.shape
    t_c = proj_c.shape[1]
    w = MLSTM_HP * (BR_W // MLSTM_HEADS)
    blk0 = OFF_MLSTM // w
    n_grp = MLSTM_HEADS // MLSTM_HP
    k_taps = conv_w.shape[0]

    def col(t, part):
        return pl.BlockSpec((1, t, w), functools.partial(lambda b, h, p: (b, 0, blk0 + p * n_grp + h), p=part))

    def gates(t):
        return pl.BlockSpec((1, t, LANE), lambda b, h: (b, 0, OFF_MGATE // LANE))

    gb = jnp.zeros((1, LANE), F32).at[0, :MLSTM_GATES].set(gate_b.reshape(-1))
    return pl.pallas_call(
        _mlstm_kernel,
        grid=(bsz, n_grp),
        in_specs=[col(t_l, p) for p in range(4)] + [gates(t_l)] + [col(t_c, p) for p in range(4)] + [gates(t_c)] + [
            pl.BlockSpec((k_taps, w), lambda b, h: (0, h)),
            pl.BlockSpec((k_taps, w), lambda b, h: (0, n_grp + h)),
            pl.BlockSpec((1, w), lambda b, h: (0, h)),
            pl.BlockSpec((1, w), lambda b, h: (0, n_grp + h)),
            pl.BlockSpec((1, LANE), lambda b, h: (0, 0))],
        out_specs=[pl.BlockSpec((1, t_l, w), lambda b, h: (b, 0, h)),
                   pl.BlockSpec((1, t_c, w), lambda b, h: (b, 0, h))],
        out_shape=[jax.ShapeDtypeStruct((bsz, t_l, BR_W), F32), jax.ShapeDtypeStruct((bsz, t_c, BR_W), F32)],
        scratch_shapes=[pltpu.VMEM((t_l + 2 * SHORT_PAD, w), F32),
                        pltpu.VMEM((t_l, w), F32), pltpu.VMEM((t_l, w), F32),
                        pltpu.VMEM((t_c, w), F32), pltpu.VMEM((t_c, w), F32),
                        pltpu.VMEM((t_l, w), F32), pltpu.VMEM((t_c, w), F32),
                        pltpu.VMEM((t_l, w), F32), pltpu.VMEM((t_c, w), F32)],
        compiler_params=_cparams(("parallel", "parallel")),
        name="mlstm_branch",
    )(proj_l, proj_l, proj_l, proj_l, proj_l, proj_c, proj_c, proj_c, proj_c, proj_c,
      conv_w, conv_w, conv_b.reshape(1, -1), conv_b.reshape(1, -1), gb)


def _attn_kernel(sink_ref, q_ref, *refs, local):
    if local:
        kp_ref, kc_ref, kn_ref, vp_ref, vc_ref, vn_ref, kx_ref, vx_ref, cos_ref, sin_ref, o_ref, k_scr, v_scr = refs
    else:
        kx_ref, vx_ref, o_ref, k_scr, v_scr = refs
    i = pl.program_id(1)
    nq = pl.num_programs(1)
    t_x = kx_ref.shape[1]
    n_loc = 3 * BLOCK if local else 0
    n_keys = n_loc + t_x
    nf = ATTN_DIM // 4
    half = LANE // 2

    def put(dst, lo, x):
        dst[0, lo:lo + x.shape[0], :] = x.astype(BF16)
        dst[1, lo:lo + x.shape[0], :] = pltpu.roll(x, half, 1).astype(BF16)

    if local:
        for j, (k_ref, v_ref) in enumerate(((kp_ref, vp_ref), (kc_ref, vc_ref), (kn_ref, vn_ref))):
            blk = jnp.clip(i + (j - 1), 0, nq - 1)
            rows = pl.ds(pl.multiple_of(blk * BLOCK, BLOCK), BLOCK)
            put(k_scr, j * BLOCK, _rope(k_ref[0], cos_ref[rows, :], sin_ref[rows, :], nf))
            put(v_scr, j * BLOCK, v_ref[0])
    put(k_scr, n_loc, kx_ref[0])
    put(v_scr, n_loc, vx_ref[0])

    lane = lax.broadcasted_iota(jnp.int32, (BLOCK, LANE), 1)
    if local:
        rel = (lax.broadcasted_iota(jnp.int32, (BLOCK, n_keys), 1)
               - lax.broadcasted_iota(jnp.int32, (BLOCK, n_keys), 0))
        kpos = (i - 1) * BLOCK + lax.broadcasted_iota(jnp.int32, (BLOCK, n_keys), 1)
        ok = (rel >= 0) & (rel <= 2 * WINDOW) & (kpos >= 0) & (kpos < nq * BLOCK)
        ok = ok | (lax.broadcasted_iota(jnp.int32, (BLOCK, n_keys), 1) >= n_loc)
        bias = jnp.where(ok, 0.0, -jnp.inf)
        qrows = pl.ds(pl.multiple_of(i * BLOCK, BLOCK), BLOCK)
        cos_q, sin_q = cos_ref[qrows, :], sin_ref[qrows, :]
    nt = (((1,), (1,)), ((), ()))
    swaps, scores, sinks = [], [], []
    for p in range(ATTN_HEADS // 2):
        qp = q_ref[0, :, p * LANE:(p + 1) * LANE]
        if local:
            qp = _rope(qp, cos_q, sin_q, nf)
        qp = qp * (ATTN_DIM ** -0.5)
        for sub in range(2):
            hd = 2 * p + sub
            kv_half = hd // (ATTN_HEADS // ATTN_KV_HEADS)
            sel = (lane >= half) if sub else (lane < half)
            qm = jnp.where(sel, qp, 0.0).astype(BF16)
            swaps.append(0 if kv_half == sub else 1)
            scores.append(lax.dot_general(qm, k_scr[swaps[-1]], nt, preferred_element_type=F32))
            sinks.append(jnp.full((BLOCK, 1), sink_ref[hd], F32))
    s = jnp.stack(scores)
    sink = jnp.stack(sinks)
    if local:
        s = s + bias[None]
    m = jnp.maximum(jnp.max(s, axis=2, keepdims=True), sink)
    e = jnp.exp(s - m)
    inv = 1.0 / (jnp.sum(e, axis=2, keepdims=True) + jnp.exp(sink - m))
    prob = (e * inv).astype(BF16)
    for p in range(ATTN_HEADS // 2):
        outs = [jnp.dot(prob[2 * p + sub], v_scr[swaps[2 * p + sub]], preferred_element_type=F32)
                for sub in range(2)]
        o_ref[0, :, p * LANE:(p + 1) * LANE] = jnp.where(lane < half, outs[0], outs[1])


def attention_branch(proj_q, proj_c, sink, cos, sin, local):
    bsz, t, _ = proj_q.shape
    t_c = proj_c.shape[1]
    nq = t // BLOCK
    n_qcol = ATTN_HEADS * ATTN_DIM
    q_blk = OFF_ATTN // n_qcol
    k_blk = (OFF_ATTN + n_qcol) // LANE
    v_blk = k_blk + 1
    n_keys = (3 * BLOCK if local else 0) + t_c

    def kv(blk, shift):
        return pl.BlockSpec((1, BLOCK, LANE), lambda b, i, s: (b, jnp.clip(i + shift, 0, nq - 1), blk))

    in_specs = [pl.BlockSpec((1, BLOCK, n_qcol), lambda b, i, s: (b, i, q_blk))]
    args = [proj_q]
    if local:
        in_specs += [kv(k_blk, -1), kv(k_blk, 0), kv(k_blk, 1), kv(v_blk, -1), kv(v_blk, 0), kv(v_blk, 1)]
        args += [proj_q] * 6
    in_specs += [pl.BlockSpec((1, t_c, LANE), lambda b, i, s: (b, 0, k_blk)),
                 pl.BlockSpec((1, t_c, LANE), lambda b, i, s: (b, 0, v_blk))]
    args += [proj_c, proj_c]
    if local:
        in_specs += [pl.BlockSpec((t, LANE), lambda b, i, s: (0, 0))] * 2
        args += [cos, sin]
    grid_spec = pltpu.PrefetchScalarGridSpec(
        num_scalar_prefetch=1,
        grid=(bsz, nq),
        in_specs=in_specs,
        out_specs=pl.BlockSpec((1, BLOCK, n_qcol), lambda b, i, s: (b, i, 0)),
        scratch_shapes=[pltpu.VMEM((2, n_keys, LANE), BF16), pltpu.VMEM((2, n_keys, LANE), BF16)])
    return pl.pallas_call(
        functools.partial(_attn_kernel, local=local),
        grid_spec=grid_spec,
        out_shape=jax.ShapeDtypeStruct((bsz, t, n_qcol), F32),
        compiler_params=_cparams(("parallel", "parallel")),
        name="attention_branch",
    )(sink, *args)


def _pack_w_in(w):
    d = w.shape[0]
    used = w.shape[1] - MLSTM_GATES + LANE
    cols = [w[:, :REF_MGATE_OFF], w[:, REF_GATE_OFF:], w[:, REF_ATTN_OFF:REF_GATE_OFF],
            w[:, REF_MGATE_OFF:REF_ATTN_OFF],
            jnp.zeros((d, _round_up(used, PROJ_TILE_N) - w.shape[1]), w.dtype)]
    return jnp.concatenate(cols, axis=1).astype(BF16)


def kernel(x, c, ctx, c_ctx, mod_w, mod_b, norm1_g, norm2_g, w_in, conv_w, conv_b, conv_ln_g, conv_ln_b,
           ret_decay, mlstm_conv_w, mlstm_conv_b, mlstm_gate_b, attn_sink, w_branch, w_out,
           peer_wq, peer_keys, peer_u, peer_v, final_g):
    bsz, seq, d = x.shape
    assert d == D_MODEL
    depth = mod_w.shape[0]
    cos_r, sin_r = rope_tables(seq, BR_W // RET_HEADS)
    cos_a, sin_a = rope_tables(seq, ATTN_DIM)
    n_cond = _round_up(bsz + 1, 8)
    cond = jnp.concatenate([c, c_ctx[None], jnp.zeros((n_cond - bsz - 1, d), F32)], axis=0)
    for l in range(depth):
        last = l == depth - 1
        mod = mod_vectors(cond, mod_w[l], mod_b[l])
        sh1, sc1, ga1, sh2, sc2, ga2 = (m[:, None, :] for m in jnp.split(mod[:bsz], 6, axis=-1))
        csh1, csc1, cga1, csh2, csc2, cga2 = (jnp.broadcast_to(m[None], (bsz, 1, d))
                                              for m in jnp.split(mod[bsz:bsz + 1], 6, axis=-1))
        w_in_p = _pack_w_in(w_in[l])
        proj_l = norm_matmul(x, norm1_g[l], sh1, sc1, w_in_p)
        proj_c = norm_matmul(ctx, norm1_g[l], csh1, csc1, w_in_p)
        ret_l, ret_c = retention_branch(proj_l, proj_c, ret_decay[l], cos_r, sin_r)
        mls_l, mls_c = mlstm_branch(proj_l, proj_c, mlstm_conv_w[l], mlstm_conv_b[l], mlstm_gate_b[l])
        conv_args = (conv_w[l], conv_b[l], conv_ln_g[l], conv_ln_b[l])
        ys_l = (conv_branch(proj_l, *conv_args), ret_l, mls_l,
                attention_branch(proj_l, proj_c, attn_sink[l], cos_a, sin_a, True))
        if not last:
            ys_c = (conv_branch(proj_c, *conv_args), ret_c, mls_c,
                    attention_branch(proj_c, proj_c, attn_sink[l], cos_a, sin_a, False))
        wb, wo = w_branch[l].astype(BF16), w_out[l].astype(BF16)
        wqt = peer_wq[l].T.astype(BF16)
        u = peer_u[l].astype(BF16)
        vt = peer_v[l].T.astype(BF16)

        def peer(xx, sh, sc, ga, out_gain=None):
            xm, *tabs = peer_prep(xx, norm2_g[l], sh, sc, wqt, peer_keys[l])
            return peer_dense(xm, *tabs, u, vt, xx, ga, out_gain)

        x = merge(ys_l, proj_l, wb, wo, x, ga1)
        x = peer(x, sh2, sc2, ga2, final_g if last else None)
        if not last:
            ctx = merge(ys_c, proj_c, wb, wo, ctx, cga1)
            ctx = peer(ctx, csh2, csc2, cga2)
    return x
```

```python
import functools

import jax
import jax.numpy as jnp
import numpy as np
from jax import lax
from jax.experimental import pallas as pl
from jax.experimental.pallas import tpu as pltpu

F32 = jnp.float32
BF16 = jnp.bfloat16

GRID_W = 64
EPS = 1e-6
N_BRANCH = 4
BR_W = 512
RET_HEADS = 4
MLSTM_HEADS = 4
MLSTM_GATES = 4 * MLSTM_HEADS
ATTN_HEADS = 8
ATTN_KV_HEADS = 2
ATTN_DIM = BR_W // ATTN_HEADS
WINDOW = 128
BLOCK = 128
CHUNK = 128
ROPE_BASE = 10000.0
NEG_INIT = -1e30
PEER_HEADS = 8
PEER_KEYS = 128
PEER_TOPK = 16

CONV_IN = 2 * BR_W
RET_IN = 4 * BR_W
MLSTM_QKVO = 4 * BR_W
ATTN_IN = ATTN_HEADS * ATTN_DIM + 2 * ATTN_KV_HEADS * ATTN_DIM
REF_MGATE_OFF = CONV_IN + RET_IN + MLSTM_QKVO
REF_ATTN_OFF = REF_MGATE_OFF + MLSTM_GATES
REF_GATE_OFF = REF_ATTN_OFF + ATTN_IN
OFF_CONV = 0
OFF_RET = OFF_CONV + CONV_IN
OFF_MLSTM = OFF_RET + RET_IN
OFF_GATE = OFF_MLSTM + MLSTM_QKVO
D_MODEL = 1024
OFF_ATTN = OFF_GATE + N_BRANCH * D_MODEL
OFF_MGATE = OFF_ATTN + ATTN_IN
LANE = 128
BF16_ROWS = 16
PROJ_TILE_N = 1024


def _round_up(a, m):
    return (a + m - 1) // m * m


V7X_VMEM_BYTES = 64 * 1024 * 1024
VMEM_LIMIT = V7X_VMEM_BYTES * 7 // 8


def _cparams(sem):
    return pltpu.CompilerParams(dimension_semantics=sem, vmem_limit_bytes=VMEM_LIMIT)


def _mod_kernel(s_ref, w_ref, b_ref, o_ref):
    s = s_ref[...]
    s = s * jax.nn.sigmoid(s)
    o_ref[...] = jnp.dot(s, w_ref[...], preferred_element_type=F32,
                         precision=lax.Precision.HIGHEST) + b_ref[...]


def mod_vectors(cond, w, b):
    r, d = cond.shape
    n = w.shape[1]
    tn = 512
    return pl.pallas_call(
        _mod_kernel,
        grid=(n // tn,),
        in_specs=[pl.BlockSpec((r, d), lambda j: (0, 0)),
                  pl.BlockSpec((d, tn), lambda j: (0, j)),
                  pl.BlockSpec((1, tn), lambda j: (0, j))],
        out_specs=pl.BlockSpec((r, tn), lambda j: (0, j)),
        out_shape=jax.ShapeDtypeStruct((r, n), F32),
        compiler_params=_cparams(("parallel",)),
        name="mod_vectors",
    )(cond, w, b.reshape(1, n))


def _norm_matmul_kernel(x_ref, g_ref, sh_ref, sc_ref, w_ref, o_ref, h_ref):
    @pl.when(pl.program_id(2) == 0)
    def _():
        x = x_ref[0]
        y = x * lax.rsqrt(jnp.mean(x * x, axis=-1, keepdims=True) + EPS)
        h_ref[...] = ((y * g_ref[...]) * (1.0 + sc_ref[0]) + sh_ref[0]).astype(BF16)

    o_ref[0] = jnp.dot(h_ref[...], w_ref[...], preferred_element_type=F32)


def norm_matmul(x, g, shift, scale, w):
    bsz, t, d = x.shape
    n = w.shape[1]
    tm = min(t, 2048)
    tn = min(n, PROJ_TILE_N)
    return pl.pallas_call(
        _norm_matmul_kernel,
        grid=(bsz, t // tm, n // tn),
        in_specs=[pl.BlockSpec((1, tm, d), lambda b, i, j: (b, i, 0)),
                  pl.BlockSpec((1, d), lambda b, i, j: (0, 0)),
                  pl.BlockSpec((1, 1, d), lambda b, i, j: (b, 0, 0)),
                  pl.BlockSpec((1, 1, d), lambda b, i, j: (b, 0, 0)),
                  pl.BlockSpec((d, tn), lambda b, i, j: (0, j))],
        out_specs=pl.BlockSpec((1, tm, tn), lambda b, i, j: (b, i, j)),
        out_shape=jax.ShapeDtypeStruct((bsz, t, n), F32),
        scratch_shapes=[pltpu.VMEM((tm, d), BF16)],
        compiler_params=_cparams(("parallel", "parallel", "arbitrary")),
        name="norm_matmul",
    )(x, g.reshape(1, d), shift, scale, w)


def _merge_kernel(y0, y1, y2, y3, g0, g1, g2, g3, wb_ref, wo_ref, x_ref, ga_ref, o_ref):
    acc = None
    for i, (y, g) in enumerate(((y0, g0), (y1, g1), (y2, g2), (y3, g3))):
        p = jnp.dot(y[0].astype(BF16), wb_ref[i], preferred_element_type=F32)
        t = jax.nn.sigmoid(g[0]) * p
        acc = t if acc is None else acc + t
    o = jnp.dot(acc.astype(BF16), wo_ref[...], preferred_element_type=F32)
    o_ref[0] = x_ref[0] + ga_ref[0] * o


def merge(ys, proj, w_branch, w_out, x, ga):
    bsz, t, d = x.shape
    tm = min(t, 512)
    gate_blk = OFF_GATE // d
    y_spec = pl.BlockSpec((1, tm, BR_W), lambda b, i: (b, i, 0))
    g_specs = [pl.BlockSpec((1, tm, d), functools.partial(lambda b, i, k: (b, i, gate_blk + k), k=k))
               for k in range(N_BRANCH)]
    return pl.pallas_call(
        _merge_kernel,
        grid=(bsz, t // tm),
        in_specs=[y_spec] * N_BRANCH + g_specs + [
            pl.BlockSpec((N_BRANCH, BR_W, d), lambda b, i: (0, 0, 0)),
            pl.BlockSpec((d, d), lambda b, i: (0, 0)),
            pl.BlockSpec((1, tm, d), lambda b, i: (b, i, 0)),
            pl.BlockSpec((1, 1, d), lambda b, i: (b, 0, 0))],
        out_specs=pl.BlockSpec((1, tm, d), lambda b, i: (b, i, 0)),
        out_shape=jax.ShapeDtypeStruct((bsz, t, d), F32),
        compiler_params=_cparams(("parallel", "parallel")),
        name="merge",
    )(*ys, proj, proj, proj, proj, w_branch, w_out, x, ga)


PEER_TE = 2048
PEER_SUB = 1024
PEER_GROUP = 2


def _gelu_exact(x):
    return 0.5 * x * (1.0 + lax.erf(x * float(1.0 / np.sqrt(2.0))))


def _peer_kernel(xm_ref, a_ref, lb_ref, b_ref, r2_ref, u_ref, vt_ref, x_ref, ga_ref, gain_ref, o_ref,
                 acc_ref, w_ref, h_ref, *, final_norm):
    j = pl.program_id(2)

    @pl.when(j == 0)
    def _():
        acc_ref[...] = jnp.zeros_like(acc_ref)

    tn = xm_ref.shape[1]

    def spread(row):
        return jnp.broadcast_to(row, (BF16_ROWS, LANE)).astype(BF16)[None]

    def slabs(words):
        return pltpu.bitcast(words, BF16).reshape(PEER_KEYS // BF16_ROWS, BF16_ROWS, LANE)

    n_sub = PEER_TE // PEER_SUB
    n_slot = tn // LANE
    rows_per_sub = PEER_SUB // PEER_KEYS

    def first_matmul(sub):
        h_ref[sub % 2] = lax.dot_general(u_ref[pl.ds(sub * PEER_SUB, PEER_SUB), :], xm_ref[0],
                                         (((1,), (1,)), ((), ())), preferred_element_type=F32)

    def second_matmul(sub):
        experts = pl.ds(sub * PEER_SUB, PEER_SUB)
        acc_ref[...] += jnp.dot(vt_ref[:, experts], w_ref[experts, :], preferred_element_type=F32)

    def gate_slice(sub, blk):
        lanes = pl.ds(blk * LANE, LANE)
        lo = blk * LANE
        for grp in range(rows_per_sub // PEER_GROUP):
            row0 = sub * rows_per_sub + grp * PEER_GROUP
            first_rows = [pl.ds(j * (PEER_TE // PEER_KEYS) + row0 + a, 1) for a in range(PEER_GROUP)]
            g = [None] * PEER_GROUP
            for h in range(PEER_HEADS):
                r2 = slabs(r2_ref[0, h, :, lanes])
                b = slabs(b_ref[0, h, :, lanes])
                for a in range(PEER_GROUP):
                    limit = spread(lb_ref[0, h, first_rows[a], :][:, lo:lo + LANE])
                    t = jnp.where(r2 < limit, b, 0.0) * spread(a_ref[0, h, first_rows[a], :][:, lo:lo + LANE])
                    g[a] = t if g[a] is None else g[a] + t
            for a in range(PEER_GROUP):
                local = pl.ds((grp * PEER_GROUP + a) * PEER_KEYS, PEER_KEYS)
                act = _gelu_exact(h_ref[sub % 2, local, lanes]).astype(BF16)
                w_ref[pl.ds((row0 + a) * PEER_KEYS, PEER_KEYS), lanes] = g[a].reshape(PEER_KEYS, LANE) * act

    first_matmul(0)
    for sub in range(n_sub):
        if sub + 1 < n_sub:
            first_matmul(sub + 1)
        if sub >= 1:
            second_matmul(sub - 1)
        for k in range(n_slot):
            gate_slice(sub, k)
    second_matmul(n_sub - 1)

    @pl.when(j == pl.num_programs(2) - 1)
    def _():
        y = x_ref[0] + ga_ref[0] * acc_ref[...].T
        if final_norm:
            y = y * lax.rsqrt(jnp.mean(y * y, axis=-1, keepdims=True) + EPS) * gain_ref[...]
        o_ref[0] = y


def peer_dense(xm, a_t, lb_t, b_t, r2_t, u, vt, x, ga, out_gain=None):
    bsz, t, d = x.shape
    final_norm = out_gain is not None
    gain = (out_gain if final_norm else jnp.ones((d,), F32)).reshape(1, d)
    n_exp = u.shape[0]
    tn = min(t, 512)
    tab = pl.BlockSpec((1, PEER_HEADS, PEER_KEYS, tn), lambda b, i, j: (b, 0, 0, i))
    pair = pl.BlockSpec((1, PEER_HEADS, PEER_KEYS // 2, tn), lambda b, i, j: (b, 0, 0, i))
    return pl.pallas_call(
        functools.partial(_peer_kernel, final_norm=final_norm),
        grid=(bsz, t // tn, n_exp // PEER_TE),
        in_specs=[pl.BlockSpec((1, tn, d), lambda b, i, j: (b, i, 0)),
                  tab, tab, pair, pair,
                  pl.BlockSpec((PEER_TE, d), lambda b, i, j: (j, 0)),
                  pl.BlockSpec((d, PEER_TE), lambda b, i, j: (0, j)),
                  pl.BlockSpec((1, tn, d), lambda b, i, j: (b, i, 0)),
                  pl.BlockSpec((1, 1, d), lambda b, i, j: (b, 0, 0)),
                  pl.BlockSpec((1, d), lambda b, i, j: (0, 0))],
        out_specs=pl.BlockSpec((1, tn, d), lambda b, i, j: (b, i, 0)),
        out_shape=jax.ShapeDtypeStruct((bsz, t, d), F32),
        scratch_shapes=[pltpu.VMEM((d, tn), F32), pltpu.VMEM((PEER_TE, tn), BF16),
                        pltpu.VMEM((2, PEER_SUB, tn), F32)],
        compiler_params=_cparams(("parallel", "parallel", "arbitrary")),
        name="peer_dense",
    )(xm, a_t, lb_t, b_t, r2_t, u, vt, x, ga, gain)


def _extract_top(s, row, n_pick, lowest_row_only, with_rank=True):
    rank = jnp.full(s.shape, float(n_pick), F32) if with_rank else None
    tops = []
    for k in range(n_pick):
        m = jnp.max(s, axis=0, keepdims=True)
        hit = s == m
        if lowest_row_only:
            hit = row == jnp.min(jnp.where(hit, row, s.shape[0]), axis=0, keepdims=True)
        if with_rank:
            rank = jnp.where(hit, float(k), rank)
        s = jnp.where(hit, -jnp.inf, s)
        tops.append(m)
    marked = (rank < float(n_pick)) if with_rank else (s == -jnp.inf)
    n_marked = jnp.sum(jnp.where(marked, 1.0, 0.0), axis=0, keepdims=True)
    n_bad = jnp.sum(jnp.where(n_marked != float(n_pick), 1.0, 0.0))
    return rank, jnp.concatenate(tops, axis=0), n_bad


def _peer_prep_kernel(x_ref, g_ref, sh_ref, sc_ref, wqt_ref, keys_ref,
                      xm_ref, a_ref, lb_ref, b_ref, r2_ref, qt_ref):
    tn = x_ref.shape[1]
    x = x_ref[0]
    y = x * lax.rsqrt(jnp.mean(x * x, axis=-1, keepdims=True) + EPS)
    xm = ((y * g_ref[...]) * (1.0 + sc_ref[0]) + sh_ref[0]).astype(BF16)
    xm_ref[0] = xm
    qt_ref[...] = lax.dot_general(wqt_ref[...], xm, (((1,), (1,)), ((), ())), preferred_element_type=F32)
    half = keys_ref.shape[2]
    row = lax.broadcasted_iota(jnp.int32, (PEER_KEYS, LANE), 0)
    sub = lax.broadcasted_iota(jnp.int32, (8, LANE), 0)
    n_cand_rows = PEER_TOPK + 8 * (PEER_TOPK // 2 - 1) + PEER_TOPK // 2
    crow = lax.broadcasted_iota(jnp.int32, (n_cand_rows, LANE), 0)

    def head_tables(h, exact):
        n_bad = 0.0
        for blk in range(tn // LANE):
            lanes = pl.ds(blk * LANE, LANE)
            s, rank, tops = [], [], []
            for c in range(2):
                qs = qt_ref[pl.ds(pl.multiple_of((2 * h + c) * half, half), half), lanes]
                sc_ = jnp.dot(keys_ref[c], qs, preferred_element_type=F32, precision=lax.Precision.HIGHEST)
                r, t, bad = _extract_top(sc_, row, PEER_TOPK, exact, with_rank=exact or c == 1)
                n_bad = n_bad + bad
                s.append(sc_)
                rank.append(r)
                tops.append(t)
            t1, t2 = tops
            groups = [t1[0:1] + t2]
            for a in range(1, PEER_TOPK // 2):
                n_valid = PEER_TOPK // (a + 1)
                grp = t1[a:a + 1] + t2[0:8]
                groups.append(grp if n_valid >= 8 else jnp.where(sub < n_valid, grp, -jnp.inf))
            groups.append(t1[PEER_TOPK // 2:] + t2[0:1])
            cand = jnp.concatenate(groups, axis=0)
            crank, _, bad = _extract_top(cand, crow, PEER_TOPK, exact)
            n_bad = n_bad + bad
            picked = crank < float(PEER_TOPK)
            z = jnp.sum(jnp.where(picked, jnp.exp(cand - cand[0:1]), 0.0), axis=0, keepdims=True)
            pk = picked.astype(F32)
            counts = [jnp.sum(pk[0:PEER_TOPK], axis=0, keepdims=True)]
            for a in range(1, PEER_TOPK // 2):
                lo = PEER_TOPK + 8 * (a - 1)
                counts.append(jnp.sum(pk[lo:lo + 8], axis=0, keepdims=True))
            lo = PEER_TOPK + 8 * (PEER_TOPK // 2 - 1)
            for a in range(PEER_TOPK // 2):
                counts.append(pk[lo + a:lo + a + 1])
            lb = jnp.zeros((PEER_KEYS, LANE), F32)
            for a in range(PEER_TOPK):
                is_pick = (rank[0] == float(a)) if exact else (s[0] == t1[a:a + 1])
                lb = jnp.where(is_pick, counts[a], lb)
            a_ref[0, h, :, lanes] = jnp.exp(s[0] - t1[0:1])
            lb_ref[0, h, :, lanes] = lb
            b_ref[0, h, :, lanes] = pltpu.bitcast((jnp.exp(s[1] - t2[0:1]) / z).astype(BF16), jnp.uint32)
            r2_ref[0, h, :, lanes] = pltpu.bitcast(rank[1].astype(BF16), jnp.uint32)
        return n_bad

    def head_body(h, carry):
        n_bad = head_tables(h, False)

        @pl.when(n_bad > 0.0)
        def _():
            head_tables(h, True)

        return carry

    lax.fori_loop(0, PEER_HEADS, head_body, 0)


def peer_prep(x, g, shift, scale, wqt, keys):
    bsz, t, d = x.shape
    tn = min(t, 512)
    nq = wqt.shape[0]
    tab_shape = jax.ShapeDtypeStruct((bsz, PEER_HEADS, PEER_KEYS, t), F32)
    tab_spec = pl.BlockSpec((1, PEER_HEADS, PEER_KEYS, tn), lambda b, i: (b, 0, 0, i))
    pair_shape = jax.ShapeDtypeStruct((bsz, PEER_HEADS, PEER_KEYS // 2, t), jnp.uint32)
    pair_spec = pl.BlockSpec((1, PEER_HEADS, PEER_KEYS // 2, tn), lambda b, i: (b, 0, 0, i))
    return pl.pallas_call(
        _peer_prep_kernel,
        grid=(bsz, t // tn),
        in_specs=[pl.BlockSpec((1, tn, d), lambda b, i: (b, i, 0)),
                  pl.BlockSpec((1, d), lambda b, i: (0, 0)),
                  pl.BlockSpec((1, 1, d), lambda b, i: (b, 0, 0)),
                  pl.BlockSpec((1, 1, d), lambda b, i: (b, 0, 0)),
                  pl.BlockSpec((nq, d), lambda b, i: (0, 0)),
                  pl.BlockSpec(keys.shape, lambda b, i: (0, 0, 0))],
        out_specs=[pl.BlockSpec((1, tn, d), lambda b, i: (b, i, 0)), tab_spec, tab_spec, pair_spec, pair_spec],
        out_shape=[jax.ShapeDtypeStruct((bsz, t, d), BF16), tab_shape, tab_shape, pair_shape, pair_shape],
        scratch_shapes=[pltpu.VMEM((nq, tn), F32)],
        compiler_params=_cparams(("parallel", "parallel")),
        name="peer_prep",
    )(x, g.reshape(1, d), shift, scale, wqt, keys)


ROW_TILE = 128
CONV_PAD = 16


def _conv_kernel(p_ref, w_ref, b_ref, lg_ref, lb_ref, o_ref, pad_ref, y_ref):
    t, c = o_ref.shape[1], o_ref.shape[2]
    k_taps = w_ref.shape[0]
    first = CONV_PAD - (k_taps - 1) // 2
    lane_w = 256
    pad_ref[0:CONV_PAD, :] = jnp.zeros((CONV_PAD, c), F32)
    pad_ref[CONV_PAD + t:, :] = jnp.zeros((CONV_PAD, c), F32)

    def glu_body(i, carry):
        rows = pl.ds(pl.multiple_of(i * ROW_TILE, ROW_TILE), ROW_TILE)
        a = p_ref[0, rows, 0:c]
        g = p_ref[0, rows, c:2 * c]
        pad_ref[pl.ds(pl.multiple_of(CONV_PAD + i * ROW_TILE, 8), ROW_TILE), :] = a * jax.nn.sigmoid(g)
        return carry

    lax.fori_loop(0, t // ROW_TILE, glu_body, 0)

    def conv_body(i, carry):
        base = pl.multiple_of(i * ROW_TILE, ROW_TILE)
        for lb in range(c // lane_w):
            lanes = pl.ds(lb * lane_w, lane_w)
            win = pad_ref[pl.ds(base, ROW_TILE + 2 * CONV_PAD), lanes]
            acc = jnp.broadcast_to(b_ref[:, lanes], (ROW_TILE, lane_w))
            for k in range(k_taps):
                acc = acc + w_ref[k:k + 1, lanes] * win[first + k:first + k + ROW_TILE]
            y_ref[:, lanes] = acc
        y = y_ref[...]
        yc = y - jnp.mean(y, axis=-1, keepdims=True)
        z = yc * lax.rsqrt(jnp.mean(yc * yc, axis=-1, keepdims=True) + EPS) * lg_ref[...] + lb_ref[...]
        o_ref[0, pl.ds(base, ROW_TILE), :] = z * jax.nn.sigmoid(z)
        return carry

    lax.fori_loop(0, t // ROW_TILE, conv_body, 0)


def conv_branch(proj, w, b, ln_g, ln_b):
    bsz, t, _ = proj.shape
    k_taps, c = w.shape
    vec = pl.BlockSpec((1, c), lambda i: (0, 0))
    return pl.pallas_call(
        _conv_kernel,
        grid=(bsz,),
        in_specs=[pl.BlockSpec((1, t, 2 * c), lambda i: (i, 0, OFF_CONV // (2 * c))),
                  pl.BlockSpec((k_taps, c), lambda i: (0, 0)), vec, vec, vec],
        out_specs=pl.BlockSpec((1, t, c), lambda i: (i, 0, 0)),
        out_shape=jax.ShapeDtypeStruct((bsz, t, c), F32),
        scratch_shapes=[pltpu.VMEM((t + 2 * CONV_PAD, c), F32), pltpu.VMEM((ROW_TILE, c), F32)],
        compiler_params=_cparams(("parallel",)),
        name="conv_branch",
    )(proj, w, b.reshape(1, c), ln_g.reshape(1, c), ln_b.reshape(1, c))


def rope_tables(n_tok, head_dim):
    nf = head_dim // 4
    lane = np.arange(LANE)
    d_idx = lane % head_dim
    by_col = d_idx >= head_dim // 2
    freqs = ROPE_BASE ** (-(d_idx % nf).astype(np.float32) / nf)
    tok = np.arange(n_tok)
    pos = np.where(by_col[None, :], (tok % GRID_W)[:, None], (tok // GRID_W)[:, None]).astype(np.float32)
    ang = jnp.asarray(pos) * jnp.asarray(freqs, F32)[None, :]
    sign = np.where(d_idx % (2 * nf) < nf, -1.0, 1.0).astype(np.float32)
    return jnp.cos(ang), jnp.sin(ang) * sign[None, :]


def _rope(x, cos, sin_signed, nf):
    lane = lax.broadcasted_iota(jnp.int32, x.shape, 1)
    partner = jnp.where(lane % (2 * nf) < nf, pltpu.roll(x, LANE - nf, 1), pltpu.roll(x, nf, 1))
    return x * cos + partner * sin_signed


FINISH_ROWS = 256


def _head_norm_rows(o):
    oc = o - jnp.mean(o, axis=-1, keepdims=True)
    return oc * lax.rsqrt(jnp.mean(oc * oc, axis=-1, keepdims=True) + EPS)


def _ret_kernel(dec_ref, ql_ref, kl_ref, vl_ref, gl_ref, qc_ref, kc_ref, vc_ref, gc_ref, cos_ref, sin_ref,
                yl_ref, yc_ref, ol_ref, oc_ref, olb_ref, ocb_ref):
    h = pl.program_id(1)
    n_l, n_c = ql_ref.shape[1] // CHUNK, qc_ref.shape[1] // CHUNK
    d = ql_ref.shape[2]
    row = lax.broadcasted_iota(jnp.int32, (CHUNK, CHUNK), 0).astype(F32)
    col = lax.broadcasted_iota(jnp.int32, (CHUNK, CHUNK), 1).astype(F32)
    lg_f = jax.nn.log_sigmoid(jnp.full((CHUNK, CHUNK), dec_ref[0, h], F32))
    lg_b = jax.nn.log_sigmoid(jnp.full((CHUNK, CHUNK), dec_ref[1, h], F32))
    dmat = (jnp.where(row >= col, jnp.exp((row - col) * lg_f), 0.0)
            + jnp.where(col >= row, jnp.exp((col - row) * lg_b), 0.0))
    qdec_f, kdec_f, cdec_f = jnp.exp((row + 1.0) * lg_f), jnp.exp((CHUNK - 1.0 - row) * lg_f), jnp.exp(CHUNK * lg_f)
    qdec_b, kdec_b, cdec_b = jnp.exp((CHUNK - row) * lg_b), jnp.exp(row * lg_b), jnp.exp(CHUNK * lg_b)
    scale = d ** -0.5
    nt = (((1,), (1,)), ((), ()))
    tn = (((0,), (0,)), ((), ()))

    def load(refs, c, rotate):
        q_ref, k_ref, v_ref = refs
        rows = pl.ds(pl.multiple_of(c * CHUNK, CHUNK), CHUNK)
        q, k, v = q_ref[0, rows, :], k_ref[0, rows, :], v_ref[0, rows, :]
        if rotate:
            cos, sin = cos_ref[rows, :], sin_ref[rows, :]
            q, k = _rope(q, cos, sin, d // 4), _rope(k, cos, sin, d // 4)
        return q.astype(BF16), k * scale, v.astype(BF16), rows

    def both(refs, of_ref, ob_ref, rotate, n):
        def body(i, carry):
            s_f, s_b = carry
            q, k, v, rows = load(refs, i, rotate)
            att = lax.dot_general(q, k.astype(BF16), nt, preferred_element_type=F32) * dmat
            o = jnp.dot(att.astype(BF16), v, preferred_element_type=F32)
            of_ref[rows, :] = o + qdec_f * jnp.dot(q, s_f.astype(BF16), preferred_element_type=F32)
            s_f = cdec_f * s_f + lax.dot_general((k * kdec_f).astype(BF16), v, tn, preferred_element_type=F32)
            q, k, v, rows = load(refs, n - 1 - i, rotate)
            ob_ref[rows, :] = qdec_b * jnp.dot(q, s_b.astype(BF16), preferred_element_type=F32)
            s_b = cdec_b * s_b + lax.dot_general((k * kdec_b).astype(BF16), v, tn, preferred_element_type=F32)
            return s_f, s_b
        return body

    lat, ctx = (ql_ref, kl_ref, vl_ref), (qc_ref, kc_ref, vc_ref)
    s0 = jnp.zeros((d, d), F32)
    state = lax.fori_loop(0, n_c, both(ctx, oc_ref, ocb_ref, False, n_c), (s0, s0))
    lax.fori_loop(0, n_l, both(lat, ol_ref, olb_ref, True, n_l), state, unroll=2)

    def finish(of_ref, ob_ref, g_ref, y_ref):
        def body(c, carry):
            rows = pl.ds(pl.multiple_of(c * FINISH_ROWS, FINISH_ROWS), FINISH_ROWS)
            g = g_ref[0, rows, :]
            y_ref[0, rows, :] = _head_norm_rows(of_ref[rows, :] + ob_ref[rows, :]) * (g * jax.nn.sigmoid(g))
            return carry
        lax.fori_loop(0, of_ref.shape[0] // FINISH_ROWS, body, 0)

    finish(ol_ref, olb_ref, gl_ref, yl_ref)
    finish(oc_ref, ocb_ref, gc_ref, yc_ref)


def retention_branch(proj_l, proj_c, ret_decay, cos, sin):
    bsz, t_l, _ = proj_l.shape
    t_c = proj_c.shape[1]
    d = BR_W // RET_HEADS
    blk0 = OFF_RET // d

    def col(t, part):
        return pl.BlockSpec((1, t, d), functools.partial(lambda b, h, dec, p: (b, 0, blk0 + p * RET_HEADS + h), p=part))

    table = pl.BlockSpec((t_l, LANE), lambda b, h, dec: (0, 0))
    grid_spec = pltpu.PrefetchScalarGridSpec(
        num_scalar_prefetch=1,
        grid=(bsz, RET_HEADS),
        in_specs=[col(t_l, p) for p in range(4)] + [col(t_c, p) for p in range(4)] + [table, table],
        out_specs=[pl.BlockSpec((1, t_l, d), lambda b, h, dec: (b, 0, h)),
                   pl.BlockSpec((1, t_c, d), lambda b, h, dec: (b, 0, h))],
        scratch_shapes=[pltpu.VMEM((t_l, d), F32), pltpu.VMEM((t_c, d), F32),
                        pltpu.VMEM((t_l, d), F32), pltpu.VMEM((t_c, d), F32)])
    return pl.pallas_call(
        _ret_kernel,
        grid_spec=grid_spec,
        out_shape=[jax.ShapeDtypeStruct((bsz, t_l, BR_W), F32), jax.ShapeDtypeStruct((bsz, t_c, BR_W), F32)],
        compiler_params=_cparams(("parallel", "parallel")),
        name="retention_branch",
    )(ret_decay, proj_l, proj_l, proj_l, proj_l, proj_c, proj_c, proj_c, proj_c, cos, sin)


SHORT_PAD = 8
MLSTM_HP = 1


def _mlstm_kernel(ql_ref, kl_ref, vl_ref, ol_ref, gl_ref, qc_ref, kc_ref, vc_ref, oc_ref, gc_ref,
                  wq_ref, wk_ref, bq_ref, bk_ref, gb_ref, yl_ref, yc_ref,
                  pad_ref, qsl_ref, ksl_ref, qsc_ref, ksc_ref, hl_ref, hc_ref, hlb_ref, hcb_ref):
    head0 = pl.program_id(1) * MLSTM_HP
    t_l, t_c = ql_ref.shape[1], qc_ref.shape[1]
    width = ql_ref.shape[2]
    d = width // MLSTM_HP
    row = lax.broadcasted_iota(jnp.int32, (CHUNK, CHUNK), 0)
    col = lax.broadcasted_iota(jnp.int32, (CHUNK, CHUNK), 1)
    eye = row == col
    scale = d ** -0.5
    nt = (((1,), (1,)), ((), ()))
    tn = (((0,), (0,)), ((), ()))

    def short_conv(x_ref, w_ref, b_ref, out_ref, t):
        k_taps = w_ref.shape[0]
        first = SHORT_PAD - (k_taps - 1) // 2
        pad_ref[0:SHORT_PAD, :] = jnp.zeros((SHORT_PAD, width), F32)
        pad_ref[SHORT_PAD + t:SHORT_PAD + t + SHORT_PAD, :] = jnp.zeros((SHORT_PAD, width), F32)

        def copy(i, carry):
            pad_ref[pl.ds(pl.multiple_of(SHORT_PAD + i * CHUNK, 8), CHUNK), :] = \
                x_ref[0, pl.ds(pl.multiple_of(i * CHUNK, CHUNK), CHUNK), :]
            return carry

        lax.fori_loop(0, t // CHUNK, copy, 0)

        def conv(i, carry):
            base = pl.multiple_of(i * CHUNK, CHUNK)
            win = pad_ref[pl.ds(base, CHUNK + 2 * SHORT_PAD), :]
            acc = jnp.broadcast_to(b_ref[...], (CHUNK, width))
            for k in range(k_taps):
                acc = acc + w_ref[k:k + 1, :] * win[first + k:first + k + CHUNK]
            out_ref[pl.ds(base, CHUNK), :] = acc * jax.nn.sigmoid(acc)
            return carry

        lax.fori_loop(0, t // CHUNK, conv, 0)

    short_conv(ql_ref, wq_ref, bq_ref, qsl_ref, t_l)
    short_conv(kl_ref, wk_ref, bk_ref, ksl_ref, t_l)
    short_conv(qc_ref, wq_ref, bq_ref, qsc_ref, t_c)
    short_conv(kc_ref, wk_ref, bk_ref, ksc_ref, t_c)

    def gate_cols(g_ref, rows, i_lane, f_lane):
        x = g_ref[0, rows, :] + gb_ref[...]
        i_col = jnp.sum(jnp.where(col == i_lane, x, 0.0), axis=1, keepdims=True)
        f_col = jax.nn.log_sigmoid(jnp.sum(jnp.where(col == f_lane, x, 0.0), axis=1, keepdims=True))
        i_row = jnp.sum(jnp.where(eye, i_col, 0.0), axis=0, keepdims=True)
        f_row = jnp.sum(jnp.where(eye, f_col, 0.0), axis=0, keepdims=True)
        return i_col, f_col, i_row, f_row

    def chunk_step(seq, c, state, forward, hk):
        qs_ref, ks_ref, v_ref, g_ref = seq[:4]
        h_ref = seq[4] if forward else seq[5]
        cmat, nvec, m = state
        rows = pl.ds(pl.multiple_of(c * CHUNK, CHUNK), CHUNK)
        lanes = pl.ds(hk * d, d)
        h = head0 + hk
        q = qs_ref[rows, lanes]
        k = ks_ref[rows, lanes] * scale
        qb, vb = q.astype(BF16), v_ref[0, rows, lanes].astype(BF16)
        i_col, f_col, i_row, f_row = gate_cols(g_ref, rows, (0 if forward else 2 * MLSTM_HEADS) + h,
                                               (MLSTM_HEADS if forward else 3 * MLSTM_HEADS) + h)
        seen = (col <= row) if forward else (col >= row)
        seen_t = (row <= col) if forward else (row >= col)
        b_col = jnp.sum(jnp.where(seen, f_row, 0.0), axis=1, keepdims=True)
        b_row = jnp.sum(jnp.where(seen_t, f_col, 0.0), axis=0, keepdims=True)
        b_end = jnp.sum(f_row, axis=1, keepdims=True)
        logw = jnp.where(seen, b_col - b_row + i_row, -jnp.inf)
        inter = b_col + m
        m_t = jnp.maximum(inter, jnp.max(logw, axis=1, keepdims=True))
        s = lax.dot_general(qb, k.astype(BF16), nt, preferred_element_type=F32) * jnp.exp(logw - m_t)
        w_prev = jnp.exp(inter - m_t)
        num = (jnp.dot(s.astype(BF16), vb, preferred_element_type=F32)
               + w_prev * jnp.dot(qb, cmat.astype(BF16), preferred_element_type=F32))
        den = jnp.sum(s, axis=1, keepdims=True) + w_prev * jnp.sum(q * nvec, axis=1, keepdims=True)
        hc = num * (1.0 / jnp.maximum(jnp.abs(den), jnp.exp(-m_t)))
        h_ref[rows, lanes] = hc
        log_wk = b_end - b_col + i_col
        m_new = jnp.maximum(b_end + m, jnp.max(log_wk, axis=0, keepdims=True))
        decay = jnp.exp(b_end + m - m_new)
        kw = k * jnp.exp(log_wk - m_new)
        cmat = decay * cmat + lax.dot_general(kw.astype(BF16), vb, tn, preferred_element_type=F32)
        nvec = decay * nvec + jnp.sum(kw, axis=0, keepdims=True)
        return cmat, nvec, m_new

    lat = (qsl_ref, ksl_ref, vl_ref, gl_ref, hl_ref, hlb_ref)
    ctx = (qsc_ref, ksc_ref, vc_ref, gc_ref, hc_ref, hcb_ref)
    n_l, n_c = t_l // CHUNK, t_c // CHUNK
    state0 = (jnp.zeros((d, d), F32), jnp.zeros((1, d), F32), jnp.full((1, 1), NEG_INIT, F32))

    def both(seq, n):
        def body(i, carry):
            fwd = tuple(chunk_step(seq, i, carry[0][hk], True, hk) for hk in range(MLSTM_HP))
            bwd = tuple(chunk_step(seq, n - 1 - i, carry[1][hk], False, hk) for hk in range(MLSTM_HP))
            return fwd, bwd
        return body

    st = lax.fori_loop(0, n_c, both(ctx, n_c), ((state0,) * MLSTM_HP, (state0,) * MLSTM_HP))
    lax.fori_loop(0, n_l, both(lat, n_l), st, unroll=2)

    def finish(hf_ref, hb_ref, o_ref, y_ref):
        def body(c, carry):
            rows = pl.ds(pl.multiple_of(c * FINISH_ROWS, FINISH_ROWS), FINISH_ROWS)
            for hk in range(MLSTM_HP):
                lanes = pl.ds(hk * d, d)
                y_ref[0, rows, lanes] = (_head_norm_rows(hf_ref[rows, lanes] + hb_ref[rows, lanes])
                                         * jax.nn.sigmoid(o_ref[0, rows, lanes]))
            return carry
        lax.fori_loop(0, hf_ref.shape[0] // FINISH_ROWS, body, 0)

    finish(hl_ref, hlb_ref, ol_ref, yl_ref)
    finish(hc_ref, hcb_ref, oc_ref, yc_ref)


def mlstm_branch(proj_l, proj_c, conv_w, conv_b, gate_b):
    bsz, t_l, _ = proj_l.shape
    t_c = proj_c.shape[1]
    w = MLSTM_HP * (BR_W // MLSTM_HEADS)
    blk0 = OFF_MLSTM // w
    n_grp = MLSTM_HEADS // MLSTM_HP
    k_taps = conv_w.shape[0]

    def col(t, part):
        return pl.BlockSpec((1, t, w), functools.partial(lambda b, h, p: (b, 0, blk0 + p * n_grp + h), p=part))

    def gates(t):
        return pl.BlockSpec((1, t, LANE), lambda b, h: (b, 0, OFF_MGATE // LANE))

    gb = jnp.zeros((1, LANE), F32).at[0, :MLSTM_GATES].set(gate_b.reshape(-1))
    return pl.pallas_call(
        _mlstm_kernel,
        grid=(bsz, n_grp),
        in_specs=[col(t_l, p) for p in range(4)] + [gates(t_l)] + [col(t_c, p) for p in range(4)] + [gates(t_c)] + [
            pl.BlockSpec((k_taps, w), lambda b, h: (0, h)),
            pl.BlockSpec((k_taps, w), lambda b, h: (0, n_grp + h)),
            pl.BlockSpec((1, w), lambda b, h: (0, h)),
            pl.BlockSpec((1, w), lambda b, h: (0, n_grp + h)),
            pl.BlockSpec((1, LANE), lambda b, h: (0, 0))],
        out_specs=[pl.BlockSpec((1, t_l, w), lambda b, h: (b, 0, h)),
                   pl.BlockSpec((1, t_c, w), lambda b, h: (b, 0, h))],
        out_shape=[jax.ShapeDtypeStruct((bsz, t_l, BR_W), F32), jax.ShapeDtypeStruct((bsz, t_c, BR_W), F32)],
        scratch_shapes=[pltpu.VMEM((t_l + 2 * SHORT_PAD, w), F32),
                        pltpu.VMEM((t_l, w), F32), pltpu.VMEM((t_l, w), F32),
                        pltpu.VMEM((t_c, w), F32), pltpu.VMEM((t_c, w), F32),
                        pltpu.VMEM((t_l, w), F32), pltpu.VMEM((t_c, w), F32),
                        pltpu.VMEM((t_l, w), F32), pltpu.VMEM((t_c, w), F32)],
        compiler_params=_cparams(("parallel", "parallel")),
        name="mlstm_branch",
    )(proj_l, proj_l, proj_l, proj_l, proj_l, proj_c, proj_c, proj_c, proj_c, proj_c,
      conv_w, conv_w, conv_b.reshape(1, -1), conv_b.reshape(1, -1), gb)


def _attn_kernel(sink_ref, q_ref, *refs, local):
    if local:
        kp_ref, kc_ref, kn_ref, vp_ref, vc_ref, vn_ref, kx_ref, vx_ref, cos_ref, sin_ref, o_ref, k_scr, v_scr = refs
    else:
        kx_ref, vx_ref, o_ref, k_scr, v_scr = refs
    i = pl.program_id(1)
    nq = pl.num_programs(1)
    t_x = kx_ref.shape[1]
    n_loc = 3 * BLOCK if local else 0
    n_keys = n_loc + t_x
    nf = ATTN_DIM // 4
    half = LANE // 2

    def put(dst, lo, x):
        dst[0, lo:lo + x.shape[0], :] = x.astype(BF16)
        dst[1, lo:lo + x.shape[0], :] = pltpu.roll(x, half, 1).astype(BF16)

    if local:
        for j, (k_ref, v_ref) in enumerate(((kp_ref, vp_ref), (kc_ref, vc_ref), (kn_ref, vn_ref))):
            blk = jnp.clip(i + (j - 1), 0, nq - 1)
            rows = pl.ds(pl.multiple_of(blk * BLOCK, BLOCK), BLOCK)
            put(k_scr, j * BLOCK, _rope(k_ref[0], cos_ref[rows, :], sin_ref[rows, :], nf))
            put(v_scr, j * BLOCK, v_ref[0])
    put(k_scr, n_loc, kx_ref[0])
    put(v_scr, n_loc, vx_ref[0])

    lane = lax.broadcasted_iota(jnp.int32, (BLOCK, LANE), 1)
    if local:
        rel = (lax.broadcasted_iota(jnp.int32, (BLOCK, n_keys), 1)
               - lax.broadcasted_iota(jnp.int32, (BLOCK, n_keys), 0))
        kpos = (i - 1) * BLOCK + lax.broadcasted_iota(jnp.int32, (BLOCK, n_keys), 1)
        ok = (rel >= 0) & (rel <= 2 * WINDOW) & (kpos >= 0) & (kpos < nq * BLOCK)
        ok = ok | (lax.broadcasted_iota(jnp.int32, (BLOCK, n_keys), 1) >= n_loc)
        bias = jnp.where(ok, 0.0, -jnp.inf)
        qrows = pl.ds(pl.multiple_of(i * BLOCK, BLOCK), BLOCK)
        cos_q, sin_q = cos_ref[qrows, :], sin_ref[qrows, :]
    nt = (((1,), (1,)), ((), ()))
    swaps, scores, sinks = [], [], []
    for p in range(ATTN_HEADS // 2):
        qp = q_ref[0, :, p * LANE:(p + 1) * LANE]
        if local:
            qp = _rope(qp, cos_q, sin_q, nf)
        qp = qp * (ATTN_DIM ** -0.5)
        for sub in range(2):
            hd = 2 * p + sub
            kv_half = hd // (ATTN_HEADS // ATTN_KV_HEADS)
            sel = (lane >= half) if sub else (lane < half)
            qm = jnp.where(sel, qp, 0.0).astype(BF16)
            swaps.append(0 if kv_half == sub else 1)
            scores.append(lax.dot_general(qm, k_scr[swaps[-1]], nt, preferred_element_type=F32))
            sinks.append(jnp.full((BLOCK, 1), sink_ref[hd], F32))
    s = jnp.stack(scores)
    sink = jnp.stack(sinks)
    if local:
        s = s + bias[None]
    m = jnp.maximum(jnp.max(s, axis=2, keepdims=True), sink)
    e = jnp.exp(s - m)
    inv = 1.0 / (jnp.sum(e, axis=2, keepdims=True) + jnp.exp(sink - m))
    prob = (e * inv).astype(BF16)
    for p in range(ATTN_HEADS // 2):
        outs = [jnp.dot(prob[2 * p + sub], v_scr[swaps[2 * p + sub]], preferred_element_type=F32)
                for sub in range(2)]
        o_ref[0, :, p * LANE:(p + 1) * LANE] = jnp.where(lane < half, outs[0], outs[1])


def attention_branch(proj_q, proj_c, sink, cos, sin, local):
    bsz, t, _ = proj_q.shape
    t_c = proj_c.shape[1]
    nq = t // BLOCK
    n_qcol = ATTN_HEADS * ATTN_DIM
    q_blk = OFF_ATTN // n_qcol
    k_blk = (OFF_ATTN + n_qcol) // LANE
    v_blk = k_blk + 1
    n_keys = (3 * BLOCK if local else 0) + t_c

    def kv(blk, shift):
        return pl.BlockSpec((1, BLOCK, LANE), lambda b, i, s: (b, jnp.clip(i + shift, 0, nq - 1), blk))

    in_specs = [pl.BlockSpec((1, BLOCK, n_qcol), lambda b, i, s: (b, i, q_blk))]
    args = [proj_q]
    if local:
        in_specs += [kv(k_blk, -1), kv(k_blk, 0), kv(k_blk, 1), kv(v_blk, -1), kv(v_blk, 0), kv(v_blk, 1)]
        args += [proj_q] * 6
    in_specs += [pl.BlockSpec((1, t_c, LANE), lambda b, i, s: (b, 0, k_blk)),
                 pl.BlockSpec((1, t_c, LANE), lambda b, i, s: (b, 0, v_blk))]
    args += [proj_c, proj_c]
    if local:
        in_specs += [pl.BlockSpec((t, LANE), lambda b, i, s: (0, 0))] * 2
        args += [cos, sin]
    grid_spec = pltpu.PrefetchScalarGridSpec(
        num_scalar_prefetch=1,
        grid=(bsz, nq),
        in_specs=in_specs,
        out_specs=pl.BlockSpec((1, BLOCK, n_qcol), lambda b, i, s: (b, i, 0)),
        scratch_shapes=[pltpu.VMEM((2, n_keys, LANE), BF16), pltpu.VMEM((2, n_keys, LANE), BF16)])
    return pl.pallas_call(
        functools.partial(_attn_kernel, local=local),
        grid_spec=grid_spec,
        out_shape=jax.ShapeDtypeStruct((bsz, t, n_qcol), F32),
        compiler_params=_cparams(("parallel", "parallel")),
        name="attention_branch",
    )(sink, *args)


def _pack_w_in(w):
    d = w.shape[0]
    used = w.shape[1] - MLSTM_GATES + LANE
    cols = [w[:, :REF_MGATE_OFF], w[:, REF_GATE_OFF:], w[:, REF_ATTN_OFF:REF_GATE_OFF],
            w[:, REF_MGATE_OFF:REF_ATTN_OFF],
            jnp.zeros((d, _round_up(used, PROJ_TILE_N) - w.shape[1]), w.dtype)]
    return jnp.concatenate(cols, axis=1).astype(BF16)


def kernel(x, c, ctx, c_ctx, mod_w, mod_b, norm1_g, norm2_g, w_in, conv_w, conv_b, conv_ln_g, conv_ln_b,
           ret_decay, mlstm_conv_w, mlstm_conv_b, mlstm_gate_b, attn_sink, w_branch, w_out,
           peer_wq, peer_keys, peer_u, peer_v, final_g):
    bsz, seq, d = x.shape
    assert d == D_MODEL
    depth = mod_w.shape[0]
    cos_r, sin_r = rope_tables(seq, BR_W // RET_HEADS)
    cos_a, sin_a = rope_tables(seq, ATTN_DIM)
    n_cond = _round_up(bsz + 1, 8)
    cond = jnp.concatenate([c, c_ctx[None], jnp.zeros((n_cond - bsz - 1, d), F32)], axis=0)
    for l in range(depth):
        last = l == depth - 1
        mod = mod_vectors(cond, mod_w[l], mod_b[l])
        sh1, sc1, ga1, sh2, sc2, ga2 = (m[:, None, :] for m in jnp.split(mod[:bsz], 6, axis=-1))
        csh1, csc1, cga1, csh2, csc2, cga2 = (m[None] for m in jnp.split(mod[bsz:bsz + 1], 6, axis=-1))
        ctx_shape = ctx.shape
        ctx_flat = ctx.reshape(1, -1, d)
        w_in_p = _pack_w_in(w_in[l])
        proj_l = norm_matmul(x, norm1_g[l], sh1, sc1, w_in_p)
        proj_c_flat = norm_matmul(ctx_flat, norm1_g[l], csh1, csc1, w_in_p)
        proj_c = proj_c_flat.reshape(ctx_shape[:2] + (-1,))
        ret_l, ret_c = retention_branch(proj_l, proj_c, ret_decay[l], cos_r, sin_r)
        mls_l, mls_c = mlstm_branch(proj_l, proj_c, mlstm_conv_w[l], mlstm_conv_b[l], mlstm_gate_b[l])
        conv_args = (conv_w[l], conv_b[l], conv_ln_g[l], conv_ln_b[l])
        ys_l = (conv_branch(proj_l, *conv_args), ret_l, mls_l,
                attention_branch(proj_l, proj_c, attn_sink[l], cos_a, sin_a, True))
        if not last:
            ys_c = (conv_branch(proj_c, *conv_args), ret_c, mls_c,
                    attention_branch(proj_c, proj_c, attn_sink[l], cos_a, sin_a, False))
        wb, wo = w_branch[l].astype(BF16), w_out[l].astype(BF16)
        wqt = peer_wq[l].T.astype(BF16)
        u = peer_u[l].astype(BF16)
        vt = peer_v[l].T.astype(BF16)

        def peer(xx, sh, sc, ga, out_gain=None):
            xm, *tabs = peer_prep(xx, norm2_g[l], sh, sc, wqt, peer_keys[l])
            return peer_dense(xm, *tabs, u, vt, xx, ga, out_gain)

        x = merge(ys_l, proj_l, wb, wo, x, ga1)
        x = peer(x, sh2, sc2, ga2, final_g if last else None)
        if not last:
            ys_c_flat = tuple(y.reshape(1, -1, y.shape[-1]) for y in ys_c)
            ctx_flat = merge(ys_c_flat, proj_c_flat, wb, wo, ctx_flat, cga1)
            ctx = peer(ctx_flat, csh2, csc2, cga2).reshape(ctx_shape)
    return x
```

```python
import functools

import jax
import jax.numpy as jnp
import numpy as np
from jax import lax
from jax.experimental import pallas as pl
from jax.experimental.pallas import tpu as pltpu

F32 = jnp.float32
BF16 = jnp.bfloat16

GRID_W = 64
EPS = 1e-6
N_BRANCH = 4
BR_W = 512
RET_HEADS = 4
MLSTM_HEADS = 4
MLSTM_GATES = 4 * MLSTM_HEADS
ATTN_HEADS = 8
ATTN_KV_HEADS = 2
ATTN_DIM = BR_W // ATTN_HEADS
WINDOW = 128
BLOCK = 128
CHUNK = 128
ROPE_BASE = 10000.0
NEG_INIT = -1e30
PEER_HEADS = 8
PEER_KEYS = 128
PEER_TOPK = 16

CONV_IN = 2 * BR_W
RET_IN = 4 * BR_W
MLSTM_QKVO = 4 * BR_W
ATTN_IN = ATTN_HEADS * ATTN_DIM + 2 * ATTN_KV_HEADS * ATTN_DIM
REF_MGATE_OFF = CONV_IN + RET_IN + MLSTM_QKVO
REF_ATTN_OFF = REF_MGATE_OFF + MLSTM_GATES
REF_GATE_OFF = REF_ATTN_OFF + ATTN_IN
OFF_CONV = 0
OFF_RET = OFF_CONV + CONV_IN
OFF_MLSTM = OFF_RET + RET_IN
OFF_GATE = OFF_MLSTM + MLSTM_QKVO
D_MODEL = 1024
OFF_ATTN = OFF_GATE + N_BRANCH * D_MODEL
OFF_MGATE = OFF_ATTN + ATTN_IN
LANE = 128
BF16_ROWS = 16
PROJ_TILE_N = 1024


def _round_up(a, m):
    return (a + m - 1) // m * m


V7X_VMEM_BYTES = 64 * 1024 * 1024
VMEM_LIMIT = V7X_VMEM_BYTES * 7 // 8


def _cparams(sem):
    return pltpu.CompilerParams(dimension_semantics=sem, vmem_limit_bytes=VMEM_LIMIT)


def _mod_kernel(s_ref, w_ref, b_ref, o_ref):
    s = s_ref[...]
    s = s * jax.nn.sigmoid(s)
    o_ref[...] = jnp.dot(s, w_ref[...], preferred_element_type=F32,
                         precision=lax.Precision.HIGHEST) + b_ref[...]


def mod_vectors(cond, w, b):
    r, d = cond.shape
    n = w.shape[1]
    tn = 512
    return pl.pallas_call(
        _mod_kernel,
        grid=(n // tn,),
        in_specs=[pl.BlockSpec((r, d), lambda j: (0, 0)),
                  pl.BlockSpec((d, tn), lambda j: (0, j)),
                  pl.BlockSpec((1, tn), lambda j: (0, j))],
        out_specs=pl.BlockSpec((r, tn), lambda j: (0, j)),
        out_shape=jax.ShapeDtypeStruct((r, n), F32),
        compiler_params=_cparams(("parallel",)),
        name="mod_vectors",
    )(cond, w, b.reshape(1, n))


def _norm_matmul_kernel(x_ref, g_ref, sh_ref, sc_ref, w_ref, o_ref, h_ref):
    @pl.when(pl.program_id(2) == 0)
    def _():
        x = x_ref[0]
        y = x * lax.rsqrt(jnp.mean(x * x, axis=-1, keepdims=True) + EPS)
        h_ref[...] = ((y * g_ref[...]) * (1.0 + sc_ref[0]) + sh_ref[0]).astype(BF16)

    o_ref[0] = jnp.dot(h_ref[...], w_ref[...], preferred_element_type=F32)


def norm_matmul(x, g, shift, scale, w):
    bsz, t, d = x.shape
    n = w.shape[1]
    tm = min(t, 2048)
    tn = min(n, PROJ_TILE_N)
    return pl.pallas_call(
        _norm_matmul_kernel,
        grid=(bsz, t // tm, n // tn),
        in_specs=[pl.BlockSpec((1, tm, d), lambda b, i, j: (b, i, 0)),
                  pl.BlockSpec((1, d), lambda b, i, j: (0, 0)),
                  pl.BlockSpec((1, 1, d), lambda b, i, j: (b, 0, 0)),
                  pl.BlockSpec((1, 1, d), lambda b, i, j: (b, 0, 0)),
                  pl.BlockSpec((d, tn), lambda b, i, j: (0, j))],
        out_specs=pl.BlockSpec((1, tm, tn), lambda b, i, j: (b, i, j)),
        out_shape=jax.ShapeDtypeStruct((bsz, t, n), F32),
        scratch_shapes=[pltpu.VMEM((tm, d), BF16)],
        compiler_params=_cparams(("parallel", "parallel", "arbitrary")),
        name="norm_matmul",
    )(x, g.reshape(1, d), shift, scale, w)


def _merge_kernel(y0, y1, y2, y3, g0, g1, g2, g3, wb_ref, wo_ref, x_ref, ga_ref, o_ref):
    acc = None
    for i, (y, g) in enumerate(((y0, g0), (y1, g1), (y2, g2), (y3, g3))):
        p = jnp.dot(y[0].astype(BF16), wb_ref[i], preferred_element_type=F32)
        t = jax.nn.sigmoid(g[0]) * p
        acc = t if acc is None else acc + t
    o = jnp.dot(acc.astype(BF16), wo_ref[...], preferred_element_type=F32)
    o_ref[0] = x_ref[0] + ga_ref[0] * o


def merge(ys, proj, w_branch, w_out, x, ga):
    bsz, t, d = x.shape
    tm = min(t, 512)
    gate_blk = OFF_GATE // d
    y_spec = pl.BlockSpec((1, tm, BR_W), lambda b, i: (b, i, 0))
    g_specs = [pl.BlockSpec((1, tm, d), functools.partial(lambda b, i, k: (b, i, gate_blk + k), k=k))
               for k in range(N_BRANCH)]
    return pl.pallas_call(
        _merge_kernel,
        grid=(bsz, t // tm),
        in_specs=[y_spec] * N_BRANCH + g_specs + [
            pl.BlockSpec((N_BRANCH, BR_W, d), lambda b, i: (0, 0, 0)),
            pl.BlockSpec((d, d), lambda b, i: (0, 0)),
            pl.BlockSpec((1, tm, d), lambda b, i: (b, i, 0)),
            pl.BlockSpec((1, 1, d), lambda b, i: (b, 0, 0))],
        out_specs=pl.BlockSpec((1, tm, d), lambda b, i: (b, i, 0)),
        out_shape=jax.ShapeDtypeStruct((bsz, t, d), F32),
        compiler_params=_cparams(("parallel", "parallel")),
        name="merge",
    )(*ys, proj, proj, proj, proj, w_branch, w_out, x, ga)


PEER_TE = 2048
PEER_SUB = 1024
PEER_GROUP = 2


def _gelu_exact(x):
    return 0.5 * x * (1.0 + lax.erf(x * float(1.0 / np.sqrt(2.0))))


def _peer_kernel(xm_ref, a_ref, lb_ref, b_ref, r2_ref, u_ref, vt_ref, x_ref, ga_ref, gain_ref, o_ref,
                 acc_ref, w_ref, h_ref, *, final_norm):
    j = pl.program_id(2)

    @pl.when(j == 0)
    def _():
        acc_ref[...] = jnp.zeros_like(acc_ref)

    tn = xm_ref.shape[1]

    def spread(row):
        return jnp.broadcast_to(row, (BF16_ROWS, LANE)).astype(BF16)[None]

    def slabs(words):
        return pltpu.bitcast(words, BF16).reshape(PEER_KEYS // BF16_ROWS, BF16_ROWS, LANE)

    n_sub = PEER_TE // PEER_SUB
    n_slot = tn // LANE
    rows_per_sub = PEER_SUB // PEER_KEYS

    def first_matmul(sub):
        h_ref[sub % 2] = lax.dot_general(u_ref[pl.ds(sub * PEER_SUB, PEER_SUB), :], xm_ref[0],
                                         (((1,), (1,)), ((), ())), preferred_element_type=F32)

    def second_matmul(sub):
        experts = pl.ds(sub * PEER_SUB, PEER_SUB)
        acc_ref[...] += jnp.dot(vt_ref[:, experts], w_ref[experts, :], preferred_element_type=F32)

    def gate_slice(sub, blk):
        lanes = pl.ds(blk * LANE, LANE)
        lo = blk * LANE
        for grp in range(rows_per_sub // PEER_GROUP):
            row0 = sub * rows_per_sub + grp * PEER_GROUP
            first_rows = [pl.ds(j * (PEER_TE // PEER_KEYS) + row0 + a, 1) for a in range(PEER_GROUP)]
            g = [None] * PEER_GROUP
            for h in range(PEER_HEADS):
                r2 = slabs(r2_ref[0, h, :, lanes])
                b = slabs(b_ref[0, h, :, lanes])
                for a in range(PEER_GROUP):
                    limit = spread(lb_ref[0, h, first_rows[a], :][:, lo:lo + LANE])
                    t = jnp.where(r2 < limit, b, 0.0) * spread(a_ref[0, h, first_rows[a], :][:, lo:lo + LANE])
                    g[a] = t if g[a] is None else g[a] + t
            for a in range(PEER_GROUP):
                local = pl.ds((grp * PEER_GROUP + a) * PEER_KEYS, PEER_KEYS)
                act = _gelu_exact(h_ref[sub % 2, local, lanes]).astype(BF16)
                w_ref[pl.ds((row0 + a) * PEER_KEYS, PEER_KEYS), lanes] = g[a].reshape(PEER_KEYS, LANE) * act

    first_matmul(0)
    for sub in range(n_sub):
        if sub + 1 < n_sub:
            first_matmul(sub + 1)
        if sub >= 1:
            second_matmul(sub - 1)
        for k in range(n_slot):
            gate_slice(sub, k)
    second_matmul(n_sub - 1)

    @pl.when(j == pl.num_programs(2) - 1)
    def _():
        y = x_ref[0] + ga_ref[0] * acc_ref[...].T
        if final_norm:
            y = y * lax.rsqrt(jnp.mean(y * y, axis=-1, keepdims=True) + EPS) * gain_ref[...]
        o_ref[0] = y


def peer_dense(xm, a_t, lb_t, b_t, r2_t, u, vt, x, ga, out_gain=None):
    bsz, t, d = x.shape
    final_norm = out_gain is not None
    gain = (out_gain if final_norm else jnp.ones((d,), F32)).reshape(1, d)
    n_exp = u.shape[0]
    tn = min(t, 512)
    tab = pl.BlockSpec((1, PEER_HEADS, PEER_KEYS, tn), lambda b, i, j: (b, 0, 0, i))
    pair = pl.BlockSpec((1, PEER_HEADS, PEER_KEYS // 2, tn), lambda b, i, j: (b, 0, 0, i))
    return pl.pallas_call(
        functools.partial(_peer_kernel, final_norm=final_norm),
        grid=(bsz, t // tn, n_exp // PEER_TE),
        in_specs=[pl.BlockSpec((1, tn, d), lambda b, i, j: (b, i, 0)),
                  tab, tab, pair, pair,
                  pl.BlockSpec((PEER_TE, d), lambda b, i, j: (j, 0)),
                  pl.BlockSpec((d, PEER_TE), lambda b, i, j: (0, j)),
                  pl.BlockSpec((1, tn, d), lambda b, i, j: (b, i, 0)),
                  pl.BlockSpec((1, 1, d), lambda b, i, j: (b, 0, 0)),
                  pl.BlockSpec((1, d), lambda b, i, j: (0, 0))],
        out_specs=pl.BlockSpec((1, tn, d), lambda b, i, j: (b, i, 0)),
        out_shape=jax.ShapeDtypeStruct((bsz, t, d), F32),
        scratch_shapes=[pltpu.VMEM((d, tn), F32), pltpu.VMEM((PEER_TE, tn), BF16),
                        pltpu.VMEM((2, PEER_SUB, tn), F32)],
        compiler_params=_cparams(("parallel", "parallel", "arbitrary")),
        name="peer_dense",
    )(xm, a_t, lb_t, b_t, r2_t, u, vt, x, ga, gain)


def _extract_top(s, row, n_pick, lowest_row_only, with_rank=True):
    rank = jnp.full(s.shape, float(n_pick), F32) if with_rank else None
    tops = []
    for k in range(n_pick):
        m = jnp.max(s, axis=0, keepdims=True)
        hit = s == m
        if lowest_row_only:
            hit = row == jnp.min(jnp.where(hit, row, s.shape[0]), axis=0, keepdims=True)
        if with_rank:
            rank = jnp.where(hit, float(k), rank)
        s = jnp.where(hit, -jnp.inf, s)
        tops.append(m)
    marked = (rank < float(n_pick)) if with_rank else (s == -jnp.inf)
    n_marked = jnp.sum(jnp.where(marked, 1.0, 0.0), axis=0, keepdims=True)
    n_bad = jnp.sum(jnp.where(n_marked != float(n_pick), 1.0, 0.0))
    return rank, jnp.concatenate(tops, axis=0), n_bad


def _peer_prep_kernel(x_ref, g_ref, sh_ref, sc_ref, wqt_ref, keys_ref,
                      xm_ref, a_ref, lb_ref, b_ref, r2_ref, qt_ref):
    tn = x_ref.shape[1]
    x = x_ref[0]
    y = x * lax.rsqrt(jnp.mean(x * x, axis=-1, keepdims=True) + EPS)
    xm = ((y * g_ref[...]) * (1.0 + sc_ref[0]) + sh_ref[0]).astype(BF16)
    xm_ref[0] = xm
    qt_ref[...] = lax.dot_general(wqt_ref[...], xm, (((1,), (1,)), ((), ())), preferred_element_type=F32)
    half = keys_ref.shape[2]
    row = lax.broadcasted_iota(jnp.int32, (PEER_KEYS, LANE), 0)
    sub = lax.broadcasted_iota(jnp.int32, (8, LANE), 0)
    n_cand_rows = PEER_TOPK + 8 * (PEER_TOPK // 2 - 1) + PEER_TOPK // 2
    crow = lax.broadcasted_iota(jnp.int32, (n_cand_rows, LANE), 0)

    def head_tables(h, exact):
        n_bad = 0.0
        for blk in range(tn // LANE):
            lanes = pl.ds(blk * LANE, LANE)
            s, rank, tops = [], [], []
            for c in range(2):
                qs = qt_ref[pl.ds(pl.multiple_of((2 * h + c) * half, half), half), lanes]
                sc_ = jnp.dot(keys_ref[c], qs, preferred_element_type=F32, precision=lax.Precision.HIGHEST)
                r, t, bad = _extract_top(sc_, row, PEER_TOPK, exact, with_rank=exact or c == 1)
                n_bad = n_bad + bad
                s.append(sc_)
                rank.append(r)
                tops.append(t)
            t1, t2 = tops
            groups = [t1[0:1] + t2]
            for a in range(1, PEER_TOPK // 2):
                n_valid = PEER_TOPK // (a + 1)
                grp = t1[a:a + 1] + t2[0:8]
                groups.append(grp if n_valid >= 8 else jnp.where(sub < n_valid, grp, -jnp.inf))
            groups.append(t1[PEER_TOPK // 2:] + t2[0:1])
            cand = jnp.concatenate(groups, axis=0)
            crank, _, bad = _extract_top(cand, crow, PEER_TOPK, exact)
            n_bad = n_bad + bad
            picked = crank < float(PEER_TOPK)
            z = jnp.sum(jnp.where(picked, jnp.exp(cand - cand[0:1]), 0.0), axis=0, keepdims=True)
            pk = picked.astype(F32)
            counts = [jnp.sum(pk[0:PEER_TOPK], axis=0, keepdims=True)]
            for a in range(1, PEER_TOPK // 2):
                lo = PEER_TOPK + 8 * (a - 1)
                counts.append(jnp.sum(pk[lo:lo + 8], axis=0, keepdims=True))
            lo = PEER_TOPK + 8 * (PEER_TOPK // 2 - 1)
            for a in range(PEER_TOPK // 2):
                counts.append(pk[lo + a:lo + a + 1])
            lb = jnp.zeros((PEER_KEYS, LANE), F32)
            for a in range(PEER_TOPK):
                is_pick = (rank[0] == float(a)) if exact else (s[0] == t1[a:a + 1])
                lb = jnp.where(is_pick, counts[a], lb)
            a_ref[0, h, :, lanes] = jnp.exp(s[0] - t1[0:1])
            lb_ref[0, h, :, lanes] = lb
            b_ref[0, h, :, lanes] = pltpu.bitcast((jnp.exp(s[1] - t2[0:1]) / z).astype(BF16), jnp.uint32)
            r2_ref[0, h, :, lanes] = pltpu.bitcast(rank[1].astype(BF16), jnp.uint32)
        return n_bad

    def head_body(h, carry):
        n_bad = head_tables(h, False)

        @pl.when(n_bad > 0.0)
        def _():
            head_tables(h, True)

        return carry

    lax.fori_loop(0, PEER_HEADS, head_body, 0)


def peer_prep(x, g, shift, scale, wqt, keys):
    bsz, t, d = x.shape
    tn = min(t, 512)
    nq = wqt.shape[0]
    tab_shape = jax.ShapeDtypeStruct((bsz, PEER_HEADS, PEER_KEYS, t), F32)
    tab_spec = pl.BlockSpec((1, PEER_HEADS, PEER_KEYS, tn), lambda b, i: (b, 0, 0, i))
    pair_shape = jax.ShapeDtypeStruct((bsz, PEER_HEADS, PEER_KEYS // 2, t), jnp.uint32)
    pair_spec = pl.BlockSpec((1, PEER_HEADS, PEER_KEYS // 2, tn), lambda b, i: (b, 0, 0, i))
    return pl.pallas_call(
        _peer_prep_kernel,
        grid=(bsz, t // tn),
        in_specs=[pl.BlockSpec((1, tn, d), lambda b, i: (b, i, 0)),
                  pl.BlockSpec((1, d), lambda b, i: (0, 0)),
                  pl.BlockSpec((1, 1, d), lambda b, i: (b, 0, 0)),
                  pl.BlockSpec((1, 1, d), lambda b, i: (b, 0, 0)),
                  pl.BlockSpec((nq, d), lambda b, i: (0, 0)),
                  pl.BlockSpec(keys.shape, lambda b, i: (0, 0, 0))],
        out_specs=[pl.BlockSpec((1, tn, d), lambda b, i: (b, i, 0)), tab_spec, tab_spec, pair_spec, pair_spec],
        out_shape=[jax.ShapeDtypeStruct((bsz, t, d), BF16), tab_shape, tab_shape, pair_shape, pair_shape],
        scratch_shapes=[pltpu.VMEM((nq, tn), F32)],
        compiler_params=_cparams(("parallel", "parallel")),
        name="peer_prep",
    )(x, g.reshape(1, d), shift, scale, wqt, keys)


ROW_TILE = 128
CONV_PAD = 16


def _conv_kernel(p_ref, w_ref, b_ref, lg_ref, lb_ref, o_ref, pad_ref, y_ref):
    t, c = o_ref.shape[1], o_ref.shape[2]
    k_taps = w_ref.shape[0]
    first = CONV_PAD - (k_taps - 1) // 2
    lane_w = 256
    pad_ref[0:CONV_PAD, :] = jnp.zeros((CONV_PAD, c), F32)
    pad_ref[CONV_PAD + t:, :] = jnp.zeros((CONV_PAD, c), F32)

    def glu_body(i, carry):
        rows = pl.ds(pl.multiple_of(i * ROW_TILE, ROW_TILE), ROW_TILE)
        a = p_ref[0, rows, 0:c]
        g = p_ref[0, rows, c:2 * c]
        pad_ref[pl.ds(pl.multiple_of(CONV_PAD + i * ROW_TILE, 8), ROW_TILE), :] = a * jax.nn.sigmoid(g)
        return carry

    lax.fori_loop(0, t // ROW_TILE, glu_body, 0)

    def conv_body(i, carry):
        base = pl.multiple_of(i * ROW_TILE, ROW_TILE)
        for lb in range(c // lane_w):
            lanes = pl.ds(lb * lane_w, lane_w)
            win = pad_ref[pl.ds(base, ROW_TILE + 2 * CONV_PAD), lanes]
            acc = jnp.broadcast_to(b_ref[:, lanes], (ROW_TILE, lane_w))
            for k in range(k_taps):
                acc = acc + w_ref[k:k + 1, lanes] * win[first + k:first + k + ROW_TILE]
            y_ref[:, lanes] = acc
        y = y_ref[...]
        yc = y - jnp.mean(y, axis=-1, keepdims=True)
        z = yc * lax.rsqrt(jnp.mean(yc * yc, axis=-1, keepdims=True) + EPS) * lg_ref[...] + lb_ref[...]
        o_ref[0, pl.ds(base, ROW_TILE), :] = z * jax.nn.sigmoid(z)
        return carry

    lax.fori_loop(0, t // ROW_TILE, conv_body, 0)


def conv_branch(proj, w, b, ln_g, ln_b):
    bsz, t, _ = proj.shape
    k_taps, c = w.shape
    vec = pl.BlockSpec((1, c), lambda i: (0, 0))
    return pl.pallas_call(
        _conv_kernel,
        grid=(bsz,),
        in_specs=[pl.BlockSpec((1, t, 2 * c), lambda i: (i, 0, OFF_CONV // (2 * c))),
                  pl.BlockSpec((k_taps, c), lambda i: (0, 0)), vec, vec, vec],
        out_specs=pl.BlockSpec((1, t, c), lambda i: (i, 0, 0)),
        out_shape=jax.ShapeDtypeStruct((bsz, t, c), F32),
        scratch_shapes=[pltpu.VMEM((t + 2 * CONV_PAD, c), F32), pltpu.VMEM((ROW_TILE, c), F32)],
        compiler_params=_cparams(("parallel",)),
        name="conv_branch",
    )(proj, w, b.reshape(1, c), ln_g.reshape(1, c), ln_b.reshape(1, c))


def rope_tables(n_tok, head_dim):
    nf = head_dim // 4
    lane = np.arange(LANE)
    d_idx = lane % head_dim
    by_col = d_idx >= head_dim // 2
    freqs = ROPE_BASE ** (-(d_idx % nf).astype(np.float32) / nf)
    tok = np.arange(n_tok)
    pos = np.where(by_col[None, :], (tok % GRID_W)[:, None], (tok // GRID_W)[:, None]).astype(np.float32)
    ang = jnp.asarray(pos) * jnp.asarray(freqs, F32)[None, :]
    sign = np.where(d_idx % (2 * nf) < nf, -1.0, 1.0).astype(np.float32)
    return jnp.cos(ang), jnp.sin(ang) * sign[None, :]


def _rope(x, cos, sin_signed, nf):
    lane = lax.broadcasted_iota(jnp.int32, x.shape, 1)
    partner = jnp.where(lane % (2 * nf) < nf, pltpu.roll(x, LANE - nf, 1), pltpu.roll(x, nf, 1))
    return x * cos + partner * sin_signed


FINISH_ROWS = 256


def _head_norm_rows(o):
    oc = o - jnp.mean(o, axis=-1, keepdims=True)
    return oc * lax.rsqrt(jnp.mean(oc * oc, axis=-1, keepdims=True) + EPS)


def _ret_kernel(dec_ref, ql_ref, kl_ref, vl_ref, gl_ref, qc_ref, kc_ref, vc_ref, gc_ref, cos_ref, sin_ref,
                yl_ref, yc_ref, ol_ref, oc_ref, olb_ref, ocb_ref):
    h = pl.program_id(1)
    n_l, n_c = ql_ref.shape[1] // CHUNK, qc_ref.shape[1] // CHUNK
    d = ql_ref.shape[2]
    row = lax.broadcasted_iota(jnp.int32, (CHUNK, CHUNK), 0).astype(F32)
    col = lax.broadcasted_iota(jnp.int32, (CHUNK, CHUNK), 1).astype(F32)
    lg_f = jax.nn.log_sigmoid(jnp.full((CHUNK, CHUNK), dec_ref[0, h], F32))
    lg_b = jax.nn.log_sigmoid(jnp.full((CHUNK, CHUNK), dec_ref[1, h], F32))
    dmat = (jnp.where(row >= col, jnp.exp((row - col) * lg_f), 0.0)
            + jnp.where(col >= row, jnp.exp((col - row) * lg_b), 0.0))
    qdec_f, kdec_f, cdec_f = jnp.exp((row + 1.0) * lg_f), jnp.exp((CHUNK - 1.0 - row) * lg_f), jnp.exp(CHUNK * lg_f)
    qdec_b, kdec_b, cdec_b = jnp.exp((CHUNK - row) * lg_b), jnp.exp(row * lg_b), jnp.exp(CHUNK * lg_b)
    scale = d ** -0.5
    nt = (((1,), (1,)), ((), ()))
    tn = (((0,), (0,)), ((), ()))

    def load(refs, c, rotate):
        q_ref, k_ref, v_ref = refs
        rows = pl.ds(pl.multiple_of(c * CHUNK, CHUNK), CHUNK)
        q, k, v = q_ref[0, rows, :], k_ref[0, rows, :], v_ref[0, rows, :]
        if rotate:
            cos, sin = cos_ref[rows, :], sin_ref[rows, :]
            q, k = _rope(q, cos, sin, d // 4), _rope(k, cos, sin, d // 4)
        return q.astype(BF16), k * scale, v.astype(BF16), rows

    def both(refs, of_ref, ob_ref, rotate, n):
        def body(i, carry):
            s_f, s_b = carry
            q, k, v, rows = load(refs, i, rotate)
            att = lax.dot_general(q, k.astype(BF16), nt, preferred_element_type=F32) * dmat
            o = jnp.dot(att.astype(BF16), v, preferred_element_type=F32)
            of_ref[rows, :] = o + qdec_f * jnp.dot(q, s_f.astype(BF16), preferred_element_type=F32)
            s_f = cdec_f * s_f + lax.dot_general((k * kdec_f).astype(BF16), v, tn, preferred_element_type=F32)
            q, k, v, rows = load(refs, n - 1 - i, rotate)
            ob_ref[rows, :] = qdec_b * jnp.dot(q, s_b.astype(BF16), preferred_element_type=F32)
            s_b = cdec_b * s_b + lax.dot_general((k * kdec_b).astype(BF16), v, tn, preferred_element_type=F32)
            return s_f, s_b
        return body

    lat, ctx = (ql_ref, kl_ref, vl_ref), (qc_ref, kc_ref, vc_ref)
    s0 = jnp.zeros((d, d), F32)
    state = lax.fori_loop(0, n_c, both(ctx, oc_ref, ocb_ref, False, n_c), (s0, s0))
    lax.fori_loop(0, n_l, both(lat, ol_ref, olb_ref, True, n_l), state, unroll=2)

    def finish(of_ref, ob_ref, g_ref, y_ref):
        def body(c, carry):
            rows = pl.ds(pl.multiple_of(c * FINISH_ROWS, FINISH_ROWS), FINISH_ROWS)
            g = g_ref[0, rows, :]
            y_ref[0, rows, :] = _head_norm_rows(of_ref[rows, :] + ob_ref[rows, :]) * (g * jax.nn.sigmoid(g))
            return carry
        lax.fori_loop(0, of_ref.shape[0] // FINISH_ROWS, body, 0)

    finish(ol_ref, olb_ref, gl_ref, yl_ref)
    finish(oc_ref, ocb_ref, gc_ref, yc_ref)


def retention_branch(proj_l, proj_c, ret_decay, cos, sin):
    bsz, t_l, _ = proj_l.shape
    t_c = proj_c.shape[1]
    d = BR_W // RET_HEADS
    blk0 = OFF_RET // d

    def col(t, part):
        return pl.BlockSpec((1, t, d), functools.partial(lambda b, h, dec, p: (b, 0, blk0 + p * RET_HEADS + h), p=part))

    table = pl.BlockSpec((t_l, LANE), lambda b, h, dec: (0, 0))
    grid_spec = pltpu.PrefetchScalarGridSpec(
        num_scalar_prefetch=1,
        grid=(bsz, RET_HEADS),
        in_specs=[col(t_l, p) for p in range(4)] + [col(t_c, p) for p in range(4)] + [table, table],
        out_specs=[pl.BlockSpec((1, t_l, d), lambda b, h, dec: (b, 0, h)),
                   pl.BlockSpec((1, t_c, d), lambda b, h, dec: (b, 0, h))],
        scratch_shapes=[pltpu.VMEM((t_l, d), F32), pltpu.VMEM((t_c, d), F32),
                        pltpu.VMEM((t_l, d), F32), pltpu.VMEM((t_c, d), F32)])
    return pl.pallas_call(
        _ret_kernel,
        grid_spec=grid_spec,
        out_shape=[jax.ShapeDtypeStruct((bsz, t_l, BR_W), F32), jax.ShapeDtypeStruct((bsz, t_c, BR_W), F32)],
        compiler_params=_cparams(("parallel", "parallel")),
        name="retention_branch",
    )(ret_decay, proj_l, proj_l, proj_l, proj_l, proj_c, proj_c, proj_c, proj_c, cos, sin)


SHORT_PAD = 8
MLSTM_HP = 1


def _mlstm_kernel(ql_ref, kl_ref, vl_ref, ol_ref, gl_ref, qc_ref, kc_ref, vc_ref, oc_ref, gc_ref,
                  wq_ref, wk_ref, bq_ref, bk_ref, gb_ref, yl_ref, yc_ref,
                  pad_ref, qsl_ref, ksl_ref, qsc_ref, ksc_ref, hl_ref, hc_ref, hlb_ref, hcb_ref):
    head0 = pl.program_id(1) * MLSTM_HP
    t_l, t_c = ql_ref.shape[1], qc_ref.shape[1]
    width = ql_ref.shape[2]
    d = width // MLSTM_HP
    row = lax.broadcasted_iota(jnp.int32, (CHUNK, CHUNK), 0)
    col = lax.broadcasted_iota(jnp.int32, (CHUNK, CHUNK), 1)
    eye = row == col
    scale = d ** -0.5
    nt = (((1,), (1,)), ((), ()))
    tn = (((0,), (0,)), ((), ()))

    def short_conv(x_ref, w_ref, b_ref, out_ref, t):
        k_taps = w_ref.shape[0]
        first = SHORT_PAD - (k_taps - 1) // 2
        pad_ref[0:SHORT_PAD, :] = jnp.zeros((SHORT_PAD, width), F32)
        pad_ref[SHORT_PAD + t:SHORT_PAD + t + SHORT_PAD, :] = jnp.zeros((SHORT_PAD, width), F32)

        def copy(i, carry):
            pad_ref[pl.ds(pl.multiple_of(SHORT_PAD + i * CHUNK, 8), CHUNK), :] = \
                x_ref[0, pl.ds(pl.multiple_of(i * CHUNK, CHUNK), CHUNK), :]
            return carry

        lax.fori_loop(0, t // CHUNK, copy, 0)

        def conv(i, carry):
            base = pl.multiple_of(i * CHUNK, CHUNK)
            win = pad_ref[pl.ds(base, CHUNK + 2 * SHORT_PAD), :]
            acc = jnp.broadcast_to(b_ref[...], (CHUNK, width))
            for k in range(k_taps):
                acc = acc + w_ref[k:k + 1, :] * win[first + k:first + k + CHUNK]
            out_ref[pl.ds(base, CHUNK), :] = acc * jax.nn.sigmoid(acc)
            return carry

        lax.fori_loop(0, t // CHUNK, conv, 0)

    short_conv(ql_ref, wq_ref, bq_ref, qsl_ref, t_l)
    short_conv(kl_ref, wk_ref, bk_ref, ksl_ref, t_l)
    short_conv(qc_ref, wq_ref, bq_ref, qsc_ref, t_c)
    short_conv(kc_ref, wk_ref, bk_ref, ksc_ref, t_c)

    def gate_cols(g_ref, rows, i_lane, f_lane):
        x = g_ref[0, rows, :] + gb_ref[...]
        i_col = jnp.sum(jnp.where(col == i_lane, x, 0.0), axis=1, keepdims=True)
        f_col = jax.nn.log_sigmoid(jnp.sum(jnp.where(col == f_lane, x, 0.0), axis=1, keepdims=True))
        i_row = jnp.sum(jnp.where(eye, i_col, 0.0), axis=0, keepdims=True)
        f_row = jnp.sum(jnp.where(eye, f_col, 0.0), axis=0, keepdims=True)
        return i_col, f_col, i_row, f_row

    def chunk_step(seq, c, state, forward, hk):
        qs_ref, ks_ref, v_ref, g_ref = seq[:4]
        h_ref = seq[4] if forward else seq[5]
        cmat, nvec, m = state
        rows = pl.ds(pl.multiple_of(c * CHUNK, CHUNK), CHUNK)
        lanes = pl.ds(hk * d, d)
        h = head0 + hk
        q = qs_ref[rows, lanes]
        k = ks_ref[rows, lanes] * scale
        qb, vb = q.astype(BF16), v_ref[0, rows, lanes].astype(BF16)
        i_col, f_col, i_row, f_row = gate_cols(g_ref, rows, (0 if forward else 2 * MLSTM_HEADS) + h,
                                               (MLSTM_HEADS if forward else 3 * MLSTM_HEADS) + h)
        seen = (col <= row) if forward else (col >= row)
        seen_t = (row <= col) if forward else (row >= col)
        b_col = jnp.sum(jnp.where(seen, f_row, 0.0), axis=1, keepdims=True)
        b_row = jnp.sum(jnp.where(seen_t, f_col, 0.0), axis=0, keepdims=True)
        b_end = jnp.sum(f_row, axis=1, keepdims=True)
        logw = jnp.where(seen, b_col - b_row + i_row, -jnp.inf)
        inter = b_col + m
        m_t = jnp.maximum(inter, jnp.max(logw, axis=1, keepdims=True))
        s = lax.dot_general(qb, k.astype(BF16), nt, preferred_element_type=F32) * jnp.exp(logw - m_t)
        w_prev = jnp.exp(inter - m_t)
        num = (jnp.dot(s.astype(BF16), vb, preferred_element_type=F32)
               + w_prev * jnp.dot(qb, cmat.astype(BF16), preferred_element_type=F32))
        den = jnp.sum(s, axis=1, keepdims=True) + w_prev * jnp.sum(q * nvec, axis=1, keepdims=True)
        hc = num * (1.0 / jnp.maximum(jnp.abs(den), jnp.exp(-m_t)))
        h_ref[rows, lanes] = hc
        log_wk = b_end - b_col + i_col
        m_new = jnp.maximum(b_end + m, jnp.max(log_wk, axis=0, keepdims=True))
        decay = jnp.exp(b_end + m - m_new)
        kw = k * jnp.exp(log_wk - m_new)
        cmat = decay * cmat + lax.dot_general(kw.astype(BF16), vb, tn, preferred_element_type=F32)
        nvec = decay * nvec + jnp.sum(kw, axis=0, keepdims=True)
        return cmat, nvec, m_new

    lat = (qsl_ref, ksl_ref, vl_ref, gl_ref, hl_ref, hlb_ref)
    ctx = (qsc_ref, ksc_ref, vc_ref, gc_ref, hc_ref, hcb_ref)
    n_l, n_c = t_l // CHUNK, t_c // CHUNK
    state0 = (jnp.zeros((d, d), F32), jnp.zeros((1, d), F32), jnp.full((1, 1), NEG_INIT, F32))

    def both(seq, n):
        def body(i, carry):
            fwd = tuple(chunk_step(seq, i, carry[0][hk], True, hk) for hk in range(MLSTM_HP))
            bwd = tuple(chunk_step(seq, n - 1 - i, carry[1][hk], False, hk) for hk in range(MLSTM_HP))
            return fwd, bwd
        return body

    st = lax.fori_loop(0, n_c, both(ctx, n_c), ((state0,) * MLSTM_HP, (state0,) * MLSTM_HP))
    lax.fori_loop(0, n_l, both(lat, n_l), st, unroll=2)

    def finish(hf_ref, hb_ref, o_ref, y_ref):
        def body(c, carry):
            rows = pl.ds(pl.multiple_of(c * FINISH_ROWS, FINISH_ROWS), FINISH_ROWS)
            for hk in range(MLSTM_HP):
                lanes = pl.ds(hk * d, d)
                y_ref[0, rows, lanes] = (_head_norm_rows(hf_ref[rows, lanes] + hb_ref[rows, lanes])
                                         * jax.nn.sigmoid(o_ref[0, rows, lanes]))
            return carry
        lax.fori_loop(0, hf_ref.shape[0] // FINISH_ROWS, body, 0)

    finish(hl_ref, hlb_ref, ol_ref, yl_ref)
    finish(hc_ref, hcb_ref, oc_ref, yc_ref)


def mlstm_branch(proj_l, proj_c, conv_w, conv_b, gate_b):
    bsz, t_l, _ = proj_l.shape
    t_c = proj_c.shape[1]
    w = MLSTM_HP * (BR_W // MLSTM_HEADS)
    blk0 = OFF_MLSTM // w
    n_grp = MLSTM_HEADS // MLSTM_HP
    k_taps = conv_w.shape[0]

    def col(t, part):
        return pl.BlockSpec((1, t, w), functools.partial(lambda b, h, p: (b, 0, blk0 + p * n_grp + h), p=part))

    def gates(t):
        return pl.BlockSpec((1, t, LANE), lambda b, h: (b, 0, OFF_MGATE // LANE))

    gb = jnp.zeros((1, LANE), F32).at[0, :MLSTM_GATES].set(gate_b.reshape(-1))
    return pl.pallas_call(
        _mlstm_kernel,
        grid=(bsz, n_grp),
        in_specs=[col(t_l, p) for p in range(4)] + [gates(t_l)] + [col(t_c, p) for p in range(4)] + [gates(t_c)] + [
            pl.BlockSpec((k_taps, w), lambda b, h: (0, h)),
            pl.BlockSpec((k_taps, w), lambda b, h: (0, n_grp + h)),
            pl.BlockSpec((1, w), lambda b, h: (0, h)),
            pl.BlockSpec((1, w), lambda b, h: (0, n_grp + h)),
            pl.BlockSpec((1, LANE), lambda b, h: (0, 0))],
        out_specs=[pl.BlockSpec((1, t_l, w), lambda b, h: (b, 0, h)),
                   pl.BlockSpec((1, t_c, w), lambda b, h: (b, 0, h))],
        out_shape=[jax.ShapeDtypeStruct((bsz, t_l, BR_W), F32), jax.ShapeDtypeStruct((bsz, t_c, BR_W), F32)],
        scratch_shapes=[pltpu.VMEM((t_l + 2 * SHORT_PAD, w), F32),
                        pltpu.VMEM((t_l, w), F32), pltpu.VMEM((t_l, w), F32),
                        pltpu.VMEM((t_c, w), F32), pltpu.VMEM((t_c, w), F32),
                        pltpu.VMEM((t_l, w), F32), pltpu.VMEM((t_c, w), F32),
                        pltpu.VMEM((t_l, w), F32), pltpu.VMEM((t_c, w), F32)],
        compiler_params=_cparams(("parallel", "parallel")),
        name="mlstm_branch",
    )(proj_l, proj_l, proj_l, proj_l, proj_l, proj_c, proj_c, proj_c, proj_c, proj_c,
      conv_w, conv_w, conv_b.reshape(1, -1), conv_b.reshape(1, -1), gb)


def _attn_kernel(sink_ref, q_ref, *refs, local):
    if local:
        kp_ref, kc_ref, kn_ref, vp_ref, vc_ref, vn_ref, kx_ref, vx_ref, cos_ref, sin_ref, o_ref, k_scr, v_scr = refs
    else:
        kx_ref, vx_ref, o_ref, k_scr, v_scr = refs
    i = pl.program_id(1)
    nq = pl.num_programs(1)
    t_x = kx_ref.shape[1]
    n_loc = 3 * BLOCK if local else 0
    n_keys = n_loc + t_x
    nf = ATTN_DIM // 4
    half = LANE // 2

    def put(dst, lo, x):
        dst[0, lo:lo + x.shape[0], :] = x.astype(BF16)
        dst[1, lo:lo + x.shape[0], :] = pltpu.roll(x, half, 1).astype(BF16)

    if local:
        for j, (k_ref, v_ref) in enumerate(((kp_ref, vp_ref), (kc_ref, vc_ref), (kn_ref, vn_ref))):
            blk = jnp.clip(i + (j - 1), 0, nq - 1)
            rows = pl.ds(pl.multiple_of(blk * BLOCK, BLOCK), BLOCK)
            put(k_scr, j * BLOCK, _rope(k_ref[0], cos_ref[rows, :], sin_ref[rows, :], nf))
            put(v_scr, j * BLOCK, v_ref[0])
    put(k_scr, n_loc, kx_ref[0])
    put(v_scr, n_loc, vx_ref[0])

    lane = lax.broadcasted_iota(jnp.int32, (BLOCK, LANE), 1)
    if local:
        rel = (lax.broadcasted_iota(jnp.int32, (BLOCK, n_keys), 1)
               - lax.broadcasted_iota(jnp.int32, (BLOCK, n_keys), 0))
        kpos = (i - 1) * BLOCK + lax.broadcasted_iota(jnp.int32, (BLOCK, n_keys), 1)
        ok = (rel >= 0) & (rel <= 2 * WINDOW) & (kpos >= 0) & (kpos < nq * BLOCK)
        ok = ok | (lax.broadcasted_iota(jnp.int32, (BLOCK, n_keys), 1) >= n_loc)
        bias = jnp.where(ok, 0.0, -jnp.inf)
        qrows = pl.ds(pl.multiple_of(i * BLOCK, BLOCK), BLOCK)
        cos_q, sin_q = cos_ref[qrows, :], sin_ref[qrows, :]
    nt = (((1,), (1,)), ((), ()))
    members = ([], [])
    for p in range(ATTN_HEADS // 2):
        qp = q_ref[0, :, p * LANE:(p + 1) * LANE]
        if local:
            qp = _rope(qp, cos_q, sin_q, nf)
        qp = qp * (ATTN_DIM ** -0.5)
        for sub in range(2):
            hd = 2 * p + sub
            kv_half = hd // (ATTN_HEADS // ATTN_KV_HEADS)
            sel = (lane >= half) if sub else (lane < half)
            members[0 if kv_half == sub else 1].append((hd, jnp.where(sel, qp, 0.0).astype(BF16)))
    order = [hd for grp in members for hd, _ in grp]
    s = jnp.concatenate(
        [lax.dot_general(jnp.concatenate([qm for _, qm in grp], axis=0), k_scr[swap], nt,
                         preferred_element_type=F32) for swap, grp in enumerate(members)],
        axis=0).reshape(ATTN_HEADS, BLOCK, n_keys)
    sink = jnp.stack([jnp.full((BLOCK, 1), sink_ref[hd], F32) for hd in order])
    if local:
        s = s + bias[None]
    m = jnp.maximum(jnp.max(s, axis=2, keepdims=True), sink)
    e = jnp.exp(s - m)
    inv = 1.0 / (jnp.sum(e, axis=2, keepdims=True) + jnp.exp(sink - m))
    prob = (e * inv).astype(BF16)
    outs = [None] * ATTN_HEADS
    first = 0
    for swap, grp in enumerate(members):
        o = jnp.dot(prob[first:first + len(grp)].reshape(len(grp) * BLOCK, n_keys), v_scr[swap],
                    preferred_element_type=F32)
        for i, (hd, _) in enumerate(grp):
            outs[hd] = o[i * BLOCK:(i + 1) * BLOCK]
        first += len(grp)
    for p in range(ATTN_HEADS // 2):
        o_ref[0, :, p * LANE:(p + 1) * LANE] = jnp.where(lane < half, outs[2 * p], outs[2 * p + 1])


def attention_branch(proj_q, proj_c, sink, cos, sin, local):
    bsz, t, _ = proj_q.shape
    t_c = proj_c.shape[1]
    nq = t // BLOCK
    n_qcol = ATTN_HEADS * ATTN_DIM
    q_blk = OFF_ATTN // n_qcol
    k_blk = (OFF_ATTN + n_qcol) // LANE
    v_blk = k_blk + 1
    n_keys = (3 * BLOCK if local else 0) + t_c

    def kv(blk, shift):
        return pl.BlockSpec((1, BLOCK, LANE), lambda b, i, s: (b, jnp.clip(i + shift, 0, nq - 1), blk))

    in_specs = [pl.BlockSpec((1, BLOCK, n_qcol), lambda b, i, s: (b, i, q_blk))]
    args = [proj_q]
    if local:
        in_specs += [kv(k_blk, -1), kv(k_blk, 0), kv(k_blk, 1), kv(v_blk, -1), kv(v_blk, 0), kv(v_blk, 1)]
        args += [proj_q] * 6
    in_specs += [pl.BlockSpec((1, t_c, LANE), lambda b, i, s: (b, 0, k_blk)),
                 pl.BlockSpec((1, t_c, LANE), lambda b, i, s: (b, 0, v_blk))]
    args += [proj_c, proj_c]
    if local:
        in_specs += [pl.BlockSpec((t, LANE), lambda b, i, s: (0, 0))] * 2
        args += [cos, sin]
    grid_spec = pltpu.PrefetchScalarGridSpec(
        num_scalar_prefetch=1,
        grid=(bsz, nq),
        in_specs=in_specs,
        out_specs=pl.BlockSpec((1, BLOCK, n_qcol), lambda b, i, s: (b, i, 0)),
        scratch_shapes=[pltpu.VMEM((2, n_keys, LANE), BF16), pltpu.VMEM((2, n_keys, LANE), BF16)])
    return pl.pallas_call(
        functools.partial(_attn_kernel, local=local),
        grid_spec=grid_spec,
        out_shape=jax.ShapeDtypeStruct((bsz, t, n_qcol), F32),
        compiler_params=_cparams(("parallel", "parallel")),
        name="attention_branch",
    )(sink, *args)


def _pack_w_in(w):
    d = w.shape[0]
    used = w.shape[1] - MLSTM_GATES + LANE
    cols = [w[:, :REF_MGATE_OFF], w[:, REF_GATE_OFF:], w[:, REF_ATTN_OFF:REF_GATE_OFF],
            w[:, REF_MGATE_OFF:REF_ATTN_OFF],
            jnp.zeros((d, _round_up(used, PROJ_TILE_N) - w.shape[1]), w.dtype)]
    return jnp.concatenate(cols, axis=1).astype(BF16)


def kernel(x, c, ctx, c_ctx, mod_w, mod_b, norm1_g, norm2_g, w_in, conv_w, conv_b, conv_ln_g, conv_ln_b,
           ret_decay, mlstm_conv_w, mlstm_conv_b, mlstm_gate_b, attn_sink, w_branch, w_out,
           peer_wq, peer_keys, peer_u, peer_v, final_g):
    bsz, seq, d = x.shape
    assert d == D_MODEL
    depth = mod_w.shape[0]
    cos_r, sin_r = rope_tables(seq, BR_W // RET_HEADS)
    cos_a, sin_a = rope_tables(seq, ATTN_DIM)
    n_cond = _round_up(bsz + 1, 8)
    cond = jnp.concatenate([c, c_ctx[None], jnp.zeros((n_cond - bsz - 1, d), F32)], axis=0)
    for l in range(depth):
        last = l == depth - 1
        mod = mod_vectors(cond, mod_w[l], mod_b[l])
        sh1, sc1, ga1, sh2, sc2, ga2 = (m[:, None, :] for m in jnp.split(mod[:bsz], 6, axis=-1))
        csh1, csc1, cga1, csh2, csc2, cga2 = (m[None] for m in jnp.split(mod[bsz:bsz + 1], 6, axis=-1))
        ctx_shape = ctx.shape
        ctx_flat = ctx.reshape(1, -1, d)
        w_in_p = _pack_w_in(w_in[l])
        proj_l = norm_matmul(x, norm1_g[l], sh1, sc1, w_in_p)
        proj_c_flat = norm_matmul(ctx_flat, norm1_g[l], csh1, csc1, w_in_p)
        proj_c = proj_c_flat.reshape(ctx_shape[:2] + (-1,))
        ret_l, ret_c = retention_branch(proj_l, proj_c, ret_decay[l], cos_r, sin_r)
        mls_l, mls_c = mlstm_branch(proj_l, proj_c, mlstm_conv_w[l], mlstm_conv_b[l], mlstm_gate_b[l])
        conv_args = (conv_w[l], conv_b[l], conv_ln_g[l], conv_ln_b[l])
        ys_l = (conv_branch(proj_l, *conv_args), ret_l, mls_l,
                attention_branch(proj_l, proj_c, attn_sink[l], cos_a, sin_a, True))
        if not last:
            ys_c = (conv_branch(proj_c, *conv_args), ret_c, mls_c,
                    attention_branch(proj_c, proj_c, attn_sink[l], cos_a, sin_a, False))
        wb, wo = w_branch[l].astype(BF16), w_out[l].astype(BF16)
        wqt = peer_wq[l].T.astype(BF16)
        u = peer_u[l].astype(BF16)
        vt = peer_v[l].T.astype(BF16)

        def peer(xx, sh, sc, ga, out_gain=None):
            xm, *tabs = peer_prep(xx, norm2_g[l], sh, sc, wqt, peer_keys[l])
            return peer_dense(xm, *tabs, u, vt, xx, ga, out_gain)

        x = merge(ys_l, proj_l, wb, wo, x, ga1)
        x = peer(x, sh2, sc2, ga2, final_g if last else None)
        if not last:
            ys_c_flat = tuple(y.reshape(1, -1, y.shape[-1]) for y in ys_c)
            ctx_flat = merge(ys_c_flat, proj_c_flat, wb, wo, ctx_flat, cga1)
            ctx = peer(ctx_flat, csh2, csc2, cga2).reshape(ctx_shape)
    return x
```
